```python
import math
import jax, jax.numpy as jnp
from jax import lax
import numpy as np

D_MODEL = 1024
BATCH = 4
SEQ = 4096
DEPTH = 1
DEC_BATCH = 32
DEC_SEQ = 1
PAST_LEN = 16384
PAGE_SIZE = 128

N_HEADS_A = 8
HEAD_DIM_A = 64
D_ATTN = N_HEADS_A * HEAD_DIM_A
MOBA_BLOCK = 256
MOBA_TOPK = 3
Q_CHUNK = 32
D_SSM = 512
SSM_GROUP = 16
N_SSM_GROUPS = D_SSM // SSM_GROUP
SSM_STATE = 64
N_MEM = 256
N_HEADS_X = 4
HEAD_DIM_X = 128
D_XATTN = N_HEADS_X * HEAD_DIM_X
N_EXPERT_GROUPS = 4
EXPERTS_PER_GROUP = 4
N_EXPERTS = N_EXPERT_GROUPS * EXPERTS_PER_GROUP
EXPERT_TOPK = 2
D_FF_EXPERT = 256
D_IN_PROJ = 3 * D_ATTN + D_SSM + 2 * D_MODEL
RMS_EPS = 1e-6
NEG_INF = -1e30

kernel_name = 'hybrid_moba_s5_hmoe_decode_step'


def rmsnorm(x, g):
    x32 = x.astype(jnp.float32)
    y = x32 * lax.rsqrt(jnp.mean(x32 * x32, axis=-1, keepdims=True) + RMS_EPS)
    return (y * g.astype(jnp.float32)).astype(x.dtype)


def moba_attention(q, k, v, q_start):
    b, sq, h, hd = q.shape
    lk = k.shape[1]
    nb = -(-lk // MOBA_BLOCK)
    pad = nb * MOBA_BLOCK - lk
    k = jnp.pad(k, ((0, 0), (0, pad), (0, 0), (0, 0)))
    v = jnp.pad(v, ((0, 0), (0, pad), (0, 0), (0, 0)))
    kb = k.reshape(b, nb, MOBA_BLOCK, h, hd).transpose(0, 3, 1, 2, 4)
    vb = v.reshape(b, nb, MOBA_BLOCK, h, hd).transpose(0, 3, 1, 2, 4)
    kmean = jnp.mean(kb.astype(jnp.float32), axis=3)
    n_sel = min(MOBA_TOPK, nb)
    qc = Q_CHUNK if sq % Q_CHUNK == 0 else sq
    n_chunks = sq // qc
    scale = hd ** -0.5
    qs = q.reshape(b, n_chunks, qc, h, hd).transpose(1, 0, 3, 2, 4)
    qpos = (q_start + jnp.arange(sq, dtype=jnp.int32)).reshape(n_chunks, qc)
    gather = jax.vmap(jax.vmap(lambda blocks, idx: blocks[idx]))
    blk_ids = jnp.arange(nb, dtype=jnp.int32)
    in_blk = jnp.arange(MOBA_BLOCK, dtype=jnp.int32)

    def chunk(args):
        qq, pos = args
        q32 = qq.astype(jnp.float32)
        own = pos // MOBA_BLOCK
        s_blk = jnp.einsum('bhqd,bhnd->bhqn', q32, kmean)
        s_blk = jnp.where(blk_ids[None, :] < own[:, None], s_blk, NEG_INF)
        _, sel = lax.top_k(s_blk, n_sel)
        sel_ok = jnp.arange(n_sel)[None, :] < own[:, None]
        own_idx = jnp.broadcast_to(own[None, None, :, None], (b, h, qc, 1)).astype(sel.dtype)
        idx = jnp.concatenate([sel, own_idx], axis=-1)
        kg = gather(kb, idx)
        vg = gather(vb, idx)
        s = jnp.einsum('bhqd,bhqjnd->bhqjn', q32, kg.astype(jnp.float32)) * scale
        kpos = idx[..., None] * MOBA_BLOCK + in_blk
        slot_ok = jnp.concatenate([sel_ok, jnp.ones((qc, 1), dtype=bool)], axis=-1)
        mask = (kpos <= pos[:, None, None]) & slot_ok[:, :, None]
        s = jnp.where(mask, s, NEG_INF)
        p = jax.nn.softmax(s.reshape(b, h, qc, -1), axis=-1).reshape(s.shape)
        o = jnp.einsum('bhqjn,bhqjnd->bhqd', p, vg.astype(jnp.float32))
        return o.astype(q.dtype)

    out = lax.map(chunk, (qs, qpos))
    return out.transpose(1, 0, 3, 2, 4).reshape(b, sq, h * hd)


def _cmul_combine(e1, e2):
    ar1, ai1, br1, bi1 = e1
    ar2, ai2, br2, bi2 = e2
    return (ar1 * ar2 - ai1 * ai2,
            ar1 * ai2 + ai1 * ar2,
            ar2 * br1 - ai2 * bi1 + br2,
            ar2 * bi1 + ai2 * br1 + bi2)


def s5_scan(u, h0_re, h0_im, lam_re, lam_im, log_dt, b_re, b_im, c_re, c_im, d):
    bsz, s, _ = u.shape
    u32 = u.astype(jnp.float32).reshape(bsz, s, N_SSM_GROUPS, SSM_GROUP)
    lr = jnp.minimum(lam_re.astype(jnp.float32), -1e-4)
    li = lam_im.astype(jnp.float32)
    dt = jnp.exp(log_dt.astype(jnp.float32))[:, None]
    mag = jnp.exp(lr * dt)
    ar = mag * jnp.cos(li * dt)
    ai = mag * jnp.sin(li * dt)
    den = lr * lr + li * li
    nr = ar - 1.0
    cr = (nr * lr + ai * li) / den
    ci = (ai * lr - nr * li) / den
    br = b_re.astype(jnp.float32)
    bi = b_im.astype(jnp.float32)
    bbar_re = cr[..., None] * br - ci[..., None] * bi
    bbar_im = cr[..., None] * bi + ci[..., None] * br
    bu_re = jnp.einsum('bsgh,gph->bsgp', u32, bbar_re)
    bu_im = jnp.einsum('bsgh,gph->bsgp', u32, bbar_im)
    h0r = h0_re.astype(jnp.float32)
    h0i = h0_im.astype(jnp.float32)
    bu_re = bu_re.at[:, 0].add(ar * h0r - ai * h0i)
    bu_im = bu_im.at[:, 0].add(ar * h0i + ai * h0r)
    a_re = jnp.broadcast_to(ar, bu_re.shape)
    a_im = jnp.broadcast_to(ai, bu_im.shape)
    _, _, xr, xi = lax.associative_scan(_cmul_combine, (a_re, a_im, bu_re, bu_im), axis=1)
    y = (jnp.einsum('bsgp,ghp->bsgh', xr, c_re.astype(jnp.float32))
         - jnp.einsum('bsgp,ghp->bsgh', xi, c_im.astype(jnp.float32))
         + d.astype(jnp.float32).reshape(N_SSM_GROUPS, SSM_GROUP) * u32)
    return y.reshape(bsz, s, D_SSM).astype(u.dtype), xr[:, -1], xi[:, -1]


def memory_kv(mem, g_mem, w_xk, w_xv):
    b = mem.shape[0]
    mn = rmsnorm(mem, g_mem)
    mk = (mn @ w_xk).reshape(b, N_MEM, N_HEADS_X, HEAD_DIM_X)
    mv = (mn @ w_xv).reshape(b, N_MEM, N_HEADS_X, HEAD_DIM_X)
    return mk, mv


def cross_attention(xn, mk, mv, w_xq, w_xo):
    b, s, _ = xn.shape
    q = (xn @ w_xq).reshape(b, s, N_HEADS_X, HEAD_DIM_X)
    sc = jnp.einsum('bshd,bmhd->bhsm', q.astype(jnp.float32), mk.astype(jnp.float32)) * (HEAD_DIM_X ** -0.5)
    p = jax.nn.softmax(sc, axis=-1)
    o = jnp.einsum('bhsm,bmhd->bshd', p, mv.astype(jnp.float32)).astype(xn.dtype)
    return o.reshape(b, s, D_XATTN) @ w_xo


def hier_moe(xn, w_group, b_group, w_erouter, b_erouter, w1, w3, w2):
    b, s, d = xn.shape
    t = xn.reshape(-1, d)
    gp = jax.nn.softmax((t @ w_group).astype(jnp.float32) + b_group.astype(jnp.float32), axis=-1)
    pg, gi = lax.top_k(gp, 1)
    el = ((t @ w_erouter).astype(jnp.float32) + b_erouter.astype(jnp.float32)).reshape(-1, N_EXPERT_GROUPS, EXPERTS_PER_GROUP)
    el_sel = jnp.take_along_axis(el, gi[:, :, None], axis=1)[:, 0]
    ev, ei = lax.top_k(el_sel, EXPERT_TOPK)
    wts = jax.nn.softmax(ev, axis=-1) * pg
    eidx = gi * EXPERTS_PER_GROUP + ei
    combine = jnp.sum(jax.nn.one_hot(eidx, N_EXPERTS, dtype=jnp.float32) * wts[..., None], axis=1)
    hdn = jax.nn.silu(jnp.einsum('td,edf->tef', t, w1)) * jnp.einsum('td,edf->tef', t, w3)
    hdn = hdn * combine[:, :, None].astype(hdn.dtype)
    return jnp.einsum('tef,efd->td', hdn, w2).reshape(b, s, d)


def decoder_layer(x, mem_k, mem_v, k_past, v_past, h0_re, h0_im, q_start, p):
    b, s, _ = x.shape
    xn = rmsnorm(x, p['g_mix'])
    proj = xn @ p['w_in']
    q, k, v, u, ga, gb = jnp.split(proj, [D_ATTN, 2 * D_ATTN, 3 * D_ATTN, 3 * D_ATTN + D_SSM,
                                          3 * D_ATTN + D_SSM + D_MODEL], axis=-1)
    q = q.reshape(b, s, N_HEADS_A, HEAD_DIM_A)
    k = k.reshape(b, s, N_HEADS_A, HEAD_DIM_A)
    v = v.reshape(b, s, N_HEADS_A, HEAD_DIM_A)
    if k_past is None:
        k_full, v_full = k, v
    else:
        k_full = jnp.concatenate([k_past.astype(k.dtype), k], axis=1)
        v_full = jnp.concatenate([v_past.astype(v.dtype), v], axis=1)
    attn = moba_attention(q, k_full, v_full, q_start)
    y_s, h_re, h_im = s5_scan(u, h0_re, h0_im, p['lam_re'], p['lam_im'], p['log_dt'],
                              p['b_re'], p['b_im'], p['c_re'], p['c_im'], p['ssm_d'])
    z = jax.nn.gelu(y_s)
    s5_out = z * jax.nn.sigmoid(z @ p['w_glu'])
    merged = jax.nn.sigmoid(ga) * (attn @ p['w_pa']) + jax.nn.sigmoid(gb) * (s5_out @ p['w_pb'])
    x = x + merged @ p['w_o']
    x = x + cross_attention(rmsnorm(x, p['g_x']), mem_k, mem_v, p['w_xq'], p['w_xo'])
    x = x + hier_moe(rmsnorm(x, p['g_ffn']), p['w_group'], p['b_group'], p['w_erouter'],
                     p['b_erouter'], p['w1'], p['w3'], p['w2'])
    return x, k, v, h_re, h_im


def setup_inputs(seed: int = 0) -> dict:
    key = jax.random.key(seed)
    ks = iter(jax.random.split(key, 48))

    def nrm(shape, scale=1.0):
        return jax.random.normal(next(ks), shape, jnp.float32) * scale

    L = DEPTH
    D = D_MODEL
    G = N_SSM_GROUPS
    P = SSM_STATE
    H16 = SSM_GROUP
    n_pages = PAST_LEN // PAGE_SIZE
    n_used = DEC_BATCH * n_pages
    n_pool = n_used + max(1, n_used // 4)
    x_prompt = nrm((BATCH, SEQ, D))
    x_sample = nrm((DEC_BATCH, DEC_SEQ, D))
    mem_prompt = nrm((BATCH, N_MEM, D))
    cache_k = nrm((L, n_pool, PAGE_SIZE, N_HEADS_A, HEAD_DIM_A))
    cache_v = nrm((L, n_pool, PAGE_SIZE, N_HEADS_A, HEAD_DIM_A))
    page_table = jax.random.permutation(next(ks), n_pool)[:n_used].reshape(DEC_BATCH, n_pages).astype(jnp.int32)
    state_ssm_re = nrm((L, DEC_BATCH, G, P), 0.3)
    state_ssm_im = nrm((L, DEC_BATCH, G, P), 0.3)
    cache_mem_k = nrm((L, DEC_BATCH, N_MEM, N_HEADS_X, HEAD_DIM_X))
    cache_mem_v = nrm((L, DEC_BATCH, N_MEM, N_HEADS_X, HEAD_DIM_X))
    g_mix = 1.0 + nrm((L, D), 0.01)
    w_in = nrm((L, D, D_IN_PROJ), D ** -0.5)
    ssm_lambda_re = -0.5 + nrm((L, G, P), 0.01)
    ssm_lambda_im = jnp.pi * jnp.arange(P, dtype=jnp.float32) + nrm((L, G, P), 0.01)
    ssm_log_dt = jax.random.uniform(next(ks), (L, G), jnp.float32, math.log(1e-3), math.log(1e-1))
    ssm_b_re = nrm((L, G, P, H16), (2 * H16) ** -0.5)
    ssm_b_im = nrm((L, G, P, H16), (2 * H16) ** -0.5)
    ssm_c_re = nrm((L, G, H16, P), 2 ** -0.5)
    ssm_c_im = nrm((L, G, H16, P), 2 ** -0.5)
    ssm_d = nrm((L, D_SSM))
    w_glu = nrm((L, D_SSM, D_SSM), D_SSM ** -0.5)
    w_pa = nrm((L, D_ATTN, D), D_ATTN ** -0.5)
    w_pb = nrm((L, D_SSM, D), D_SSM ** -0.5)
    w_o = nrm((L, D, D), D ** -0.5)
    g_x = 1.0 + nrm((L, D), 0.01)
    g_mem = 1.0 + nrm((L, D), 0.01)
    w_xq = nrm((L, D, D_XATTN), D ** -0.5)
    w_xk = nrm((L, D, D_XATTN), D ** -0.5)
    w_xv = nrm((L, D, D_XATTN), D ** -0.5)
    w_xo = nrm((L, D_XATTN, D), D_XATTN ** -0.5)
    g_ffn = 1.0 + nrm((L, D), 0.01)
    w_group = nrm((L, D, N_EXPERT_GROUPS), D ** -0.5)
    b_group = nrm((L, N_EXPERT_GROUPS), 0.01)
    w_erouter = nrm((L, D, N_EXPERTS), D ** -0.5)
    b_erouter = nrm((L, N_EXPERTS), 0.01)
    w1 = nrm((L, N_EXPERTS, D, D_FF_EXPERT), D ** -0.5)
    w3 = nrm((L, N_EXPERTS, D, D_FF_EXPERT), D ** -0.5)
    w2 = nrm((L, N_EXPERTS, D_FF_EXPERT, D), D_FF_EXPERT ** -0.5)
    g_final = 1.0 + nrm((D,), 0.01)
    return {'x_prompt': x_prompt, 'x_sample': x_sample, 'mem_prompt': mem_prompt,
            'cache_k': cache_k, 'cache_v': cache_v, 'page_table': page_table,
            'state_ssm_re': state_ssm_re, 'state_ssm_im': state_ssm_im,
            'cache_mem_k': cache_mem_k, 'cache_mem_v': cache_mem_v,
            'g_mix': g_mix, 'w_in': w_in, 'ssm_lambda_re': ssm_lambda_re, 'ssm_lambda_im': ssm_lambda_im,
            'ssm_log_dt': ssm_log_dt, 'ssm_b_re': ssm_b_re, 'ssm_b_im': ssm_b_im,
            'ssm_c_re': ssm_c_re, 'ssm_c_im': ssm_c_im, 'ssm_d': ssm_d, 'w_glu': w_glu,
            'w_pa': w_pa, 'w_pb': w_pb, 'w_o': w_o, 'g_x': g_x, 'g_mem': g_mem,
            'w_xq': w_xq, 'w_xk': w_xk, 'w_xv': w_xv, 'w_xo': w_xo, 'g_ffn': g_ffn,
            'w_group': w_group, 'b_group': b_group, 'w_erouter': w_erouter, 'b_erouter': b_erouter,
            'w1': w1, 'w3': w3, 'w2': w2, 'g_final': g_final}


def reference(x_prompt, x_sample, mem_prompt, cache_k, cache_v, page_table, state_ssm_re, state_ssm_im,
              cache_mem_k, cache_mem_v, g_mix, w_in, ssm_lambda_re, ssm_lambda_im, ssm_log_dt,
              ssm_b_re, ssm_b_im, ssm_c_re, ssm_c_im, ssm_d, w_glu, w_pa, w_pb, w_o, g_x, g_mem,
              w_xq, w_xk, w_xv, w_xo, g_ffn, w_group, b_group, w_erouter, b_erouter, w1, w3, w2, g_final):
    hp = x_prompt
    hs = x_sample
    n_pages = page_table.shape[1]
    past_len = n_pages * PAGE_SIZE
    kp_l, vp_l, srp_l, sip_l, mkp_l, mvp_l = [], [], [], [], [], []
    ks_l, vs_l, srs_l, sis_l = [], [], [], []
    for l in range(DEPTH):
        p = dict(g_mix=g_mix[l], w_in=w_in[l], lam_re=ssm_lambda_re[l], lam_im=ssm_lambda_im[l],
                 log_dt=ssm_log_dt[l], b_re=ssm_b_re[l], b_im=ssm_b_im[l], c_re=ssm_c_re[l],
                 c_im=ssm_c_im[l], ssm_d=ssm_d[l], w_glu=w_glu[l], w_pa=w_pa[l], w_pb=w_pb[l],
                 w_o=w_o[l], g_x=g_x[l], w_xq=w_xq[l], w_xo=w_xo[l], g_ffn=g_ffn[l],
                 w_group=w_group[l], b_group=b_group[l], w_erouter=w_erouter[l],
                 b_erouter=b_erouter[l], w1=w1[l], w3=w3[l], w2=w2[l])
        mk_p, mv_p = memory_kv(mem_prompt, g_mem[l], w_xk[l], w_xv[l])
        h0 = jnp.zeros((hp.shape[0], N_SSM_GROUPS, SSM_STATE), jnp.float32)
        hp, k_p, v_p, sr_p, si_p = decoder_layer(hp, mk_p, mv_p, None, None, h0, h0, 0, p)
        k_past = cache_k[l][page_table].reshape(DEC_BATCH, past_len, N_HEADS_A, HEAD_DIM_A)
        v_past = cache_v[l][page_table].reshape(DEC_BATCH, past_len, N_HEADS_A, HEAD_DIM_A)
        hs, k_s, v_s, sr_s, si_s = decoder_layer(hs, cache_mem_k[l], cache_mem_v[l], k_past, v_past,
                                                 state_ssm_re[l], state_ssm_im[l], past_len, p)
        kp_l.append(k_p); vp_l.append(v_p); srp_l.append(sr_p); sip_l.append(si_p)
        mkp_l.append(mk_p); mvp_l.append(mv_p)
        ks_l.append(k_s); vs_l.append(v_s); srs_l.append(sr_s); sis_l.append(si_s)
    y_prompt = rmsnorm(hp, g_final)
    y_sample = rmsnorm(hs, g_final)
    new_k_prompt = jnp.stack(kp_l)
    new_v_prompt = jnp.stack(vp_l)
    new_ssm_re_prompt = jnp.stack(srp_l)
    new_ssm_im_prompt = jnp.stack(sip_l)
    new_mem_k_prompt = jnp.stack(mkp_l)
    new_mem_v_prompt = jnp.stack(mvp_l)
    new_k_sample = jnp.stack(ks_l)
    new_v_sample = jnp.stack(vs_l)
    new_ssm_re_sample = jnp.stack(srs_l)
    new_ssm_im_sample = jnp.stack(sis_l)
    return (y_prompt, y_sample, new_k_prompt, new_v_prompt, new_ssm_re_prompt, new_ssm_im_prompt,
            new_mem_k_prompt, new_mem_v_prompt, new_k_sample, new_v_sample, new_ssm_re_sample,
            new_ssm_im_sample)
```

```python
import functools

import jax
import jax.numpy as jnp
from jax import lax
from jax.experimental import pallas as pl
from jax.experimental.pallas import tpu as pltpu

F32 = jnp.float32
BF16 = jnp.bfloat16
I32 = jnp.int32

D_MODEL = 1024
N_HEADS_A = 8
HEAD_DIM_A = 64
D_ATTN = N_HEADS_A * HEAD_DIM_A
MOBA_BLOCK = 256
BLOCK_SHIFT = MOBA_BLOCK.bit_length() - 1
MOBA_TOPK = 3
D_SSM = 512
SSM_GROUP = 16
N_SSM_GROUPS = D_SSM // SSM_GROUP
SSM_STATE = 64
N_STATE = N_SSM_GROUPS * SSM_STATE
N_MEM = 256
N_HEADS_X = 4
HEAD_DIM_X = 128
D_XATTN = N_HEADS_X * HEAD_DIM_X
N_EXPERT_GROUPS = 4
EXPERTS_PER_GROUP = 4
N_EXPERTS = N_EXPERT_GROUPS * EXPERTS_PER_GROUP
D_FF_EXPERT = 256
D_IN_PROJ = 3 * D_ATTN + D_SSM + 2 * D_MODEL
RMS_EPS = 1e-6
NEG_INF = -1e30
PAGE_SIZE = 128

LANES = 128
VMEM_LIMIT = 56 * 1024 * 1024
BIG_NEG = -3e38
M_INIT = -1e29
NO_IDX = 1e9


def _cparams(*sem):
    return pltpu.CompilerParams(dimension_semantics=sem, vmem_limit_bytes=VMEM_LIMIT)


def _rms(x, g):
    return x * lax.rsqrt(jnp.mean(x * x, axis=-1, keepdims=True) + RMS_EPS) * g


def _dot(a, b):
    return jnp.dot(a, b, preferred_element_type=F32)


def _dot_nt(a, b):
    return lax.dot_general(a, b, (((1,), (1,)), ((), ())), preferred_element_type=F32)


def _split_bf16(x):
    hi = x.astype(BF16)
    return hi, (x - hi.astype(F32)).astype(BF16)


def _mm(a, w, precise, nt=False):
    dot = _dot_nt if nt else _dot
    if not precise:
        return dot(a.astype(BF16), w.astype(BF16))
    ah, al = _split_bf16(a.astype(F32))
    wh, wl = _split_bf16(w)
    return dot(ah, wh) + dot(al, wh) + dot(ah, wl)


def _inproj_kernel(x_ref, g_ref, w_ref, q_ref, k_ref, v_ref, kb_ref, vb_ref, u_ref, sga_ref, gb_ref):
    xn = _rms(x_ref[...], g_ref[...]).astype(BF16)

    def mm(c0, n):
        return _dot(xn, w_ref[:, c0:c0 + n])

    q_ref[...] = mm(0, D_ATTN)
    k = mm(D_ATTN, D_ATTN)
    k_ref[...] = k
    kb_ref[...] = k.astype(BF16)
    v = mm(2 * D_ATTN, D_ATTN)
    v_ref[...] = v
    vb_ref[...] = v.astype(BF16)
    u_ref[...] = mm(3 * D_ATTN, D_SSM)
    sga_ref[...] = jax.nn.sigmoid(mm(3 * D_ATTN + D_SSM, D_MODEL)).astype(BF16)
    gb_ref[...] = mm(3 * D_ATTN + D_SSM + D_MODEL, D_MODEL)


def _inproj(x, g, w_bf, tm):
    t = x.shape[0]
    row = lambda n: pl.BlockSpec((tm, n), lambda i: (i, 0))
    full = lambda a: pl.BlockSpec(a.shape, lambda i: (0,) * a.ndim)
    shp = lambda n, dt: jax.ShapeDtypeStruct((t, n), dt)
    return pl.pallas_call(
        _inproj_kernel,
        grid=(t // tm,),
        in_specs=[row(D_MODEL), full(g), full(w_bf)],
        out_specs=[row(D_ATTN)] * 5 + [row(D_SSM), row(D_MODEL), row(D_MODEL)],
        out_shape=[shp(D_ATTN, F32), shp(D_ATTN, F32), shp(D_ATTN, F32), shp(D_ATTN, BF16),
                   shp(D_ATTN, BF16), shp(D_SSM, F32), shp(D_MODEL, BF16), shp(D_MODEL, F32)],
        compiler_params=_cparams("parallel"),
        name="inproj",
    )(x, g, w_bf)


INPROJ_COLS = 512
GA_COL0 = 3 * D_ATTN + D_SSM


def _inproj_precise_kernel(x_ref, g_ref, w_ref, o_ref):
    c0 = pl.program_id(0) * INPROJ_COLS
    acc = _mm(_rms(x_ref[...], g_ref[...]), w_ref[...], True)
    is_ga = (c0 >= GA_COL0) & (c0 < GA_COL0 + D_MODEL)

    @pl.when(is_ga)
    def _():
        o_ref[...] = jax.nn.sigmoid(acc)

    @pl.when(jnp.logical_not(is_ga))
    def _():
        o_ref[...] = acc


def _inproj_precise(x, g, w):
    t = x.shape[0]
    full = lambda a: pl.BlockSpec(a.shape, lambda j: (0,) * a.ndim)
    return pl.pallas_call(
        _inproj_precise_kernel,
        grid=(D_IN_PROJ // INPROJ_COLS,),
        in_specs=[full(x), full(g), pl.BlockSpec((D_MODEL, INPROJ_COLS), lambda j: (0, j))],
        out_specs=pl.BlockSpec((t, INPROJ_COLS), lambda j: (0, j)),
        out_shape=jax.ShapeDtypeStruct((t, D_IN_PROJ), F32),
        compiler_params=_cparams("parallel"),
        name="inproj_sample",
    )(x, g, w)


def _ssm_param_kernel(lr_ref, li_ref, dt_ref, br_ref, bi_ref, ar_ref, ai_ref, bbr_ref, bbi_ref):
    lr = jnp.minimum(lr_ref[...], -1e-4)
    li = li_ref[...]
    dt = jnp.exp(dt_ref[...])
    mag = jnp.exp(lr * dt)
    ar = mag * jnp.cos(li * dt)
    ai = mag * jnp.sin(li * dt)
    den = lr * lr + li * li
    nr = ar - 1.0
    cr = (nr * lr + ai * li) / den
    ci = (ai * lr - nr * li) / den
    ar_ref[...] = ar
    ai_ref[...] = ai
    br = br_ref[...]
    bi = bi_ref[...]
    bbr_ref[...] = cr * br - ci * bi
    bbi_ref[...] = cr * bi + ci * br


def _ssm_params(lam_re, lam_im, log_dt, b_re, b_im):
    n = N_STATE
    lr = lam_re.reshape(1, n)
    li = lam_im.reshape(1, n)
    dt = jnp.broadcast_to(log_dt[:, None], (N_SSM_GROUPS, SSM_STATE)).reshape(1, n)
    brt = b_re.reshape(n, SSM_GROUP).T
    bit = b_im.reshape(n, SSM_GROUP).T
    row = jax.ShapeDtypeStruct((1, n), F32)
    mat = jax.ShapeDtypeStruct((SSM_GROUP, n), F32)
    return pl.pallas_call(_ssm_param_kernel, out_shape=[row, row, mat, mat], name="ssm_params")(
        lr, li, dt, brt, bit)


HALF_STATE = N_STATE // 2
SCAN_COLS = 512
SCAN_TILES = SCAN_COLS // 128


def _s5_kernel(u_ref, gb_ref, h0_ref, ar_ref, ai_ref, bb_ref, cc_ref, d_ref, wglu_ref, wpb_ref,
               sb_ref, ht_ref, s_ref, carry_ref, *, bn, lc, precise):
    m = bn * lc

    @pl.when(pl.program_id(0) == 0)
    def _():
        carry_ref[...] = h0_ref[...]

    u = u_ref[...].reshape(m, D_SSM)
    for j in range(16):
        sec, jj = divmod(j, 4)
        gbase = (sec // 2) * 16 + jj * 4
        lt = (gbase * SSM_GROUP) // LANES
        bu = _mm(u[:, LANES * lt:LANES * (lt + 1)], bb_ref[j], precise)
        s_ref[2 * j] = bu[:, :LANES]
        s_ref[2 * j + 1] = bu[:, LANES:]

    tiles_half = HALF_STATE // LANES
    for h in range(2):
        for c in range(HALF_STATE // SCAN_COLS):
            re_t = [2 * tiles_half * h + SCAN_TILES * c + n for n in range(SCAN_TILES)]
            im_t = [t + tiles_half for t in re_t]
            a_t = [tiles_half * h + SCAN_TILES * c + n for n in range(SCAN_TILES)]
            ars = [jnp.broadcast_to(ar_ref[:, LANES * t:LANES * (t + 1)], (bn, LANES)) for t in a_t]
            ais = [jnp.broadcast_to(ai_ref[:, LANES * t:LANES * (t + 1)], (bn, LANES)) for t in a_t]

            def body(t, carry, re_t=re_t, im_t=im_t, ars=ars, ais=ais):
                rows = pl.ds(t, bn, stride=lc) if lc > 1 else pl.ds(0, bn)
                out = []
                for n in range(SCAN_TILES):
                    xr, xi = carry[2 * n], carry[2 * n + 1]
                    nxr = ars[n] * xr - ais[n] * xi + s_ref[re_t[n], rows, :]
                    nxi = ars[n] * xi + ais[n] * xr + s_ref[im_t[n], rows, :]
                    s_ref[re_t[n], rows, :] = nxr
                    s_ref[im_t[n], rows, :] = nxi
                    out += [nxr, nxi]
                return tuple(out)

            x0 = []
            for n in range(SCAN_TILES):
                x0 += [carry_ref[:, LANES * re_t[n]:LANES * (re_t[n] + 1)],
                       carry_ref[:, LANES * im_t[n]:LANES * (im_t[n] + 1)]]
            xs = lax.fori_loop(0, lc, body, tuple(x0), unroll=min(lc, 8))
            for n in range(SCAN_TILES):
                carry_ref[:, LANES * re_t[n]:LANES * (re_t[n] + 1)] = xs[2 * n]
                carry_ref[:, LANES * im_t[n]:LANES * (im_t[n] + 1)] = xs[2 * n + 1]
    ht_ref[...] = carry_ref[...]

    ys = []
    for h in range(2):
        sdt = F32 if precise else BF16
        xh = jnp.concatenate([s_ref[2 * tiles_half * h + n].astype(sdt) for n in range(2 * tiles_half)], axis=1)
        ys.append(_mm(xh, cc_ref[h], precise))
    y = jnp.concatenate(ys, axis=1) + d_ref[...] * u
    z = jax.nn.gelu(y)
    s5 = z * jax.nn.sigmoid(_mm(z, wglu_ref[...], precise))
    pb = _mm(s5, wpb_ref[...], precise)
    gb = gb_ref[...].reshape(m, D_MODEL)
    sb_ref[...] = (jax.nn.sigmoid(gb) * pb).astype(sb_ref.dtype).reshape(sb_ref.shape)


def _s5(u3, gb3, h0, ar, ai, bb, cc, d, wglu, wpb, bn, lc, precise):
    nb, s, _ = u3.shape
    rows = bn * lc // nb
    nchunk = s // rows
    full = lambda a: pl.BlockSpec(a.shape, lambda c: (0,) * a.ndim)
    blk = lambda n: pl.BlockSpec((nb, rows, n), lambda c: (0, c, 0))
    return pl.pallas_call(
        functools.partial(_s5_kernel, bn=bn, lc=lc, precise=precise),
        grid=(nchunk,),
        in_specs=[blk(D_SSM), blk(D_MODEL), full(h0), full(ar), full(ai), full(bb), full(cc), full(d),
                  full(wglu), full(wpb)],
        out_specs=[blk(D_MODEL), full(h0)],
        out_shape=[jax.ShapeDtypeStruct((nb, s, D_MODEL), F32 if precise else BF16),
                   jax.ShapeDtypeStruct(h0.shape, F32)],
        scratch_shapes=[pltpu.VMEM((2 * N_STATE // LANES, bn * lc, LANES), F32),
                        pltpu.VMEM((bn, 2 * N_STATE), F32)],
        compiler_params=_cparams("arbitrary"),
        name="s5",
    )(u3, gb3, h0, ar, ai, bb, cc, d, wglu, wpb)


def _s5_weights(ar, ai, bbt_re, bbt_im, c_re, c_im):
    g, p, h = N_SSM_GROUPS, SSM_STATE, SSM_GROUP
    eye = jnp.eye(g, dtype=F32)

    def bfull(bt):
        b = bt.reshape(h, g, p)
        return jnp.einsum('hgp,gk->ghkp', b, eye).reshape(g * h, g * p)

    bre, bim = bfull(bbt_re), bfull(bbt_im)
    tiles = []
    for j in range(16):
        sec, jj = divmod(j, 4)
        src = bre if sec % 2 == 0 else bim
        gbase = (sec // 2) * 16 + jj * 4
        lt = (gbase * h) // LANES
        tiles.append(src[LANES * lt:LANES * (lt + 1), gbase * p:(gbase + 4) * p])
    bb = jnp.stack(tiles)

    def cfull(c):
        return jnp.einsum('ghp,gk->gpkh', c, eye).reshape(g * p, g * h)

    cre, cim = cfull(c_re), cfull(c_im)
    halves = []
    for hh in range(2):
        rs = slice(HALF_STATE * hh, HALF_STATE * (hh + 1))
        cs = slice(256 * hh, 256 * (hh + 1))
        halves.append(jnp.concatenate([cre[rs, cs], -cim[rs, cs]], axis=0))
    return bb, jnp.stack(halves)


def _state_to_lanes(re, im):
    b = re.shape[0]
    r = re.reshape(b, 2, HALF_STATE)
    i = im.reshape(b, 2, HALF_STATE)
    return jnp.concatenate([r[:, 0], i[:, 0], r[:, 1], i[:, 1]], axis=1)


def _lanes_to_state(h):
    b = h.shape[0]
    h4 = h.reshape(b, 4, HALF_STATE)
    re = jnp.concatenate([h4[:, 0], h4[:, 2]], axis=1).reshape(b, N_SSM_GROUPS, SSM_STATE)
    im = jnp.concatenate([h4[:, 1], h4[:, 3]], axis=1).reshape(b, N_SSM_GROUPS, SSM_STATE)
    return re, im


def _moba_kernel(q_ref, k_ref, v_ref, o_ref, ka0_ref, ka1_ref, ahi_ref, alo_ref, *, seq):
    i = pl.program_id(2)
    blk = MOBA_BLOCK
    hd = HEAD_DIM_A

    @pl.when(i == 0)
    def _():
        k = k_ref[...]
        rblk = lax.broadcasted_iota(I32, (seq, LANES), 0) >> BLOCK_SHIFT
        lane = lax.broadcasted_iota(I32, (seq, LANES), 1)
        kf = k.astype(F32)
        ka0_ref[...] = jnp.where(lane < hd, kf, jnp.where(lane - hd == rblk, 1.0, 0.0)).astype(BF16)
        ka1_ref[...] = jnp.where(lane >= hd, kf, jnp.where(lane == rblk, 1.0, 0.0)).astype(BF16)
        r = lax.broadcasted_iota(I32, (LANES, seq), 0)
        cblk = lax.broadcasted_iota(I32, (LANES, seq), 1) >> BLOCK_SHIFT
        ind = jnp.where((r == cblk) | (r - hd == cblk), 1.0, 0.0).astype(BF16)
        kmean = _dot(ind, k) * (1.0 / blk)
        rr = lax.broadcasted_iota(I32, (LANES, LANES), 0)
        ll = lax.broadcasted_iota(I32, (LANES, LANES), 1)
        keep = ((rr < hd) & (ll >= hd)) | ((rr >= hd) & (ll < hd))
        hi, lo = _split_bf16(jnp.where(keep, kmean, 0.0))
        ahi_ref[...] = hi
        alo_ref[...] = lo

    q = q_ref[...]
    qhi, qlo = _split_bf16(q)
    sc = _dot_nt(qhi, ahi_ref[...]) + _dot_nt(qlo, ahi_ref[...]) + _dot_nt(qhi, alo_ref[...])

    lane = lax.broadcasted_iota(I32, (blk, LANES), 1)
    lane_f = lane.astype(F32)
    is_h0 = lane >= hd
    jidx = lane & (hd - 1)
    ind_lane = jidx < (seq // blk)
    valid = ind_lane & (jidx < i)
    taken = jnp.zeros((blk, LANES), jnp.bool_)
    for _ in range(MOBA_TOPK):
        sm = jnp.where(valid & jnp.logical_not(taken), sc, BIG_NEG)
        m0 = jnp.max(jnp.where(is_h0, sm, BIG_NEG), axis=1, keepdims=True)
        m1 = jnp.max(jnp.where(is_h0, BIG_NEG, sm), axis=1, keepdims=True)
        mx = jnp.where(is_h0, m0, m1)
        cand = jnp.where((sm == mx) & (sm > BIG_NEG), lane_f, NO_IDX)
        i0 = jnp.min(jnp.where(is_h0, cand, NO_IDX), axis=1, keepdims=True)
        i1 = jnp.min(jnp.where(is_h0, NO_IDX, cand), axis=1, keepdims=True)
        taken = taken | (lane_f == jnp.where(is_h0, i0, i1))
    masked = ind_lane & jnp.logical_not(taken | (jidx == i))
    selb = jnp.where(masked, NEG_INF, 0.0)
    qs = q * (hd ** -0.5)
    qa0 = jnp.where(lane < hd, qs, selb).astype(BF16)
    qa1 = jnp.where(lane >= hd, qs, selb).astype(BF16)

    def block(j, carry, causal):
        off = pl.multiple_of(j * blk, blk)
        vj = v_ref[pl.ds(off, blk), :]
        out = []
        for (qa, ka_ref), (mm, ll_, acc) in zip(((qa0, ka0_ref), (qa1, ka1_ref)), carry):
            s = _dot_nt(qa, ka_ref[pl.ds(off, blk), :])
            if causal:
                row = lax.broadcasted_iota(I32, (blk, blk), 0)
                col = lax.broadcasted_iota(I32, (blk, blk), 1)
                s = jnp.where(col <= row, s, NEG_INF)
            m_new = jnp.maximum(mm, jnp.max(s, axis=1, keepdims=True))
            alpha = jnp.exp(mm - m_new)
            p = jnp.exp(s - m_new)
            l_new = alpha * ll_ + jnp.sum(p, axis=1, keepdims=True)
            acc_new = alpha * acc + _dot(p.astype(BF16), vj)
            out.append((m_new, l_new, acc_new))
        return tuple(out)

    init = tuple((jnp.full((blk, 1), M_INIT, F32), jnp.zeros((blk, 1), F32), jnp.zeros((blk, LANES), F32))
                 for _ in range(2))
    carry = block(i, init, True)
    carry = lax.fori_loop(0, i, lambda j, c: block(j, c, False), carry)
    (_, l0, a0), (_, l1, a1) = carry
    o_ref[...] = jnp.where(lane < hd, a0 / l0, a1 / l1).astype(BF16)


def _moba_prompt(q, kb, vb, batch, seq):
    nqb = seq // MOBA_BLOCK
    qspec = pl.BlockSpec((MOBA_BLOCK, LANES), lambda b, hp, i: (b * nqb + i, hp))
    kvspec = pl.BlockSpec((seq, LANES), lambda b, hp, i: (b, hp))
    return pl.pallas_call(
        functools.partial(_moba_kernel, seq=seq),
        grid=(batch, D_ATTN // LANES, nqb),
        in_specs=[qspec, kvspec, kvspec],
        out_specs=qspec,
        out_shape=jax.ShapeDtypeStruct((batch * seq, D_ATTN), BF16),
        scratch_shapes=[pltpu.VMEM((seq, LANES), BF16), pltpu.VMEM((seq, LANES), BF16),
                        pltpu.VMEM((LANES, LANES), BF16), pltpu.VMEM((LANES, LANES), BF16)],
        compiler_params=_cparams("parallel", "parallel", "arbitrary"),
        name="moba_prompt",
    )(q, kb, vb)


def _memkv_kernel(m_ref, g_ref, wk_ref, wv_ref, mk_ref, mv_ref):
    mn = _rms(m_ref[...], g_ref[...]).astype(BF16)
    mk_ref[...] = _dot(mn, wk_ref[...])
    mv_ref[...] = _dot(mn, wv_ref[...])


def _memkv(mem, g, wk, wv, tm):
    t = mem.shape[0]
    row = lambda n: pl.BlockSpec((tm, n), lambda i: (i, 0))
    full = lambda a: pl.BlockSpec(a.shape, lambda i: (0,) * a.ndim)
    shp = jax.ShapeDtypeStruct((t, D_XATTN), F32)
    return pl.pallas_call(
        _memkv_kernel, grid=(t // tm,),
        in_specs=[row(D_MODEL), full(g), full(wk), full(wv)],
        out_specs=[row(D_XATTN)] * 2, out_shape=[shp, shp],
        compiler_params=_cparams("parallel"), name="memkv",
    )(mem, g, wk, wv)


def _merge_kernel(x_ref, at_ref, sga_ref, sb_ref, wpa_ref, wo_ref, gx_ref, wxq_ref, x1_ref, xq_ref, *, precise):
    pa = _mm(at_ref[...], wpa_ref[...], precise)
    merged = sga_ref[...].astype(F32) * pa + sb_ref[...].astype(F32)
    x1 = x_ref[...] + _mm(merged, wo_ref[...], precise)
    x1_ref[...] = x1
    xq_ref[...] = _mm(_rms(x1, gx_ref[...]), wxq_ref[...], precise).astype(xq_ref.dtype)


def _merge(x, attn, sga, sb, wpa, wo, gx, wxq, tm, precise):
    t = x.shape[0]
    row = lambda n: pl.BlockSpec((tm, n), lambda i: (i, 0))
    full = lambda a: pl.BlockSpec(a.shape, lambda i: (0,) * a.ndim)
    return pl.pallas_call(
        functools.partial(_merge_kernel, precise=precise), grid=(t // tm,),
        in_specs=[row(D_MODEL), row(D_ATTN), row(D_MODEL), row(D_MODEL), full(wpa), full(wo), full(gx),
                  full(wxq)],
        out_specs=[row(D_MODEL), row(D_XATTN)],
        out_shape=[jax.ShapeDtypeStruct((t, D_MODEL), F32),
                   jax.ShapeDtypeStruct((t, D_XATTN), F32 if precise else BF16)],
        compiler_params=_cparams("parallel"), name="merge",
    )(x, attn, sga, sb, wpa, wo, gx, wxq)


def _xattn_kernel(q_ref, mk_ref, mv_ref, o_ref, *, precise):
    q = q_ref[0]
    mk = mk_ref[0]
    mv = mv_ref[0]
    outs = []
    for h in range(N_HEADS_X):
        cs = slice(HEAD_DIM_X * h, HEAD_DIM_X * (h + 1))
        s = _mm(q[:, cs], mk[:, cs], precise, nt=True) * (HEAD_DIM_X ** -0.5)
        e = jnp.exp(s - jnp.max(s, axis=1, keepdims=True))
        p = e / jnp.sum(e, axis=1, keepdims=True)
        outs.append(_mm(p, mv[:, cs], precise))
    o_ref[0] = jnp.concatenate(outs, axis=1).astype(o_ref.dtype)


def _xattn(q3, mk3, mv3, tq, precise):
    b, s, _ = q3.shape
    qspec = pl.BlockSpec((1, tq, D_XATTN), lambda bi, si: (bi, si, 0))
    mspec = pl.BlockSpec((1, N_MEM, D_XATTN), lambda bi, si: (bi, 0, 0))
    return pl.pallas_call(
        functools.partial(_xattn_kernel, precise=precise), grid=(b, s // tq),
        in_specs=[qspec, mspec, mspec], out_specs=qspec,
        out_shape=jax.ShapeDtypeStruct(q3.shape, q3.dtype),
        compiler_params=_cparams("parallel", "parallel"), name="xattn",
    )(q3, mk3, mv3)


GROUP_LANE0 = N_EXPERTS


def _post_kernel(x1_ref, xo_ref, wxo_ref, gf_ref, wrh_ref, wrl_ref, br_ref, x2_ref, xn_ref, cmb_ref, *, precise):
    x2 = x1_ref[...] + _mm(xo_ref[...], wxo_ref[...], precise)
    x2_ref[...] = x2
    t = _rms(x2, gf_ref[...])
    thi, tlo = _split_bf16(t)
    xn_ref[...] = thi
    logits = _dot(thi, wrh_ref[...]) + _dot(tlo, wrh_ref[...]) + _dot(thi, wrl_ref[...]) + br_ref[...]

    lane_i = lax.broadcasted_iota(I32, logits.shape, 1)
    lane = lane_i.astype(F32)
    lane_group = (lane_i >> (EXPERTS_PER_GROUP.bit_length() - 1)).astype(F32)
    isg = (lane_i >= GROUP_LANE0) & (lane_i < GROUP_LANE0 + N_EXPERT_GROUPS)
    gmax = jnp.max(jnp.where(isg, logits, BIG_NEG), axis=1, keepdims=True)
    eg = jnp.where(isg, jnp.exp(jnp.where(isg, logits, gmax) - gmax), 0.0)
    gp = eg / jnp.sum(eg, axis=1, keepdims=True)
    pg = jnp.max(jnp.where(isg, gp, -1.0), axis=1, keepdims=True)
    gi = jnp.min(jnp.where(isg & (gp == pg), lane, NO_IDX), axis=1, keepdims=True) - GROUP_LANE0

    insel = (lane_i < N_EXPERTS) & (lane_group == gi)
    el = jnp.where(insel, logits, BIG_NEG)
    m1 = jnp.max(el, axis=1, keepdims=True)
    i1 = jnp.min(jnp.where(insel & (el == m1), lane, NO_IDX), axis=1, keepdims=True)
    rest = insel & (lane != i1)
    el2 = jnp.where(rest, logits, BIG_NEG)
    m2 = jnp.max(el2, axis=1, keepdims=True)
    i2 = jnp.min(jnp.where(rest & (el2 == m2), lane, NO_IDX), axis=1, keepdims=True)
    e2 = jnp.exp(m2 - m1)
    den = 1.0 + e2
    cmb_ref[...] = jnp.where(lane == i1, (1.0 / den) * pg, jnp.where(lane == i2, (e2 / den) * pg, 0.0))


def _post(x1, xo, wxo, gf, wrh, wrl, br, tm, precise):
    t = x1.shape[0]
    row = lambda n: pl.BlockSpec((tm, n), lambda i: (i, 0))
    full = lambda a: pl.BlockSpec(a.shape, lambda i: (0,) * a.ndim)
    return pl.pallas_call(
        functools.partial(_post_kernel, precise=precise), grid=(t // tm,),
        in_specs=[row(D_MODEL), row(D_XATTN), full(wxo), full(gf), full(wrh), full(wrl), full(br)],
        out_specs=[row(D_MODEL), row(D_MODEL), row(LANES)],
        out_shape=[jax.ShapeDtypeStruct((t, D_MODEL), F32), jax.ShapeDtypeStruct((t, D_MODEL), BF16),
                   jax.ShapeDtypeStruct((t, LANES), F32)],
        compiler_params=_cparams("parallel"), name="post",
    )(x1, xo, wxo, gf, wrh, wrl, br)


def _moe_kernel(x2_ref, xn_ref, cmb_ref, w13_ref, w2_ref, gfin_ref, y_ref, acc_ref):
    e = pl.program_id(1)

    @pl.when(e == 0)
    def _():
        acc_ref[...] = jnp.zeros_like(acc_ref)

    h = _dot(xn_ref[...], w13_ref[0])
    cmb = cmb_ref[...]
    lane = lax.broadcasted_iota(I32, cmb.shape, 1)
    cw = jnp.sum(jnp.where(lane == e, cmb, 0.0), axis=1, keepdims=True)
    hd = jax.nn.silu(h[:, :D_FF_EXPERT]) * h[:, D_FF_EXPERT:] * cw
    acc_ref[...] += _dot(hd.astype(BF16), w2_ref[0])

    @pl.when(e == N_EXPERTS - 1)
    def _():
        y_ref[...] = _rms(x2_ref[...] + acc_ref[...], gfin_ref[...])


def _moe(x2, xn, cmb, w13, w2, gfin, tm):
    t = x2.shape[0]
    row = lambda n: pl.BlockSpec((tm, n), lambda i, e: (i, 0))
    return pl.pallas_call(
        _moe_kernel, grid=(t // tm, N_EXPERTS),
        in_specs=[row(D_MODEL), row(D_MODEL), row(LANES),
                  pl.BlockSpec((1, D_MODEL, 2 * D_FF_EXPERT), lambda i, e: (e, 0, 0)),
                  pl.BlockSpec((1, D_FF_EXPERT, D_MODEL), lambda i, e: (e, 0, 0)),
                  pl.BlockSpec(gfin.shape, lambda i, e: (0, 0))],
        out_specs=row(D_MODEL),
        out_shape=jax.ShapeDtypeStruct((t, D_MODEL), F32),
        scratch_shapes=[pltpu.VMEM((tm, D_MODEL), F32)],
        compiler_params=_cparams("parallel", "arbitrary"), name="moe",
    )(x2, xn, cmb, w13, w2, gfin)


PAGES_PER_STEP = 16
PAGES_PER_BLOCK = MOBA_BLOCK // PAGE_SIZE


def _pagesum_kernel(pt_ref, *refs):
    del pt_ref
    pages, o_ref = refs[:PAGES_PER_STEP], refs[PAGES_PER_STEP]
    rows = []
    for r in range(PAGES_PER_STEP // PAGES_PER_BLOCK):
        acc = jnp.sum(pages[PAGES_PER_BLOCK * r][0], axis=0, keepdims=True)
        for s in range(1, PAGES_PER_BLOCK):
            acc = acc + jnp.sum(pages[PAGES_PER_BLOCK * r + s][0], axis=0, keepdims=True)
        rows.append(acc)
    o_ref[0] = jnp.concatenate(rows, axis=0)


def _pagesum(cache3, page_table):
    nb, npages = page_table.shape
    steps = npages // PAGES_PER_STEP
    nblk = npages // PAGES_PER_BLOCK

    def pspec(r):
        return pl.BlockSpec((1, PAGE_SIZE, D_ATTN), lambda b, s, pt: (pt[b, s * PAGES_PER_STEP + r], 0, 0))

    return pl.pallas_call(
        _pagesum_kernel,
        grid_spec=pltpu.PrefetchScalarGridSpec(
            num_scalar_prefetch=1, grid=(nb, steps),
            in_specs=[pspec(r) for r in range(PAGES_PER_STEP)],
            out_specs=pl.BlockSpec((1, PAGES_PER_STEP // PAGES_PER_BLOCK, D_ATTN), lambda b, s, pt: (b, s, 0))),
        out_shape=jax.ShapeDtypeStruct((nb, nblk, D_ATTN), F32),
        compiler_params=_cparams("parallel", "arbitrary"), name="pagesum",
    )(page_table, *([cache3] * PAGES_PER_STEP))


def _ssel_kernel(q_ref, ks_ref, seg_ref, o_ref, *, n_past):
    q = q_ref[0]
    prod = ks_ref[0] * q
    phi, plo = _split_bf16(prod)
    sc = (_dot(phi, seg_ref[...]) + _dot(plo, seg_ref[...])) * (1.0 / MOBA_BLOCK)
    row_i = lax.broadcasted_iota(I32, sc.shape, 0)
    row = row_i.astype(F32)
    valid = row_i < n_past
    taken = jnp.zeros(sc.shape, jnp.bool_)
    picks = []
    for _ in range(MOBA_TOPK):
        sm = jnp.where(valid & jnp.logical_not(taken), sc, BIG_NEG)
        mx = jnp.max(sm, axis=0, keepdims=True)
        idx = jnp.min(jnp.where((sm == mx) & (sm > BIG_NEG), row, NO_IDX), axis=0, keepdims=True)
        taken = taken | (row == idx)
        picks.append(idx.astype(I32))
    picks.append(jnp.zeros((8 - MOBA_TOPK, LANES), I32))
    o_ref[0] = jnp.concatenate(picks, axis=0)


def _sample_select(q3, ksum, seg, n_past):
    nb, nblk, _ = ksum.shape
    return pl.pallas_call(
        functools.partial(_ssel_kernel, n_past=n_past), grid=(nb,),
        in_specs=[pl.BlockSpec((1, 1, D_ATTN), lambda b: (b, 0, 0)),
                  pl.BlockSpec((1, nblk, D_ATTN), lambda b: (b, 0, 0)),
                  pl.BlockSpec(seg.shape, lambda b: (0, 0))],
        out_specs=pl.BlockSpec((1, 8, LANES), lambda b: (b, 0, 0)),
        out_shape=jax.ShapeDtypeStruct((nb, 8, LANES), I32),
        compiler_params=_cparams("parallel"), name="sample_select",
    )(q3, ksum, seg)


N_SEL_PAGES = 2 * MOBA_TOPK * PAGES_PER_BLOCK


def _sattn_kernel(sel_ref, pt_ref, q_ref, kn_ref, vn_ref, *refs):
    del sel_ref, pt_ref
    kp, vp, o_ref = refs[:N_SEL_PAGES], refs[N_SEL_PAGES:2 * N_SEL_PAGES], refs[2 * N_SEL_PAGES]
    hd = HEAD_DIM_A
    lane = lax.broadcasted_iota(I32, (1, LANES), 1)
    qs = q_ref[0] * (hd ** -0.5)
    kn = kn_ref[0]
    vn = vn_ref[0]
    per_head = N_SEL_PAGES // 2
    outs = []
    for hh in range(2):
        mine = (lane >= hd) if hh == 1 else (lane < hd)
        qh = jnp.where(mine, qs, 0.0)
        q8 = jnp.broadcast_to(qh, (8, LANES))
        ss = [_mm(q8, kp[hh * per_head + r][0], True, nt=True) for r in range(per_head)]
        s_self = jnp.sum(qh * kn, axis=1, keepdims=True)
        mx = s_self
        for s in ss:
            mx = jnp.maximum(mx, jnp.max(s, axis=1, keepdims=True)[:1])
        p_self = jnp.exp(s_self - mx)
        den = p_self
        acc = p_self * vn
        acc8 = jnp.zeros((8, LANES), F32)
        for r, s in enumerate(ss):
            p = jnp.exp(s - mx)
            den = den + jnp.sum(p, axis=1, keepdims=True)[:1]
            acc8 = acc8 + _mm(p, vp[hh * per_head + r][0], True)
        outs.append((acc + acc8[:1]) / den)
    o_ref[0] = jnp.where(lane < hd, outs[0], outs[1])


def _sample_attn(sel_flat, pt_flat, q3, kn3, vn3, ck3, cv3, n_pages):
    nb = q3.shape[0]
    nhp = D_ATTN // LANES

    def pspec(slot):
        hh, rem = divmod(slot, MOBA_TOPK * PAGES_PER_BLOCK)
        r, half = divmod(rem, PAGES_PER_BLOCK)

        def imap(b, hp, sel, pt):
            blk = sel[(b * N_HEADS_A + 2 * hp + hh) * MOBA_TOPK + r]
            return (pt[b * n_pages + blk * PAGES_PER_BLOCK + half], 0, hp)

        return pl.BlockSpec((1, PAGE_SIZE, LANES), imap)

    tok = pl.BlockSpec((1, 1, LANES), lambda b, hp, sel, pt: (b, 0, hp))
    return pl.pallas_call(
        _sattn_kernel,
        grid_spec=pltpu.PrefetchScalarGridSpec(
            num_scalar_prefetch=2, grid=(nb, nhp),
            in_specs=[tok, tok, tok] + [pspec(s) for s in range(N_SEL_PAGES)] * 2,
            out_specs=tok),
        out_shape=jax.ShapeDtypeStruct((nb, 1, D_ATTN), F32),
        compiler_params=_cparams("parallel", "arbitrary"), name="sample_attn",
    )(sel_flat, pt_flat, q3, kn3, vn3, *([ck3] * N_SEL_PAGES), *([cv3] * N_SEL_PAGES))


def _tail(x, attn, sga, sb, mk3, mv3, w, batch, seq, tm, tq, precise):
    x1, xq = _merge(x, attn, sga, sb, w['wpa'], w['wo'], w['gx'], w['wxq'], tm, precise)
    xq3 = xq.reshape(batch, seq, D_XATTN)
    pad = (-seq) % tq
    if pad:
        xq3 = jnp.pad(xq3, ((0, 0), (0, pad), (0, 0)))
    xo = _xattn(xq3, mk3, mv3, tq, precise)[:, :seq].reshape(batch * seq, D_XATTN)
    x2, xn, cmb = _post(x1, xo, w['wxo'], w['gf'], w['wrh'], w['wrl'], w['br'], tm, precise)
    return _moe(x2, xn, cmb, w['w13'], w['w2'], w['gfin'], tm)


def kernel(x_prompt, x_sample, mem_prompt, cache_k, cache_v, page_table, state_ssm_re, state_ssm_im,
           cache_mem_k, cache_mem_v, g_mix, w_in, ssm_lambda_re, ssm_lambda_im, ssm_log_dt,
           ssm_b_re, ssm_b_im, ssm_c_re, ssm_c_im, ssm_d, w_glu, w_pa, w_pb, w_o, g_x, g_mem,
           w_xq, w_xk, w_xv, w_xo, g_ffn, w_group, b_group, w_erouter, b_erouter, w1, w3, w2, g_final):
    depth = w_in.shape[0]
    assert depth == 1
    l = 0
    bp, sp, _ = x_prompt.shape
    bs, ss, _ = x_sample.shape
    assert ss == 1
    n_pages = page_table.shape[1]
    past_len = n_pages * PAGE_SIZE
    assert past_len % MOBA_BLOCK == 0 and sp % MOBA_BLOCK == 0

    row = lambda a: a.reshape(1, -1).astype(F32)
    bf = lambda a: a.astype(BF16)
    wr = jnp.zeros((D_MODEL, LANES), F32)
    wr = wr.at[:, :N_EXPERTS].set(w_erouter[l]).at[:, GROUP_LANE0:GROUP_LANE0 + N_EXPERT_GROUPS].set(w_group[l])
    wrh, wrl = _split_bf16(wr)
    br = jnp.zeros((1, LANES), F32)
    br = br.at[0, :N_EXPERTS].set(b_erouter[l]).at[0, GROUP_LANE0:GROUP_LANE0 + N_EXPERT_GROUPS].set(b_group[l])
    ws = dict(wpa=w_pa[l], wo=w_o[l], gx=row(g_x[l]), wxq=w_xq[l], wxo=w_xo[l], gf=row(g_ffn[l]),
              wrh=wrh, wrl=wrl, br=br, w13=bf(jnp.concatenate([w1[l], w3[l]], axis=-1)), w2=bf(w2[l]),
              gfin=row(g_final))
    wp = dict(ws, wpa=bf(w_pa[l]), wo=bf(w_o[l]), wxq=bf(w_xq[l]), wxo=bf(w_xo[l]))
    gmix = row(g_mix[l])

    ar, ai, bbt_re, bbt_im = _ssm_params(ssm_lambda_re[l], ssm_lambda_im[l], ssm_log_dt[l], ssm_b_re[l],
                                         ssm_b_im[l])
    bb, cc = _s5_weights(ar, ai, bbt_re, bbt_im, ssm_c_re[l], ssm_c_im[l])
    d_row = row(ssm_d[l])

    tp = bp * sp
    xp = x_prompt.reshape(tp, D_MODEL)
    q, k, v, kb, vb, u, sga, gb = _inproj(xp, gmix, bf(w_in[l]), 512)
    attn = _moba_prompt(q, kb, vb, bp, sp)
    h0 = jnp.zeros((bp, 2 * N_STATE), F32)
    sb3, ht = _s5(u.reshape(bp, sp, D_SSM), gb.reshape(bp, sp, D_MODEL), h0, ar, ai, bf(bb), bf(cc), d_row,
                  bf(w_glu[l]), bf(w_pb[l]), bn=bp, lc=128, precise=False)
    sr_p, si_p = _lanes_to_state(ht)
    mk_p, mv_p = _memkv(mem_prompt.reshape(bp * N_MEM, D_MODEL), row(g_mem[l]), bf(w_xk[l]), bf(w_xv[l]), 256)
    y_p = _tail(xp, attn, sga, sb3.reshape(tp, D_MODEL), mk_p.reshape(bp, N_MEM, D_XATTN),
                mv_p.reshape(bp, N_MEM, D_XATTN), wp, bp, sp, 1024, 512, False)

    xs = x_sample.reshape(bs, D_MODEL)
    proj_s = _inproj_precise(xs, gmix, w_in[l])
    q_s, k_s, v_s = (proj_s[:, D_ATTN * n:D_ATTN * (n + 1)] for n in range(3))
    u_s = proj_s[:, 3 * D_ATTN:GA_COL0]
    sga_s = proj_s[:, GA_COL0:GA_COL0 + D_MODEL]
    gb_s = proj_s[:, GA_COL0 + D_MODEL:]
    ck3 = cache_k[l].reshape(-1, PAGE_SIZE, D_ATTN)
    cv3 = cache_v[l].reshape(-1, PAGE_SIZE, D_ATTN)
    ksum = _pagesum(ck3, page_table)
    seg = (jnp.arange(D_ATTN)[:, None] // HEAD_DIM_A == jnp.arange(LANES)[None, :]).astype(BF16)
    sel = _sample_select(q_s.reshape(bs, 1, D_ATTN), ksum, seg, past_len // MOBA_BLOCK)
    sel_flat = sel[:, :MOBA_TOPK, :N_HEADS_A].transpose(0, 2, 1).reshape(-1)
    attn_s = _sample_attn(sel_flat, page_table.reshape(-1), q_s.reshape(bs, 1, D_ATTN),
                          k_s.reshape(bs, 1, D_ATTN), v_s.reshape(bs, 1, D_ATTN), ck3, cv3, n_pages)
    h0_s = _state_to_lanes(state_ssm_re[l].reshape(bs, N_STATE), state_ssm_im[l].reshape(bs, N_STATE))
    sb_s, ht_s = _s5(u_s.reshape(1, bs, D_SSM), gb_s.reshape(1, bs, D_MODEL), h0_s, ar, ai, bb, cc, d_row,
                     w_glu[l], w_pb[l], bn=bs, lc=1, precise=True)
    sr_s, si_s = _lanes_to_state(ht_s)
    y_s = _tail(xs, attn_s.reshape(bs, D_ATTN), sga_s, sb_s.reshape(bs, D_MODEL),
                cache_mem_k[l].reshape(bs, N_MEM, D_XATTN), cache_mem_v[l].reshape(bs, N_MEM, D_XATTN),
                ws, bs, 1, bs, 16, True)

    kv5 = lambda a, b, s: a.reshape(1, b, s, N_HEADS_A, HEAD_DIM_A)
    st4 = lambda a: a[None]
    mem5 = lambda a: a.reshape(1, bp, N_MEM, N_HEADS_X, HEAD_DIM_X)
    return (y_p.reshape(bp, sp, D_MODEL), y_s.reshape(bs, 1, D_MODEL),
            kv5(k, bp, sp), kv5(v, bp, sp), st4(sr_p), st4(si_p), mem5(mk_p), mem5(mv_p),
            kv5(k_s, bs, 1), kv5(v_s, bs, 1), st4(sr_s), st4(si_s))
```

```python
import functools

import jax
import jax.numpy as jnp
from jax import lax
from jax.experimental import pallas as pl
from jax.experimental.pallas import tpu as pltpu

F32 = jnp.float32
BF16 = jnp.bfloat16
I32 = jnp.int32

D_MODEL = 1024
N_HEADS_A = 8
HEAD_DIM_A = 64
D_ATTN = N_HEADS_A * HEAD_DIM_A
MOBA_BLOCK = 256
BLOCK_SHIFT = MOBA_BLOCK.bit_length() - 1
MOBA_TOPK = 3
D_SSM = 512
SSM_GROUP = 16
N_SSM_GROUPS = D_SSM // SSM_GROUP
SSM_STATE = 64
N_STATE = N_SSM_GROUPS * SSM_STATE
N_MEM = 256
N_HEADS_X = 4
HEAD_DIM_X = 128
D_XATTN = N_HEADS_X * HEAD_DIM_X
N_EXPERT_GROUPS = 4
EXPERTS_PER_GROUP = 4
N_EXPERTS = N_EXPERT_GROUPS * EXPERTS_PER_GROUP
D_FF_EXPERT = 256
D_IN_PROJ = 3 * D_ATTN + D_SSM + 2 * D_MODEL
RMS_EPS = 1e-6
NEG_INF = -1e30
PAGE_SIZE = 128

LANES = 128
VMEM_LIMIT = 56 * 1024 * 1024
BIG_NEG = -3e38
M_INIT = -1e29
NO_IDX = 1e9


def _cparams(*sem):
    return pltpu.CompilerParams(dimension_semantics=sem, vmem_limit_bytes=VMEM_LIMIT)


def _rms(x, g):
    return x * lax.rsqrt(jnp.mean(x * x, axis=-1, keepdims=True) + RMS_EPS) * g


def _dot(a, b):
    return jnp.dot(a, b, preferred_element_type=F32)


def _dot_nt(a, b):
    return lax.dot_general(a, b, (((1,), (1,)), ((), ())), preferred_element_type=F32)


def _split_bf16(x):
    hi = x.astype(BF16)
    return hi, (x - hi.astype(F32)).astype(BF16)


def _mm(a, w, precise, nt=False):
    dot = _dot_nt if nt else _dot
    if not precise:
        return dot(a.astype(BF16), w.astype(BF16))
    ah, al = _split_bf16(a.astype(F32))
    wh, wl = _split_bf16(w)
    return dot(ah, wh) + dot(al, wh) + dot(ah, wl)


def _inproj_kernel(x_ref, g_ref, w_ref, q_ref, k_ref, v_ref, kb_ref, vb_ref, u_ref, sga_ref, gb_ref):
    xn = _rms(x_ref[...], g_ref[...]).astype(BF16)

    def mm(c0, n):
        return _dot(xn, w_ref[:, c0:c0 + n])

    q_ref[...] = mm(0, D_ATTN)
    k = mm(D_ATTN, D_ATTN)
    k_ref[...] = k
    kb_ref[...] = k.astype(BF16)
    v = mm(2 * D_ATTN, D_ATTN)
    v_ref[...] = v
    vb_ref[...] = v.astype(BF16)
    u_ref[...] = mm(3 * D_ATTN, D_SSM)
    sga_ref[...] = jax.nn.sigmoid(mm(3 * D_ATTN + D_SSM, D_MODEL)).astype(BF16)
    gb_ref[...] = mm(3 * D_ATTN + D_SSM + D_MODEL, D_MODEL)


def _inproj(x, g, w_bf, tm):
    t = x.shape[0]
    row = lambda n: pl.BlockSpec((tm, n), lambda i: (i, 0))
    full = lambda a: pl.BlockSpec(a.shape, lambda i: (0,) * a.ndim)
    shp = lambda n, dt: jax.ShapeDtypeStruct((t, n), dt)
    return pl.pallas_call(
        _inproj_kernel,
        grid=(t // tm,),
        in_specs=[row(D_MODEL), full(g), full(w_bf)],
        out_specs=[row(D_ATTN)] * 5 + [row(D_SSM), row(D_MODEL), row(D_MODEL)],
        out_shape=[shp(D_ATTN, F32), shp(D_ATTN, F32), shp(D_ATTN, F32), shp(D_ATTN, BF16),
                   shp(D_ATTN, BF16), shp(D_SSM, F32), shp(D_MODEL, BF16), shp(D_MODEL, F32)],
        compiler_params=_cparams("parallel"),
        name="inproj",
    )(x, g, w_bf)


INPROJ_COLS = 512
GA_COL0 = 3 * D_ATTN + D_SSM


def _inproj_precise_kernel(x_ref, g_ref, w_ref, o_ref):
    c0 = pl.program_id(0) * INPROJ_COLS
    acc = _mm(_rms(x_ref[...], g_ref[...]), w_ref[...], True)
    is_ga = (c0 >= GA_COL0) & (c0 < GA_COL0 + D_MODEL)

    @pl.when(is_ga)
    def _():
        o_ref[...] = jax.nn.sigmoid(acc)

    @pl.when(jnp.logical_not(is_ga))
    def _():
        o_ref[...] = acc


def _inproj_precise(x, g, w):
    t = x.shape[0]
    full = lambda a: pl.BlockSpec(a.shape, lambda j: (0,) * a.ndim)
    return pl.pallas_call(
        _inproj_precise_kernel,
        grid=(D_IN_PROJ // INPROJ_COLS,),
        in_specs=[full(x), full(g), pl.BlockSpec((D_MODEL, INPROJ_COLS), lambda j: (0, j))],
        out_specs=pl.BlockSpec((t, INPROJ_COLS), lambda j: (0, j)),
        out_shape=jax.ShapeDtypeStruct((t, D_IN_PROJ), F32),
        compiler_params=_cparams("parallel"),
        name="inproj_sample",
    )(x, g, w)


def _ssm_param_kernel(lr_ref, li_ref, dt_ref, br_ref, bi_ref, ar_ref, ai_ref, bbr_ref, bbi_ref):
    lr = jnp.minimum(lr_ref[...], -1e-4)
    li = li_ref[...]
    dt = jnp.exp(dt_ref[...])
    mag = jnp.exp(lr * dt)
    ar = mag * jnp.cos(li * dt)
    ai = mag * jnp.sin(li * dt)
    den = lr * lr + li * li
    nr = ar - 1.0
    cr = (nr * lr + ai * li) / den
    ci = (ai * lr - nr * li) / den
    ar_ref[...] = ar
    ai_ref[...] = ai
    br = br_ref[...]
    bi = bi_ref[...]
    bbr_ref[...] = cr * br - ci * bi
    bbi_ref[...] = cr * bi + ci * br


def _ssm_params(lam_re, lam_im, log_dt, b_re, b_im):
    n = N_STATE
    lr = lam_re.reshape(1, n)
    li = lam_im.reshape(1, n)
    dt = jnp.broadcast_to(log_dt[:, None], (N_SSM_GROUPS, SSM_STATE)).reshape(1, n)
    brt = b_re.reshape(n, SSM_GROUP).T
    bit = b_im.reshape(n, SSM_GROUP).T
    row = jax.ShapeDtypeStruct((1, n), F32)
    mat = jax.ShapeDtypeStruct((SSM_GROUP, n), F32)
    return pl.pallas_call(_ssm_param_kernel, out_shape=[row, row, mat, mat], name="ssm_params")(
        lr, li, dt, brt, bit)


HALF_STATE = N_STATE // 2
SCAN_COLS = 512
SCAN_TILES = SCAN_COLS // 128


def _s5_kernel(u_ref, gb_ref, h0_ref, ar_ref, ai_ref, bb_ref, cc_ref, d_ref, wglu_ref, wpb_ref,
               sb_ref, ht_ref, s_ref, carry_ref, *, bn, lc, precise):
    m = bn * lc

    @pl.when(pl.program_id(0) == 0)
    def _():
        carry_ref[...] = h0_ref[...]

    u = u_ref[...].reshape(m, D_SSM)
    for j in range(16):
        sec, jj = divmod(j, 4)
        gbase = (sec // 2) * 16 + jj * 4
        lt = (gbase * SSM_GROUP) // LANES
        bu = _mm(u[:, LANES * lt:LANES * (lt + 1)], bb_ref[j], precise)
        s_ref[2 * j] = bu[:, :LANES]
        s_ref[2 * j + 1] = bu[:, LANES:]

    tiles_half = HALF_STATE // LANES
    for h in range(2):
        for c in range(HALF_STATE // SCAN_COLS):
            re_t = [2 * tiles_half * h + SCAN_TILES * c + n for n in range(SCAN_TILES)]
            im_t = [t + tiles_half for t in re_t]
            a_t = [tiles_half * h + SCAN_TILES * c + n for n in range(SCAN_TILES)]
            ars = [jnp.broadcast_to(ar_ref[:, LANES * t:LANES * (t + 1)], (bn, LANES)) for t in a_t]
            ais = [jnp.broadcast_to(ai_ref[:, LANES * t:LANES * (t + 1)], (bn, LANES)) for t in a_t]

            def body(t, carry, re_t=re_t, im_t=im_t, ars=ars, ais=ais):
                rows = pl.ds(t, bn, stride=lc) if lc > 1 else pl.ds(0, bn)
                out = []
                for n in range(SCAN_TILES):
                    xr, xi = carry[2 * n], carry[2 * n + 1]
                    nxr = ars[n] * xr - ais[n] * xi + s_ref[re_t[n], rows, :]
                    nxi = ars[n] * xi + ais[n] * xr + s_ref[im_t[n], rows, :]
                    s_ref[re_t[n], rows, :] = nxr
                    s_ref[im_t[n], rows, :] = nxi
                    out += [nxr, nxi]
                return tuple(out)

            x0 = []
            for n in range(SCAN_TILES):
                x0 += [carry_ref[:, LANES * re_t[n]:LANES * (re_t[n] + 1)],
                       carry_ref[:, LANES * im_t[n]:LANES * (im_t[n] + 1)]]
            xs = lax.fori_loop(0, lc, body, tuple(x0), unroll=min(lc, 8))
            for n in range(SCAN_TILES):
                carry_ref[:, LANES * re_t[n]:LANES * (re_t[n] + 1)] = xs[2 * n]
                carry_ref[:, LANES * im_t[n]:LANES * (im_t[n] + 1)] = xs[2 * n + 1]
    ht_ref[...] = carry_ref[...]

    ys = []
    for h in range(2):
        sdt = F32 if precise else BF16
        xh = jnp.concatenate([s_ref[2 * tiles_half * h + n].astype(sdt) for n in range(2 * tiles_half)], axis=1)
        ys.append(_mm(xh, cc_ref[h], precise))
    y = jnp.concatenate(ys, axis=1) + d_ref[...] * u
    z = jax.nn.gelu(y)
    s5 = z * jax.nn.sigmoid(_mm(z, wglu_ref[...], precise))
    pb = _mm(s5, wpb_ref[...], precise)
    gb = gb_ref[...].reshape(m, D_MODEL)
    sb_ref[...] = (jax.nn.sigmoid(gb) * pb).astype(sb_ref.dtype).reshape(sb_ref.shape)


def _s5(u3, gb3, h0, ar, ai, bb, cc, d, wglu, wpb, bn, lc, precise):
    nb, s, _ = u3.shape
    rows = bn * lc // nb
    nchunk = s // rows
    full = lambda a: pl.BlockSpec(a.shape, lambda c: (0,) * a.ndim)
    blk = lambda n: pl.BlockSpec((nb, rows, n), lambda c: (0, c, 0))
    return pl.pallas_call(
        functools.partial(_s5_kernel, bn=bn, lc=lc, precise=precise),
        grid=(nchunk,),
        in_specs=[blk(D_SSM), blk(D_MODEL), full(h0), full(ar), full(ai), full(bb), full(cc), full(d),
                  full(wglu), full(wpb)],
        out_specs=[blk(D_MODEL), full(h0)],
        out_shape=[jax.ShapeDtypeStruct((nb, s, D_MODEL), F32 if precise else BF16),
                   jax.ShapeDtypeStruct(h0.shape, F32)],
        scratch_shapes=[pltpu.VMEM((2 * N_STATE // LANES, bn * lc, LANES), F32),
                        pltpu.VMEM((bn, 2 * N_STATE), F32)],
        compiler_params=_cparams("arbitrary"),
        name="s5",
    )(u3, gb3, h0, ar, ai, bb, cc, d, wglu, wpb)


def _s5_weights(ar, ai, bbt_re, bbt_im, c_re, c_im):
    g, p, h = N_SSM_GROUPS, SSM_STATE, SSM_GROUP
    eye = jnp.eye(g, dtype=F32)

    def bfull(bt):
        b = bt.reshape(h, g, p)
        return jnp.einsum('hgp,gk->ghkp', b, eye).reshape(g * h, g * p)

    bre, bim = bfull(bbt_re), bfull(bbt_im)
    tiles = []
    for j in range(16):
        sec, jj = divmod(j, 4)
        src = bre if sec % 2 == 0 else bim
        gbase = (sec // 2) * 16 + jj * 4
        lt = (gbase * h) // LANES
        tiles.append(src[LANES * lt:LANES * (lt + 1), gbase * p:(gbase + 4) * p])
    bb = jnp.stack(tiles)

    def cfull(c):
        return jnp.einsum('ghp,gk->gpkh', c, eye).reshape(g * p, g * h)

    cre, cim = cfull(c_re), cfull(c_im)
    halves = []
    for hh in range(2):
        rs = slice(HALF_STATE * hh, HALF_STATE * (hh + 1))
        cs = slice(256 * hh, 256 * (hh + 1))
        halves.append(jnp.concatenate([cre[rs, cs], -cim[rs, cs]], axis=0))
    return bb, jnp.stack(halves)


def _state_to_lanes(re, im):
    b = re.shape[0]
    r = re.reshape(b, 2, HALF_STATE)
    i = im.reshape(b, 2, HALF_STATE)
    return jnp.concatenate([r[:, 0], i[:, 0], r[:, 1], i[:, 1]], axis=1)


def _lanes_to_state(h):
    b = h.shape[0]
    h4 = h.reshape(b, 4, HALF_STATE)
    re = jnp.concatenate([h4[:, 0], h4[:, 2]], axis=1).reshape(b, N_SSM_GROUPS, SSM_STATE)
    im = jnp.concatenate([h4[:, 1], h4[:, 3]], axis=1).reshape(b, N_SSM_GROUPS, SSM_STATE)
    return re, im


def _moba_kernel(q_ref, k_ref, v_ref, o_ref, ka0_ref, ka1_ref, ahi_ref, alo_ref, *, seq):
    i = pl.program_id(2)
    blk = MOBA_BLOCK
    hd = HEAD_DIM_A

    @pl.when(i == 0)
    def _():
        k = k_ref[...]
        rblk = lax.broadcasted_iota(I32, (seq, LANES), 0) >> BLOCK_SHIFT
        lane = lax.broadcasted_iota(I32, (seq, LANES), 1)
        kf = k.astype(F32)
        ka0_ref[...] = jnp.where(lane < hd, kf, jnp.where(lane - hd == rblk, 1.0, 0.0)).astype(BF16)
        ka1_ref[...] = jnp.where(lane >= hd, kf, jnp.where(lane == rblk, 1.0, 0.0)).astype(BF16)
        r = lax.broadcasted_iota(I32, (LANES, seq), 0)
        cblk = lax.broadcasted_iota(I32, (LANES, seq), 1) >> BLOCK_SHIFT
        ind = jnp.where((r == cblk) | (r - hd == cblk), 1.0, 0.0).astype(BF16)
        kmean = _dot(ind, k) * (1.0 / blk)
        rr = lax.broadcasted_iota(I32, (LANES, LANES), 0)
        ll = lax.broadcasted_iota(I32, (LANES, LANES), 1)
        keep = ((rr < hd) & (ll >= hd)) | ((rr >= hd) & (ll < hd))
        hi, lo = _split_bf16(jnp.where(keep, kmean, 0.0))
        ahi_ref[...] = hi
        alo_ref[...] = lo

    q = q_ref[...]
    qhi, qlo = _split_bf16(q)
    sc = _dot_nt(qhi, ahi_ref[...]) + _dot_nt(qlo, ahi_ref[...]) + _dot_nt(qhi, alo_ref[...])

    lane = lax.broadcasted_iota(I32, (blk, LANES), 1)
    lane_f = lane.astype(F32)
    is_h0 = lane >= hd
    jidx = lane & (hd - 1)
    ind_lane = jidx < (seq // blk)
    valid = ind_lane & (jidx < i)
    taken = jnp.zeros((blk, LANES), jnp.bool_)
    for _ in range(MOBA_TOPK):
        sm = jnp.where(valid & jnp.logical_not(taken), sc, BIG_NEG)
        m0 = jnp.max(jnp.where(is_h0, sm, BIG_NEG), axis=1, keepdims=True)
        m1 = jnp.max(jnp.where(is_h0, BIG_NEG, sm), axis=1, keepdims=True)
        mx = jnp.where(is_h0, m0, m1)
        cand = jnp.where((sm == mx) & (sm > BIG_NEG), lane_f, NO_IDX)
        i0 = jnp.min(jnp.where(is_h0, cand, NO_IDX), axis=1, keepdims=True)
        i1 = jnp.min(jnp.where(is_h0, NO_IDX, cand), axis=1, keepdims=True)
        taken = taken | (lane_f == jnp.where(is_h0, i0, i1))
    masked = ind_lane & jnp.logical_not(taken | (jidx == i))
    selb = jnp.where(masked, NEG_INF, 0.0)
    qs = q * (hd ** -0.5)
    qa0 = jnp.where(lane < hd, qs, selb).astype(BF16)
    qa1 = jnp.where(lane >= hd, qs, selb).astype(BF16)

    def block(j, carry, causal):
        off = pl.multiple_of(j * blk, blk)
        vj = v_ref[pl.ds(off, blk), :]
        out = []
        for (qa, ka_ref), (mm, ll_, acc) in zip(((qa0, ka0_ref), (qa1, ka1_ref)), carry):
            s = _dot_nt(qa, ka_ref[pl.ds(off, blk), :])
            if causal:
                row = lax.broadcasted_iota(I32, (blk, blk), 0)
                col = lax.broadcasted_iota(I32, (blk, blk), 1)
                s = jnp.where(col <= row, s, NEG_INF)
            m_new = jnp.maximum(mm, jnp.max(s, axis=1, keepdims=True))
            alpha = jnp.exp(mm - m_new)
            p = jnp.exp(s - m_new)
            l_new = alpha * ll_ + jnp.sum(p, axis=1, keepdims=True)
            acc_new = alpha * acc + _dot(p.astype(BF16), vj)
            out.append((m_new, l_new, acc_new))
        return tuple(out)

    init = tuple((jnp.full((blk, 1), M_INIT, F32), jnp.zeros((blk, 1), F32), jnp.zeros((blk, LANES), F32))
                 for _ in range(2))
    carry = block(i, init, True)
    carry = lax.fori_loop(0, i, lambda j, c: block(j, c, False), carry)
    (_, l0, a0), (_, l1, a1) = carry
    o_ref[...] = jnp.where(lane < hd, a0 / l0, a1 / l1).astype(BF16)


def _moba_prompt(q, kb, vb, batch, seq):
    nqb = seq // MOBA_BLOCK
    qspec = pl.BlockSpec((MOBA_BLOCK, LANES), lambda b, hp, i: (b * nqb + i, hp))
    kvspec = pl.BlockSpec((seq, LANES), lambda b, hp, i: (b, hp))
    return pl.pallas_call(
        functools.partial(_moba_kernel, seq=seq),
        grid=(batch, D_ATTN // LANES, nqb),
        in_specs=[qspec, kvspec, kvspec],
        out_specs=qspec,
        out_shape=jax.ShapeDtypeStruct((batch * seq, D_ATTN), BF16),
        scratch_shapes=[pltpu.VMEM((seq, LANES), BF16), pltpu.VMEM((seq, LANES), BF16),
                        pltpu.VMEM((LANES, LANES), BF16), pltpu.VMEM((LANES, LANES), BF16)],
        compiler_params=_cparams("parallel", "parallel", "arbitrary"),
        name="moba_prompt",
    )(q, kb, vb)


def _memkv_kernel(m_ref, g_ref, wk_ref, wv_ref, mk_ref, mv_ref):
    mn = _rms(m_ref[...], g_ref[...]).astype(BF16)
    mk_ref[...] = _dot(mn, wk_ref[...])
    mv_ref[...] = _dot(mn, wv_ref[...])


def _memkv(mem, g, wk, wv, tm):
    t = mem.shape[0]
    row = lambda n: pl.BlockSpec((tm, n), lambda i: (i, 0))
    full = lambda a: pl.BlockSpec(a.shape, lambda i: (0,) * a.ndim)
    shp = jax.ShapeDtypeStruct((t, D_XATTN), F32)
    return pl.pallas_call(
        _memkv_kernel, grid=(t // tm,),
        in_specs=[row(D_MODEL), full(g), full(wk), full(wv)],
        out_specs=[row(D_XATTN)] * 2, out_shape=[shp, shp],
        compiler_params=_cparams("parallel"), name="memkv",
    )(mem, g, wk, wv)


def _merge_kernel(x_ref, at_ref, sga_ref, sb_ref, wpa_ref, wo_ref, gx_ref, wxq_ref, x1_ref, xq_ref, *, precise):
    pa = _mm(at_ref[...], wpa_ref[...], precise)
    merged = sga_ref[...].astype(F32) * pa + sb_ref[...].astype(F32)
    x1 = x_ref[...] + _mm(merged, wo_ref[...], precise)
    x1_ref[...] = x1
    xq_ref[...] = _mm(_rms(x1, gx_ref[...]), wxq_ref[...], precise).astype(xq_ref.dtype)


def _merge(x, attn, sga, sb, wpa, wo, gx, wxq, tm, precise):
    t = x.shape[0]
    row = lambda n: pl.BlockSpec((tm, n), lambda i: (i, 0))
    full = lambda a: pl.BlockSpec(a.shape, lambda i: (0,) * a.ndim)
    return pl.pallas_call(
        functools.partial(_merge_kernel, precise=precise), grid=(t // tm,),
        in_specs=[row(D_MODEL), row(D_ATTN), row(D_MODEL), row(D_MODEL), full(wpa), full(wo), full(gx),
                  full(wxq)],
        out_specs=[row(D_MODEL), row(D_XATTN)],
        out_shape=[jax.ShapeDtypeStruct((t, D_MODEL), F32),
                   jax.ShapeDtypeStruct((t, D_XATTN), F32 if precise else BF16)],
        compiler_params=_cparams("parallel"), name="merge",
    )(x, attn, sga, sb, wpa, wo, gx, wxq)


def _xattn_kernel(q_ref, mk_ref, mv_ref, o_ref, *, precise):
    q = q_ref[0]
    mk = mk_ref[0]
    mv = mv_ref[0]
    outs = []
    for h in range(N_HEADS_X):
        cs = slice(HEAD_DIM_X * h, HEAD_DIM_X * (h + 1))
        s = _mm(q[:, cs], mk[:, cs], precise, nt=True) * (HEAD_DIM_X ** -0.5)
        e = jnp.exp(s - jnp.max(s, axis=1, keepdims=True))
        p = e / jnp.sum(e, axis=1, keepdims=True)
        outs.append(_mm(p, mv[:, cs], precise))
    o_ref[0] = jnp.concatenate(outs, axis=1).astype(o_ref.dtype)


def _xattn(q3, mk3, mv3, tq, precise):
    b, s, _ = q3.shape
    qspec = pl.BlockSpec((1, tq, D_XATTN), lambda bi, si: (bi, si, 0))
    mspec = pl.BlockSpec((1, N_MEM, D_XATTN), lambda bi, si: (bi, 0, 0))
    return pl.pallas_call(
        functools.partial(_xattn_kernel, precise=precise), grid=(b, s // tq),
        in_specs=[qspec, mspec, mspec], out_specs=qspec,
        out_shape=jax.ShapeDtypeStruct(q3.shape, q3.dtype),
        compiler_params=_cparams("parallel", "parallel"), name="xattn",
    )(q3, mk3, mv3)


GROUP_LANE0 = N_EXPERTS


def _post_kernel(x1_ref, xo_ref, wxo_ref, gf_ref, wrh_ref, wrl_ref, br_ref, x2_ref, xn_ref, cmb_ref, *, precise):
    x2 = x1_ref[...] + _mm(xo_ref[...], wxo_ref[...], precise)
    x2_ref[...] = x2
    t = _rms(x2, gf_ref[...])
    thi, tlo = _split_bf16(t)
    xn_ref[...] = thi
    logits = _dot(thi, wrh_ref[...]) + _dot(tlo, wrh_ref[...]) + _dot(thi, wrl_ref[...]) + br_ref[...]

    lane_i = lax.broadcasted_iota(I32, logits.shape, 1)
    lane = lane_i.astype(F32)
    lane_group = (lane_i >> (EXPERTS_PER_GROUP.bit_length() - 1)).astype(F32)
    isg = (lane_i >= GROUP_LANE0) & (lane_i < GROUP_LANE0 + N_EXPERT_GROUPS)
    gmax = jnp.max(jnp.where(isg, logits, BIG_NEG), axis=1, keepdims=True)
    eg = jnp.where(isg, jnp.exp(jnp.where(isg, logits, gmax) - gmax), 0.0)
    gp = eg / jnp.sum(eg, axis=1, keepdims=True)
    pg = jnp.max(jnp.where(isg, gp, -1.0), axis=1, keepdims=True)
    gi = jnp.min(jnp.where(isg & (gp == pg), lane, NO_IDX), axis=1, keepdims=True) - GROUP_LANE0

    insel = (lane_i < N_EXPERTS) & (lane_group == gi)
    el = jnp.where(insel, logits, BIG_NEG)
    m1 = jnp.max(el, axis=1, keepdims=True)
    i1 = jnp.min(jnp.where(insel & (el == m1), lane, NO_IDX), axis=1, keepdims=True)
    rest = insel & (lane != i1)
    el2 = jnp.where(rest, logits, BIG_NEG)
    m2 = jnp.max(el2, axis=1, keepdims=True)
    i2 = jnp.min(jnp.where(rest & (el2 == m2), lane, NO_IDX), axis=1, keepdims=True)
    e2 = jnp.exp(m2 - m1)
    den = 1.0 + e2
    cmb_ref[...] = jnp.where(lane == i1, (1.0 / den) * pg, jnp.where(lane == i2, (e2 / den) * pg, 0.0))


def _post(x1, xo, wxo, gf, wrh, wrl, br, tm, precise):
    t = x1.shape[0]
    row = lambda n: pl.BlockSpec((tm, n), lambda i: (i, 0))
    full = lambda a: pl.BlockSpec(a.shape, lambda i: (0,) * a.ndim)
    return pl.pallas_call(
        functools.partial(_post_kernel, precise=precise), grid=(t // tm,),
        in_specs=[row(D_MODEL), row(D_XATTN), full(wxo), full(gf), full(wrh), full(wrl), full(br)],
        out_specs=[row(D_MODEL), row(D_MODEL), row(LANES)],
        out_shape=[jax.ShapeDtypeStruct((t, D_MODEL), F32), jax.ShapeDtypeStruct((t, D_MODEL), BF16),
                   jax.ShapeDtypeStruct((t, LANES), F32)],
        compiler_params=_cparams("parallel"), name="post",
    )(x1, xo, wxo, gf, wrh, wrl, br)


def _moe_kernel(x2_ref, xn_ref, cmb_ref, w13_ref, w2_ref, gfin_ref, y_ref, acc_ref):
    e = pl.program_id(1)

    @pl.when(e == 0)
    def _():
        acc_ref[...] = jnp.zeros_like(acc_ref)

    h = _dot(xn_ref[...], w13_ref[0])
    cmb = cmb_ref[...]
    lane = lax.broadcasted_iota(I32, cmb.shape, 1)
    cw = jnp.sum(jnp.where(lane == e, cmb, 0.0), axis=1, keepdims=True)
    hd = jax.nn.silu(h[:, :D_FF_EXPERT]) * h[:, D_FF_EXPERT:] * cw
    acc_ref[...] += _dot(hd.astype(BF16), w2_ref[0])

    @pl.when(e == N_EXPERTS - 1)
    def _():
        y_ref[...] = _rms(x2_ref[...] + acc_ref[...], gfin_ref[...])


def _moe(x2, xn, cmb, w13, w2, gfin, tm):
    t = x2.shape[0]
    row = lambda n: pl.BlockSpec((tm, n), lambda i, e: (i, 0))
    return pl.pallas_call(
        _moe_kernel, grid=(t // tm, N_EXPERTS),
        in_specs=[row(D_MODEL), row(D_MODEL), row(LANES),
                  pl.BlockSpec((1, D_MODEL, 2 * D_FF_EXPERT), lambda i, e: (e, 0, 0)),
                  pl.BlockSpec((1, D_FF_EXPERT, D_MODEL), lambda i, e: (e, 0, 0)),
                  pl.BlockSpec(gfin.shape, lambda i, e: (0, 0))],
        out_specs=row(D_MODEL),
        out_shape=jax.ShapeDtypeStruct((t, D_MODEL), F32),
        scratch_shapes=[pltpu.VMEM((tm, D_MODEL), F32)],
        compiler_params=_cparams("parallel", "arbitrary"), name="moe",
    )(x2, xn, cmb, w13, w2, gfin)


PAGES_PER_STEP = 16
PAGES_PER_BLOCK = MOBA_BLOCK // PAGE_SIZE
BLOCKS_PER_STEP = PAGES_PER_STEP // PAGES_PER_BLOCK


def _sscore_kernel(pt_ref, q_ref, *refs, n_past):
    del pt_ref
    pages, o_ref, ks_ref = refs[:PAGES_PER_STEP], refs[PAGES_PER_STEP], refs[PAGES_PER_STEP + 1]
    s = pl.program_id(1)
    lane = lax.broadcasted_iota(I32, (D_ATTN, LANES), 1)

    @pl.when(s == 0)
    def _():
        ks_ref[...] = jnp.zeros_like(ks_ref)

    ks = ks_ref[...]
    for r in range(BLOCKS_PER_STEP):
        acc = pages[PAGES_PER_BLOCK * r][0].reshape(D_ATTN, PAGE_SIZE)
        for t in range(1, PAGES_PER_BLOCK):
            acc = acc + pages[PAGES_PER_BLOCK * r + t][0].reshape(D_ATTN, PAGE_SIZE)
        ks = jnp.where(lane == s * BLOCKS_PER_STEP + r, jnp.sum(acc, axis=1, keepdims=True), ks)
    ks_ref[...] = ks

    @pl.when(s == pl.num_programs(1) - 1)
    def _():
        hrow = lax.broadcasted_iota(I32, (N_HEADS_A, D_ATTN), 0)
        hcol = lax.broadcasted_iota(I32, (N_HEADS_A, D_ATTN), 1) >> (HEAD_DIM_A.bit_length() - 1)
        qbd = jnp.where(hrow == hcol, jnp.broadcast_to(q_ref[0], (N_HEADS_A, D_ATTN)), 0.0)
        sc = _mm(qbd, ks, True) * (1.0 / MOBA_BLOCK)
        bl_i = lax.broadcasted_iota(I32, sc.shape, 1)
        bl = bl_i.astype(F32)
        valid = bl_i < n_past
        taken = jnp.zeros(sc.shape, jnp.bool_)
        out = jnp.zeros(sc.shape, F32)
        for r in range(MOBA_TOPK):
            sm = jnp.where(valid & jnp.logical_not(taken), sc, BIG_NEG)
            mx = jnp.max(sm, axis=1, keepdims=True)
            idx = jnp.min(jnp.where((sm == mx) & (sm > BIG_NEG), bl, NO_IDX), axis=1, keepdims=True)
            taken = taken | (bl == idx)
            out = jnp.where(bl_i == r, idx, out)
        o_ref[0] = out.astype(I32)


def _sample_scores(cache4, page_table, q3, n_past):
    nb, npages = page_table.shape

    def pspec(r):
        return pl.BlockSpec((1, N_HEADS_A, HEAD_DIM_A, PAGE_SIZE),
                            lambda b, s, pt: (pt[b, s * PAGES_PER_STEP + r], 0, 0, 0))

    return pl.pallas_call(
        functools.partial(_sscore_kernel, n_past=n_past),
        grid_spec=pltpu.PrefetchScalarGridSpec(
            num_scalar_prefetch=1, grid=(nb, npages // PAGES_PER_STEP),
            in_specs=[pl.BlockSpec((1, 1, D_ATTN), lambda b, s, pt: (b, 0, 0))]
            + [pspec(r) for r in range(PAGES_PER_STEP)],
            out_specs=pl.BlockSpec((1, N_HEADS_A, LANES), lambda b, s, pt: (b, 0, 0)),
            scratch_shapes=[pltpu.VMEM((D_ATTN, LANES), F32)]),
        out_shape=jax.ShapeDtypeStruct((nb, N_HEADS_A, LANES), I32),
        compiler_params=_cparams("parallel", "arbitrary"), name="sample_scores",
    )(page_table, q3, *([cache4] * PAGES_PER_STEP))


PAGES_PER_HEAD = MOBA_TOPK * PAGES_PER_BLOCK
N_SEL_PAGES = N_HEADS_A * PAGES_PER_HEAD


def _sattn_kernel(sel_ref, pt_ref, q_ref, kn_ref, vn_ref, *refs):
    del sel_ref, pt_ref
    kp, vp, o_ref = refs[:N_SEL_PAGES], refs[N_SEL_PAGES:2 * N_SEL_PAGES], refs[2 * N_SEL_PAGES]
    for h in range(N_HEADS_A):
        qh = q_ref[0, h:h + 1, :] * (HEAD_DIM_A ** -0.5)
        q8 = jnp.broadcast_to(qh, (8, HEAD_DIM_A))
        kts = [kp[h * PAGES_PER_HEAD + r][0, 0] for r in range(PAGES_PER_HEAD)]
        vts = [vp[h * PAGES_PER_HEAD + r][0, 0] for r in range(PAGES_PER_HEAD)]
        ss = [_mm(q8, kt, True) for kt in kts]
        s_self = jnp.sum(qh * kn_ref[0, h:h + 1, :], axis=1, keepdims=True)
        mx = s_self
        for s in ss:
            mx = jnp.maximum(mx, jnp.max(s, axis=1, keepdims=True)[:1])
        p_self = jnp.exp(s_self - mx)
        den = p_self
        acc8 = jnp.zeros((8, HEAD_DIM_A), F32)
        for s, vt in zip(ss, vts):
            p = jnp.exp(s - mx)
            den = den + jnp.sum(p, axis=1, keepdims=True)[:1]
            acc8 = acc8 + _mm(p, vt, True, nt=True)
        o_ref[0, h:h + 1, :] = (p_self * vn_ref[0, h:h + 1, :] + acc8[:1]) / den


def _sample_attn(sel_flat, pt_flat, q3, kn3, vn3, ck4, cv4, n_pages):
    nb = q3.shape[0]

    def pspec(slot):
        h, rem = divmod(slot, PAGES_PER_HEAD)
        r, half = divmod(rem, PAGES_PER_BLOCK)

        def imap(b, sel, pt):
            blk = sel[(b * N_HEADS_A + h) * MOBA_TOPK + r]
            return (pt[b * n_pages + blk * PAGES_PER_BLOCK + half], h, 0, 0)

        return pl.BlockSpec((1, 1, HEAD_DIM_A, PAGE_SIZE), imap)

    tok = pl.BlockSpec((1, N_HEADS_A, HEAD_DIM_A), lambda b, sel, pt: (b, 0, 0))
    return pl.pallas_call(
        _sattn_kernel,
        grid_spec=pltpu.PrefetchScalarGridSpec(
            num_scalar_prefetch=2, grid=(nb,),
            in_specs=[tok, tok, tok] + [pspec(s) for s in range(N_SEL_PAGES)] * 2,
            out_specs=tok),
        out_shape=jax.ShapeDtypeStruct((nb, N_HEADS_A, HEAD_DIM_A), F32),
        compiler_params=_cparams("arbitrary"), name="sample_attn",
    )(sel_flat, pt_flat, q3, kn3, vn3, *([ck4] * N_SEL_PAGES), *([cv4] * N_SEL_PAGES))


def _tail(x, attn, sga, sb, mk3, mv3, w, batch, seq, tm, tq, precise):
    x1, xq = _merge(x, attn, sga, sb, w['wpa'], w['wo'], w['gx'], w['wxq'], tm, precise)
    xq3 = xq.reshape(batch, seq, D_XATTN)
    pad = (-seq) % tq
    if pad:
        xq3 = jnp.pad(xq3, ((0, 0), (0, pad), (0, 0)))
    xo = _xattn(xq3, mk3, mv3, tq, precise)[:, :seq].reshape(batch * seq, D_XATTN)
    x2, xn, cmb = _post(x1, xo, w['wxo'], w['gf'], w['wrh'], w['wrl'], w['br'], tm, precise)
    return _moe(x2, xn, cmb, w['w13'], w['w2'], w['gfin'], tm)


def kernel(x_prompt, x_sample, mem_prompt, cache_k, cache_v, page_table, state_ssm_re, state_ssm_im,
           cache_mem_k, cache_mem_v, g_mix, w_in, ssm_lambda_re, ssm_lambda_im, ssm_log_dt,
           ssm_b_re, ssm_b_im, ssm_c_re, ssm_c_im, ssm_d, w_glu, w_pa, w_pb, w_o, g_x, g_mem,
           w_xq, w_xk, w_xv, w_xo, g_ffn, w_group, b_group, w_erouter, b_erouter, w1, w3, w2, g_final):
    depth = w_in.shape[0]
    assert depth == 1
    l = 0
    bp, sp, _ = x_prompt.shape
    bs, ss, _ = x_sample.shape
    assert ss == 1
    n_pages = page_table.shape[1]
    past_len = n_pages * PAGE_SIZE
    assert past_len % MOBA_BLOCK == 0 and sp % MOBA_BLOCK == 0

    row = lambda a: a.reshape(1, -1).astype(F32)
    bf = lambda a: a.astype(BF16)
    wr = jnp.zeros((D_MODEL, LANES), F32)
    wr = wr.at[:, :N_EXPERTS].set(w_erouter[l]).at[:, GROUP_LANE0:GROUP_LANE0 + N_EXPERT_GROUPS].set(w_group[l])
    wrh, wrl = _split_bf16(wr)
    br = jnp.zeros((1, LANES), F32)
    br = br.at[0, :N_EXPERTS].set(b_erouter[l]).at[0, GROUP_LANE0:GROUP_LANE0 + N_EXPERT_GROUPS].set(b_group[l])
    ws = dict(wpa=w_pa[l], wo=w_o[l], gx=row(g_x[l]), wxq=w_xq[l], wxo=w_xo[l], gf=row(g_ffn[l]),
              wrh=wrh, wrl=wrl, br=br, w13=bf(jnp.concatenate([w1[l], w3[l]], axis=-1)), w2=bf(w2[l]),
              gfin=row(g_final))
    wp = dict(ws, wpa=bf(w_pa[l]), wo=bf(w_o[l]), wxq=bf(w_xq[l]), wxo=bf(w_xo[l]))
    gmix = row(g_mix[l])

    ar, ai, bbt_re, bbt_im = _ssm_params(ssm_lambda_re[l], ssm_lambda_im[l], ssm_log_dt[l], ssm_b_re[l],
                                         ssm_b_im[l])
    bb, cc = _s5_weights(ar, ai, bbt_re, bbt_im, ssm_c_re[l], ssm_c_im[l])
    d_row = row(ssm_d[l])

    tp = bp * sp
    xp = x_prompt.reshape(tp, D_MODEL)
    q, k, v, kb, vb, u, sga, gb = _inproj(xp, gmix, bf(w_in[l]), 512)
    attn = _moba_prompt(q, kb, vb, bp, sp)
    h0 = jnp.zeros((bp, 2 * N_STATE), F32)
    sb3, ht = _s5(u.reshape(bp, sp, D_SSM), gb.reshape(bp, sp, D_MODEL), h0, ar, ai, bf(bb), bf(cc), d_row,
                  bf(w_glu[l]), bf(w_pb[l]), bn=bp, lc=128, precise=False)
    sr_p, si_p = _lanes_to_state(ht)
    mk_p, mv_p = _memkv(mem_prompt.reshape(bp * N_MEM, D_MODEL), row(g_mem[l]), bf(w_xk[l]), bf(w_xv[l]), 256)
    y_p = _tail(xp, attn, sga, sb3.reshape(tp, D_MODEL), mk_p.reshape(bp, N_MEM, D_XATTN),
                mv_p.reshape(bp, N_MEM, D_XATTN), wp, bp, sp, 1024, 512, False)

    xs = x_sample.reshape(bs, D_MODEL)
    proj_s = _inproj_precise(xs, gmix, w_in[l])
    q_s, k_s, v_s = (proj_s[:, D_ATTN * n:D_ATTN * (n + 1)] for n in range(3))
    u_s = proj_s[:, 3 * D_ATTN:GA_COL0]
    sga_s = proj_s[:, GA_COL0:GA_COL0 + D_MODEL]
    gb_s = proj_s[:, GA_COL0 + D_MODEL:]
    ck4 = jnp.transpose(cache_k[l], (0, 2, 3, 1))
    cv4 = jnp.transpose(cache_v[l], (0, 2, 3, 1))
    sel = _sample_scores(ck4, page_table, q_s.reshape(bs, 1, D_ATTN), past_len // MOBA_BLOCK)
    sel_flat = sel[:, :, :MOBA_TOPK].reshape(-1)
    hsplit = lambda a: a.reshape(bs, N_HEADS_A, HEAD_DIM_A)
    attn_s = _sample_attn(sel_flat, page_table.reshape(-1), hsplit(q_s), hsplit(k_s), hsplit(v_s), ck4, cv4,
                          n_pages)
    h0_s = _state_to_lanes(state_ssm_re[l].reshape(bs, N_STATE), state_ssm_im[l].reshape(bs, N_STATE))
    sb_s, ht_s = _s5(u_s.reshape(1, bs, D_SSM), gb_s.reshape(1, bs, D_MODEL), h0_s, ar, ai, bb, cc, d_row,
                     w_glu[l], w_pb[l], bn=bs, lc=1, precise=True)
    sr_s, si_s = _lanes_to_state(ht_s)
    y_s = _tail(xs, attn_s.reshape(bs, D_ATTN), sga_s, sb_s.reshape(bs, D_MODEL),
                cache_mem_k[l].reshape(bs, N_MEM, D_XATTN), cache_mem_v[l].reshape(bs, N_MEM, D_XATTN),
                ws, bs, 1, bs, 16, True)

    kv5 = lambda a, b, s: a.reshape(1, b, s, N_HEADS_A, HEAD_DIM_A)
    st4 = lambda a: a[None]
    mem5 = lambda a: a.reshape(1, bp, N_MEM, N_HEADS_X, HEAD_DIM_X)
    return (y_p.reshape(bp, sp, D_MODEL), y_s.reshape(bs, 1, D_MODEL),
            kv5(k, bp, sp), kv5(v, bp, sp), st4(sr_p), st4(si_p), mem5(mk_p), mem5(mv_p),
            kv5(k_s, bs, 1), kv5(v_s, bs, 1), st4(sr_s), st4(si_s))
```

```python
import functools

import jax
import jax.numpy as jnp
from jax import lax
from jax.experimental import pallas as pl
from jax.experimental.pallas import tpu as pltpu

F32 = jnp.float32
BF16 = jnp.bfloat16
I32 = jnp.int32

D_MODEL = 1024
N_HEADS_A = 8
HEAD_DIM_A = 64
D_ATTN = N_HEADS_A * HEAD_DIM_A
MOBA_BLOCK = 256
BLOCK_SHIFT = MOBA_BLOCK.bit_length() - 1
MOBA_TOPK = 3
D_SSM = 512
SSM_GROUP = 16
N_SSM_GROUPS = D_SSM // SSM_GROUP
SSM_STATE = 64
N_STATE = N_SSM_GROUPS * SSM_STATE
N_MEM = 256
N_HEADS_X = 4
HEAD_DIM_X = 128
D_XATTN = N_HEADS_X * HEAD_DIM_X
N_EXPERT_GROUPS = 4
EXPERTS_PER_GROUP = 4
N_EXPERTS = N_EXPERT_GROUPS * EXPERTS_PER_GROUP
D_FF_EXPERT = 256
D_IN_PROJ = 3 * D_ATTN + D_SSM + 2 * D_MODEL
RMS_EPS = 1e-6
NEG_INF = -1e30
PAGE_SIZE = 128

LANES = 128
VMEM_LIMIT = 56 * 1024 * 1024
BIG_NEG = -3e38
M_INIT = -1e29
NO_IDX = 1e9


def _cparams(*sem):
    return pltpu.CompilerParams(dimension_semantics=sem, vmem_limit_bytes=VMEM_LIMIT)


def _rms(x, g):
    return x * lax.rsqrt(jnp.mean(x * x, axis=-1, keepdims=True) + RMS_EPS) * g


def _dot(a, b):
    return jnp.dot(a, b, preferred_element_type=F32)


def _dot_nt(a, b):
    return lax.dot_general(a, b, (((1,), (1,)), ((), ())), preferred_element_type=F32)


def _split_bf16(x):
    hi = x.astype(BF16)
    return hi, (x - hi.astype(F32)).astype(BF16)


def _mm(a, w, precise, nt=False):
    dot = _dot_nt if nt else _dot
    if not precise:
        return dot(a.astype(BF16), w.astype(BF16))
    ah, al = _split_bf16(a.astype(F32))
    wh, wl = _split_bf16(w)
    return dot(ah, wh) + dot(al, wh) + dot(ah, wl)


def _inproj_kernel(x_ref, g_ref, w_ref, q_ref, k_ref, v_ref, kb_ref, vb_ref, u_ref, sga_ref, gb_ref):
    xn = _rms(x_ref[...], g_ref[...]).astype(BF16)

    def mm(c0, n):
        return _dot(xn, w_ref[:, c0:c0 + n])

    q_ref[...] = mm(0, D_ATTN)
    k = mm(D_ATTN, D_ATTN)
    k_ref[...] = k
    kb_ref[...] = k.astype(BF16)
    v = mm(2 * D_ATTN, D_ATTN)
    v_ref[...] = v
    vb_ref[...] = v.astype(BF16)
    u_ref[...] = mm(3 * D_ATTN, D_SSM)
    sga_ref[...] = jax.nn.sigmoid(mm(3 * D_ATTN + D_SSM, D_MODEL)).astype(BF16)
    gb_ref[...] = mm(3 * D_ATTN + D_SSM + D_MODEL, D_MODEL)


def _inproj(x, g, w_bf, tm):
    t = x.shape[0]
    row = lambda n: pl.BlockSpec((tm, n), lambda i: (i, 0))
    full = lambda a: pl.BlockSpec(a.shape, lambda i: (0,) * a.ndim)
    shp = lambda n, dt: jax.ShapeDtypeStruct((t, n), dt)
    return pl.pallas_call(
        _inproj_kernel,
        grid=(t // tm,),
        in_specs=[row(D_MODEL), full(g), full(w_bf)],
        out_specs=[row(D_ATTN)] * 5 + [row(D_SSM), row(D_MODEL), row(D_MODEL)],
        out_shape=[shp(D_ATTN, F32), shp(D_ATTN, F32), shp(D_ATTN, F32), shp(D_ATTN, BF16),
                   shp(D_ATTN, BF16), shp(D_SSM, F32), shp(D_MODEL, BF16), shp(D_MODEL, F32)],
        compiler_params=_cparams("parallel"),
        name="inproj",
    )(x, g, w_bf)


INPROJ_COLS = 512
GA_COL0 = 3 * D_ATTN + D_SSM


def _inproj_precise_kernel(x_ref, g_ref, w_ref, o_ref):
    c0 = pl.program_id(0) * INPROJ_COLS
    acc = _mm(_rms(x_ref[...], g_ref[...]), w_ref[...], True)
    is_ga = (c0 >= GA_COL0) & (c0 < GA_COL0 + D_MODEL)

    @pl.when(is_ga)
    def _():
        o_ref[...] = jax.nn.sigmoid(acc)

    @pl.when(jnp.logical_not(is_ga))
    def _():
        o_ref[...] = acc


def _inproj_precise(x, g, w):
    t = x.shape[0]
    full = lambda a: pl.BlockSpec(a.shape, lambda j: (0,) * a.ndim)
    return pl.pallas_call(
        _inproj_precise_kernel,
        grid=(D_IN_PROJ // INPROJ_COLS,),
        in_specs=[full(x), full(g), pl.BlockSpec((D_MODEL, INPROJ_COLS), lambda j: (0, j))],
        out_specs=pl.BlockSpec((t, INPROJ_COLS), lambda j: (0, j)),
        out_shape=jax.ShapeDtypeStruct((t, D_IN_PROJ), F32),
        compiler_params=_cparams("parallel"),
        name="inproj_sample",
    )(x, g, w)


def _ssm_param_kernel(lr_ref, li_ref, dt_ref, br_ref, bi_ref, ar_ref, ai_ref, bbr_ref, bbi_ref):
    lr = jnp.minimum(lr_ref[...], -1e-4)
    li = li_ref[...]
    dt = jnp.exp(dt_ref[...])
    mag = jnp.exp(lr * dt)
    ar = mag * jnp.cos(li * dt)
    ai = mag * jnp.sin(li * dt)
    den = lr * lr + li * li
    nr = ar - 1.0
    cr = (nr * lr + ai * li) / den
    ci = (ai * lr - nr * li) / den
    ar_ref[...] = ar
    ai_ref[...] = ai
    br = br_ref[...]
    bi = bi_ref[...]
    bbr_ref[...] = cr * br - ci * bi
    bbi_ref[...] = cr * bi + ci * br


def _ssm_params(lam_re, lam_im, log_dt, b_re, b_im):
    n = N_STATE
    lr = lam_re.reshape(1, n)
    li = lam_im.reshape(1, n)
    dt = jnp.broadcast_to(log_dt[:, None], (N_SSM_GROUPS, SSM_STATE)).reshape(1, n)
    brt = b_re.reshape(n, SSM_GROUP).T
    bit = b_im.reshape(n, SSM_GROUP).T
    row = jax.ShapeDtypeStruct((1, n), F32)
    mat = jax.ShapeDtypeStruct((SSM_GROUP, n), F32)
    return pl.pallas_call(_ssm_param_kernel, out_shape=[row, row, mat, mat], name="ssm_params")(
        lr, li, dt, brt, bit)


HALF_STATE = N_STATE // 2
SCAN_COLS = 512
SCAN_TILES = SCAN_COLS // 128


def _s5_kernel(u_ref, gb_ref, h0_ref, ar_ref, ai_ref, bb_ref, cc_ref, d_ref, wglu_ref, wpb_ref,
               sb_ref, ht_ref, s_ref, carry_ref, *, bn, lc, precise):
    m = bn * lc

    @pl.when(pl.program_id(0) == 0)
    def _():
        carry_ref[...] = h0_ref[...]

    u = u_ref[...].reshape(m, D_SSM)
    for j in range(16):
        sec, jj = divmod(j, 4)
        gbase = (sec // 2) * 16 + jj * 4
        lt = (gbase * SSM_GROUP) // LANES
        bu = _mm(u[:, LANES * lt:LANES * (lt + 1)], bb_ref[j], precise)
        s_ref[2 * j] = bu[:, :LANES]
        s_ref[2 * j + 1] = bu[:, LANES:]

    tiles_half = HALF_STATE // LANES
    for h in range(2):
        for c in range(HALF_STATE // SCAN_COLS):
            re_t = [2 * tiles_half * h + SCAN_TILES * c + n for n in range(SCAN_TILES)]
            im_t = [t + tiles_half for t in re_t]
            a_t = [tiles_half * h + SCAN_TILES * c + n for n in range(SCAN_TILES)]
            ars = [jnp.broadcast_to(ar_ref[:, LANES * t:LANES * (t + 1)], (bn, LANES)) for t in a_t]
            ais = [jnp.broadcast_to(ai_ref[:, LANES * t:LANES * (t + 1)], (bn, LANES)) for t in a_t]

            def body(t, carry, re_t=re_t, im_t=im_t, ars=ars, ais=ais):
                rows = pl.ds(t, bn, stride=lc) if lc > 1 else pl.ds(0, bn)
                out = []
                for n in range(SCAN_TILES):
                    xr, xi = carry[2 * n], carry[2 * n + 1]
                    nxr = ars[n] * xr - ais[n] * xi + s_ref[re_t[n], rows, :]
                    nxi = ars[n] * xi + ais[n] * xr + s_ref[im_t[n], rows, :]
                    s_ref[re_t[n], rows, :] = nxr
                    s_ref[im_t[n], rows, :] = nxi
                    out += [nxr, nxi]
                return tuple(out)

            x0 = []
            for n in range(SCAN_TILES):
                x0 += [carry_ref[:, LANES * re_t[n]:LANES * (re_t[n] + 1)],
                       carry_ref[:, LANES * im_t[n]:LANES * (im_t[n] + 1)]]
            xs = lax.fori_loop(0, lc, body, tuple(x0), unroll=min(lc, 8))
            for n in range(SCAN_TILES):
                carry_ref[:, LANES * re_t[n]:LANES * (re_t[n] + 1)] = xs[2 * n]
                carry_ref[:, LANES * im_t[n]:LANES * (im_t[n] + 1)] = xs[2 * n + 1]
    ht_ref[...] = carry_ref[...]

    ys = []
    for h in range(2):
        sdt = F32 if precise else BF16
        xh = jnp.concatenate([s_ref[2 * tiles_half * h + n].astype(sdt) for n in range(2 * tiles_half)], axis=1)
        ys.append(_mm(xh, cc_ref[h], precise))
    y = jnp.concatenate(ys, axis=1) + d_ref[...] * u
    z = jax.nn.gelu(y)
    s5 = z * jax.nn.sigmoid(_mm(z, wglu_ref[...], precise))
    pb = _mm(s5, wpb_ref[...], precise)
    gb = gb_ref[...].reshape(m, D_MODEL)
    sb_ref[...] = (jax.nn.sigmoid(gb) * pb).astype(sb_ref.dtype).reshape(sb_ref.shape)


def _s5(u3, gb3, h0, ar, ai, bb, cc, d, wglu, wpb, bn, lc, precise):
    nb, s, _ = u3.shape
    rows = bn * lc // nb
    nchunk = s // rows
    full = lambda a: pl.BlockSpec(a.shape, lambda c: (0,) * a.ndim)
    blk = lambda n: pl.BlockSpec((nb, rows, n), lambda c: (0, c, 0))
    return pl.pallas_call(
        functools.partial(_s5_kernel, bn=bn, lc=lc, precise=precise),
        grid=(nchunk,),
        in_specs=[blk(D_SSM), blk(D_MODEL), full(h0), full(ar), full(ai), full(bb), full(cc), full(d),
                  full(wglu), full(wpb)],
        out_specs=[blk(D_MODEL), full(h0)],
        out_shape=[jax.ShapeDtypeStruct((nb, s, D_MODEL), F32 if precise else BF16),
                   jax.ShapeDtypeStruct(h0.shape, F32)],
        scratch_shapes=[pltpu.VMEM((2 * N_STATE // LANES, bn * lc, LANES), F32),
                        pltpu.VMEM((bn, 2 * N_STATE), F32)],
        compiler_params=_cparams("arbitrary"),
        name="s5",
    )(u3, gb3, h0, ar, ai, bb, cc, d, wglu, wpb)


def _s5_weights(ar, ai, bbt_re, bbt_im, c_re, c_im):
    g, p, h = N_SSM_GROUPS, SSM_STATE, SSM_GROUP
    eye = jnp.eye(g, dtype=F32)

    def bfull(bt):
        b = bt.reshape(h, g, p)
        return jnp.einsum('hgp,gk->ghkp', b, eye).reshape(g * h, g * p)

    bre, bim = bfull(bbt_re), bfull(bbt_im)
    tiles = []
    for j in range(16):
        sec, jj = divmod(j, 4)
        src = bre if sec % 2 == 0 else bim
        gbase = (sec // 2) * 16 + jj * 4
        lt = (gbase * h) // LANES
        tiles.append(src[LANES * lt:LANES * (lt + 1), gbase * p:(gbase + 4) * p])
    bb = jnp.stack(tiles)

    def cfull(c):
        return jnp.einsum('ghp,gk->gpkh', c, eye).reshape(g * p, g * h)

    cre, cim = cfull(c_re), cfull(c_im)
    halves = []
    for hh in range(2):
        rs = slice(HALF_STATE * hh, HALF_STATE * (hh + 1))
        cs = slice(256 * hh, 256 * (hh + 1))
        halves.append(jnp.concatenate([cre[rs, cs], -cim[rs, cs]], axis=0))
    return bb, jnp.stack(halves)


def _state_to_lanes(re, im):
    b = re.shape[0]
    r = re.reshape(b, 2, HALF_STATE)
    i = im.reshape(b, 2, HALF_STATE)
    return jnp.concatenate([r[:, 0], i[:, 0], r[:, 1], i[:, 1]], axis=1)


def _lanes_to_state(h):
    b = h.shape[0]
    h4 = h.reshape(b, 4, HALF_STATE)
    re = jnp.concatenate([h4[:, 0], h4[:, 2]], axis=1).reshape(b, N_SSM_GROUPS, SSM_STATE)
    im = jnp.concatenate([h4[:, 1], h4[:, 3]], axis=1).reshape(b, N_SSM_GROUPS, SSM_STATE)
    return re, im


MOBA_CHUNK = 4


def _moba_kernel(q_ref, k_ref, v_ref, o_ref, ka0_ref, ka1_ref, selb_ref, *, seq):
    blk, hd = MOBA_BLOCK, HEAD_DIM_A
    nblk = seq // blk
    chunk_rows = MOBA_CHUNK * blk

    k = k_ref[...]
    rblk = lax.broadcasted_iota(I32, (seq, LANES), 0) >> BLOCK_SHIFT
    lane = lax.broadcasted_iota(I32, (seq, LANES), 1)
    kf = k.astype(F32)
    ka0_ref[...] = jnp.where(lane < hd, kf, jnp.where(lane - hd == rblk, 1.0, 0.0)).astype(BF16)
    ka1_ref[...] = jnp.where(lane >= hd, kf, jnp.where(lane == rblk, 1.0, 0.0)).astype(BF16)
    r = lax.broadcasted_iota(I32, (LANES, seq), 0)
    cblk = lax.broadcasted_iota(I32, (LANES, seq), 1) >> BLOCK_SHIFT
    ind = jnp.where((r == cblk) | (r - hd == cblk), 1.0, 0.0).astype(BF16)
    kmean = _dot(ind, k) * (1.0 / blk)
    rr = lax.broadcasted_iota(I32, (LANES, LANES), 0)
    ll = lax.broadcasted_iota(I32, (LANES, LANES), 1)
    keep = ((rr < hd) & (ll >= hd)) | ((rr >= hd) & (ll < hd))
    ahi, alo = _split_bf16(jnp.where(keep, kmean, 0.0))

    qhi, qlo = _split_bf16(q_ref[...])
    sc = _dot_nt(qhi, ahi) + _dot_nt(qlo, ahi) + _dot_nt(qhi, alo)
    lane_f = lane.astype(F32)
    is_h0 = lane >= hd
    jidx = lane & (hd - 1)
    ind_lane = jidx < nblk
    valid = ind_lane & (jidx < rblk)
    taken = jnp.zeros((seq, LANES), jnp.bool_)
    for _ in range(MOBA_TOPK):
        sm = jnp.where(valid & jnp.logical_not(taken), sc, BIG_NEG)
        m0 = jnp.max(jnp.where(is_h0, sm, BIG_NEG), axis=1, keepdims=True)
        m1 = jnp.max(jnp.where(is_h0, BIG_NEG, sm), axis=1, keepdims=True)
        mx = jnp.where(is_h0, m0, m1)
        cand = jnp.where((sm == mx) & (sm > BIG_NEG), lane_f, NO_IDX)
        i0 = jnp.min(jnp.where(is_h0, cand, NO_IDX), axis=1, keepdims=True)
        i1 = jnp.min(jnp.where(is_h0, NO_IDX, cand), axis=1, keepdims=True)
        taken = taken | (lane_f == jnp.where(is_h0, i0, i1))
    selb_ref[...] = jnp.where(ind_lane & jnp.logical_not(taken), NEG_INF, 0.0).astype(BF16)

    lane_b = lax.broadcasted_iota(I32, (blk, LANES), 1)
    causal = lax.broadcasted_iota(I32, (blk, blk), 1) <= lax.broadcasted_iota(I32, (blk, blk), 0)
    heads = ((ka0_ref, lane_b < hd), (ka1_ref, lane_b >= hd))

    def qblock(i, _):
        off = pl.multiple_of(i * blk, blk)
        qs = q_ref[pl.ds(off, blk), :] * (hd ** -0.5)
        sb = selb_ref[pl.ds(off, blk), :].astype(F32)
        vd = v_ref[pl.ds(off, blk), :]
        qas, init = [], []
        for ka_ref, mine in heads:
            s = _dot_nt(jnp.where(mine, qs, 0.0).astype(BF16), ka_ref[pl.ds(off, blk), :])
            s = jnp.where(causal, s, NEG_INF)
            m = jnp.max(s, axis=1, keepdims=True)
            p = jnp.exp(s - m)
            init.append((m, jnp.sum(p, axis=1, keepdims=True), _dot(p.astype(BF16), vd)))
            qas.append(jnp.where(mine, qs, sb).astype(BF16))

        def chunk(c, carry):
            coff = pl.multiple_of(c * chunk_rows, chunk_rows)
            vc = v_ref[pl.ds(coff, chunk_rows), :]
            out = []
            for (ka_ref, _), qa, (m, l, acc) in zip(heads, qas, carry):
                s = _dot_nt(qa, ka_ref[pl.ds(coff, chunk_rows), :])
                m_new = jnp.maximum(m, jnp.max(s, axis=1, keepdims=True))
                alpha = jnp.exp(m - m_new)
                p = jnp.exp(s - m_new)
                out.append((m_new, alpha * l + jnp.sum(p, axis=1, keepdims=True),
                            alpha * acc + _dot(p.astype(BF16), vc)))
            return tuple(out)

        n_chunks = (i + MOBA_CHUNK - 1) >> (MOBA_CHUNK.bit_length() - 1)
        (_, l0, a0), (_, l1, a1) = lax.fori_loop(0, n_chunks, chunk, tuple(init))
        o_ref[pl.ds(off, blk), :] = jnp.where(lane_b < hd, a0 / l0, a1 / l1).astype(BF16)
        return 0

    lax.fori_loop(0, nblk, qblock, 0)


def _moba_prompt(q, kb, vb, batch, seq):
    spec = pl.BlockSpec((seq, LANES), lambda b, hp: (b, hp))
    return pl.pallas_call(
        functools.partial(_moba_kernel, seq=seq),
        grid=(batch, D_ATTN // LANES),
        in_specs=[spec, spec, spec],
        out_specs=spec,
        out_shape=jax.ShapeDtypeStruct((batch * seq, D_ATTN), BF16),
        scratch_shapes=[pltpu.VMEM((seq, LANES), BF16)] * 3,
        compiler_params=_cparams("parallel", "parallel"),
        name="moba_prompt",
    )(q, kb, vb)


def _memkv_kernel(m_ref, g_ref, wk_ref, wv_ref, mk_ref, mv_ref):
    mn = _rms(m_ref[...], g_ref[...]).astype(BF16)
    mk_ref[...] = _dot(mn, wk_ref[...])
    mv_ref[...] = _dot(mn, wv_ref[...])


def _memkv(mem, g, wk, wv, tm):
    t = mem.shape[0]
    row = lambda n: pl.BlockSpec((tm, n), lambda i: (i, 0))
    full = lambda a: pl.BlockSpec(a.shape, lambda i: (0,) * a.ndim)
    shp = jax.ShapeDtypeStruct((t, D_XATTN), F32)
    return pl.pallas_call(
        _memkv_kernel, grid=(t // tm,),
        in_specs=[row(D_MODEL), full(g), full(wk), full(wv)],
        out_specs=[row(D_XATTN)] * 2, out_shape=[shp, shp],
        compiler_params=_cparams("parallel"), name="memkv",
    )(mem, g, wk, wv)


def _merge_kernel(x_ref, at_ref, sga_ref, sb_ref, wpa_ref, wo_ref, gx_ref, wxq_ref, x1_ref, xq_ref, *, precise):
    pa = _mm(at_ref[...], wpa_ref[...], precise)
    merged = sga_ref[...].astype(F32) * pa + sb_ref[...].astype(F32)
    x1 = x_ref[...] + _mm(merged, wo_ref[...], precise)
    x1_ref[...] = x1
    xq_ref[...] = _mm(_rms(x1, gx_ref[...]), wxq_ref[...], precise).astype(xq_ref.dtype)


def _merge(x, attn, sga, sb, wpa, wo, gx, wxq, tm, precise):
    t = x.shape[0]
    row = lambda n: pl.BlockSpec((tm, n), lambda i: (i, 0))
    full = lambda a: pl.BlockSpec(a.shape, lambda i: (0,) * a.ndim)
    return pl.pallas_call(
        functools.partial(_merge_kernel, precise=precise), grid=(t // tm,),
        in_specs=[row(D_MODEL), row(D_ATTN), row(D_MODEL), row(D_MODEL), full(wpa), full(wo), full(gx),
                  full(wxq)],
        out_specs=[row(D_MODEL), row(D_XATTN)],
        out_shape=[jax.ShapeDtypeStruct((t, D_MODEL), F32),
                   jax.ShapeDtypeStruct((t, D_XATTN), F32 if precise else BF16)],
        compiler_params=_cparams("parallel"), name="merge",
    )(x, attn, sga, sb, wpa, wo, gx, wxq)


def _xattn_kernel(q_ref, mk_ref, mv_ref, o_ref, *, precise):
    q = q_ref[0]
    mk = mk_ref[0]
    mv = mv_ref[0]
    outs = []
    for h in range(N_HEADS_X):
        cs = slice(HEAD_DIM_X * h, HEAD_DIM_X * (h + 1))
        s = _mm(q[:, cs], mk[:, cs], precise, nt=True) * (HEAD_DIM_X ** -0.5)
        e = jnp.exp(s - jnp.max(s, axis=1, keepdims=True))
        p = e / jnp.sum(e, axis=1, keepdims=True)
        outs.append(_mm(p, mv[:, cs], precise))
    o_ref[0] = jnp.concatenate(outs, axis=1).astype(o_ref.dtype)


def _xattn(q3, mk3, mv3, tq, precise):
    b, s, _ = q3.shape
    qspec = pl.BlockSpec((1, tq, D_XATTN), lambda bi, si: (bi, si, 0))
    mspec = pl.BlockSpec((1, N_MEM, D_XATTN), lambda bi, si: (bi, 0, 0))
    return pl.pallas_call(
        functools.partial(_xattn_kernel, precise=precise), grid=(b, s // tq),
        in_specs=[qspec, mspec, mspec], out_specs=qspec,
        out_shape=jax.ShapeDtypeStruct(q3.shape, q3.dtype),
        compiler_params=_cparams("parallel", "parallel"), name="xattn",
    )(q3, mk3, mv3)


GROUP_LANE0 = N_EXPERTS


def _post_kernel(x1_ref, xo_ref, wxo_ref, gf_ref, wrh_ref, wrl_ref, br_ref, x2_ref, xn_ref, cmb_ref, *, precise):
    x2 = x1_ref[...] + _mm(xo_ref[...], wxo_ref[...], precise)
    x2_ref[...] = x2
    t = _rms(x2, gf_ref[...])
    thi, tlo = _split_bf16(t)
    xn_ref[...] = thi
    logits = _dot(thi, wrh_ref[...]) + _dot(tlo, wrh_ref[...]) + _dot(thi, wrl_ref[...]) + br_ref[...]

    lane_i = lax.broadcasted_iota(I32, logits.shape, 1)
    lane = lane_i.astype(F32)
    lane_group = (lane_i >> (EXPERTS_PER_GROUP.bit_length() - 1)).astype(F32)
    isg = (lane_i >= GROUP_LANE0) & (lane_i < GROUP_LANE0 + N_EXPERT_GROUPS)
    gmax = jnp.max(jnp.where(isg, logits, BIG_NEG), axis=1, keepdims=True)
    eg = jnp.where(isg, jnp.exp(jnp.where(isg, logits, gmax) - gmax), 0.0)
    gp = eg / jnp.sum(eg, axis=1, keepdims=True)
    pg = jnp.max(jnp.where(isg, gp, -1.0), axis=1, keepdims=True)
    gi = jnp.min(jnp.where(isg & (gp == pg), lane, NO_IDX), axis=1, keepdims=True) - GROUP_LANE0

    insel = (lane_i < N_EXPERTS) & (lane_group == gi)
    el = jnp.where(insel, logits, BIG_NEG)
    m1 = jnp.max(el, axis=1, keepdims=True)
    i1 = jnp.min(jnp.where(insel & (el == m1), lane, NO_IDX), axis=1, keepdims=True)
    rest = insel & (lane != i1)
    el2 = jnp.where(rest, logits, BIG_NEG)
    m2 = jnp.max(el2, axis=1, keepdims=True)
    i2 = jnp.min(jnp.where(rest & (el2 == m2), lane, NO_IDX), axis=1, keepdims=True)
    e2 = jnp.exp(m2 - m1)
    den = 1.0 + e2
    cmb_ref[...] = jnp.where(lane == i1, (1.0 / den) * pg, jnp.where(lane == i2, (e2 / den) * pg, 0.0))


def _post(x1, xo, wxo, gf, wrh, wrl, br, tm, precise):
    t = x1.shape[0]
    row = lambda n: pl.BlockSpec((tm, n), lambda i: (i, 0))
    full = lambda a: pl.BlockSpec(a.shape, lambda i: (0,) * a.ndim)
    return pl.pallas_call(
        functools.partial(_post_kernel, precise=precise), grid=(t // tm,),
        in_specs=[row(D_MODEL), row(D_XATTN), full(wxo), full(gf), full(wrh), full(wrl), full(br)],
        out_specs=[row(D_MODEL), row(D_MODEL), row(LANES)],
        out_shape=[jax.ShapeDtypeStruct((t, D_MODEL), F32), jax.ShapeDtypeStruct((t, D_MODEL), BF16),
                   jax.ShapeDtypeStruct((t, LANES), F32)],
        compiler_params=_cparams("parallel"), name="post",
    )(x1, xo, wxo, gf, wrh, wrl, br)


def _moe_kernel(x2_ref, xn_ref, cmb_ref, w13_ref, w2_ref, gfin_ref, y_ref, acc_ref):
    e = pl.program_id(1)

    @pl.when(e == 0)
    def _():
        acc_ref[...] = jnp.zeros_like(acc_ref)

    h = _dot(xn_ref[...], w13_ref[0])
    cmb = cmb_ref[...]
    lane = lax.broadcasted_iota(I32, cmb.shape, 1)
    cw = jnp.sum(jnp.where(lane == e, cmb, 0.0), axis=1, keepdims=True)
    hd = jax.nn.silu(h[:, :D_FF_EXPERT]) * h[:, D_FF_EXPERT:] * cw
    acc_ref[...] += _dot(hd.astype(BF16), w2_ref[0])

    @pl.when(e == N_EXPERTS - 1)
    def _():
        y_ref[...] = _rms(x2_ref[...] + acc_ref[...], gfin_ref[...])


def _moe(x2, xn, cmb, w13, w2, gfin, tm):
    t = x2.shape[0]
    row = lambda n: pl.BlockSpec((tm, n), lambda i, e: (i, 0))
    return pl.pallas_call(
        _moe_kernel, grid=(t // tm, N_EXPERTS),
        in_specs=[row(D_MODEL), row(D_MODEL), row(LANES),
                  pl.BlockSpec((1, D_MODEL, 2 * D_FF_EXPERT), lambda i, e: (e, 0, 0)),
                  pl.BlockSpec((1, D_FF_EXPERT, D_MODEL), lambda i, e: (e, 0, 0)),
                  pl.BlockSpec(gfin.shape, lambda i, e: (0, 0))],
        out_specs=row(D_MODEL),
        out_shape=jax.ShapeDtypeStruct((t, D_MODEL), F32),
        scratch_shapes=[pltpu.VMEM((tm, D_MODEL), F32)],
        compiler_params=_cparams("parallel", "arbitrary"), name="moe",
    )(x2, xn, cmb, w13, w2, gfin)


PAGES_PER_STEP = 16
PAGES_PER_BLOCK = MOBA_BLOCK // PAGE_SIZE
BLOCKS_PER_STEP = PAGES_PER_STEP // PAGES_PER_BLOCK


def _sscore_kernel(pt_ref, q_ref, *refs, n_past):
    del pt_ref
    pages, o_ref, ks_ref = refs[:PAGES_PER_STEP], refs[PAGES_PER_STEP], refs[PAGES_PER_STEP + 1]
    s = pl.program_id(1)
    lane = lax.broadcasted_iota(I32, (D_ATTN, LANES), 1)

    @pl.when(s == 0)
    def _():
        ks_ref[...] = jnp.zeros_like(ks_ref)

    ks = ks_ref[...]
    for r in range(BLOCKS_PER_STEP):
        acc = pages[PAGES_PER_BLOCK * r][0].reshape(D_ATTN, PAGE_SIZE)
        for t in range(1, PAGES_PER_BLOCK):
            acc = acc + pages[PAGES_PER_BLOCK * r + t][0].reshape(D_ATTN, PAGE_SIZE)
        ks = jnp.where(lane == s * BLOCKS_PER_STEP + r, jnp.sum(acc, axis=1, keepdims=True), ks)
    ks_ref[...] = ks

    @pl.when(s == pl.num_programs(1) - 1)
    def _():
        hrow = lax.broadcasted_iota(I32, (N_HEADS_A, D_ATTN), 0)
        hcol = lax.broadcasted_iota(I32, (N_HEADS_A, D_ATTN), 1) >> (HEAD_DIM_A.bit_length() - 1)
        qbd = jnp.where(hrow == hcol, jnp.broadcast_to(q_ref[0], (N_HEADS_A, D_ATTN)), 0.0)
        sc = _mm(qbd, ks, True) * (1.0 / MOBA_BLOCK)
        bl_i = lax.broadcasted_iota(I32, sc.shape, 1)
        bl = bl_i.astype(F32)
        valid = bl_i < n_past
        taken = jnp.zeros(sc.shape, jnp.bool_)
        out = jnp.zeros(sc.shape, F32)
        for r in range(MOBA_TOPK):
            sm = jnp.where(valid & jnp.logical_not(taken), sc, BIG_NEG)
            mx = jnp.max(sm, axis=1, keepdims=True)
            idx = jnp.min(jnp.where((sm == mx) & (sm > BIG_NEG), bl, NO_IDX), axis=1, keepdims=True)
            taken = taken | (bl == idx)
            out = jnp.where(bl_i == r, idx, out)
        o_ref[0] = out.astype(I32)


def _sample_scores(cache4, page_table, q3, n_past):
    nb, npages = page_table.shape

    def pspec(r):
        return pl.BlockSpec((1, N_HEADS_A, HEAD_DIM_A, PAGE_SIZE),
                            lambda b, s, pt: (pt[b, s * PAGES_PER_STEP + r], 0, 0, 0))

    return pl.pallas_call(
        functools.partial(_sscore_kernel, n_past=n_past),
        grid_spec=pltpu.PrefetchScalarGridSpec(
            num_scalar_prefetch=1, grid=(nb, npages // PAGES_PER_STEP),
            in_specs=[pl.BlockSpec((1, 1, D_ATTN), lambda b, s, pt: (b, 0, 0))]
            + [pspec(r) for r in range(PAGES_PER_STEP)],
            out_specs=pl.BlockSpec((1, N_HEADS_A, LANES), lambda b, s, pt: (b, 0, 0)),
            scratch_shapes=[pltpu.VMEM((D_ATTN, LANES), F32)]),
        out_shape=jax.ShapeDtypeStruct((nb, N_HEADS_A, LANES), I32),
        compiler_params=_cparams("parallel", "arbitrary"), name="sample_scores",
    )(page_table, q3, *([cache4] * PAGES_PER_STEP))


PAGES_PER_HEAD = MOBA_TOPK * PAGES_PER_BLOCK
N_SEL_PAGES = N_HEADS_A * PAGES_PER_HEAD


def _sattn_kernel(sel_ref, pt_ref, q_ref, kn_ref, vn_ref, *refs):
    del sel_ref, pt_ref
    kp, vp, o_ref = refs[:N_SEL_PAGES], refs[N_SEL_PAGES:2 * N_SEL_PAGES], refs[2 * N_SEL_PAGES]
    for h in range(N_HEADS_A):
        qh = q_ref[0, h:h + 1, :] * (HEAD_DIM_A ** -0.5)
        q8 = jnp.broadcast_to(qh, (8, HEAD_DIM_A))
        kts = [kp[h * PAGES_PER_HEAD + r][0, 0] for r in range(PAGES_PER_HEAD)]
        vts = [vp[h * PAGES_PER_HEAD + r][0, 0] for r in range(PAGES_PER_HEAD)]
        ss = [_mm(q8, kt, True) for kt in kts]
        s_self = jnp.sum(qh * kn_ref[0, h:h + 1, :], axis=1, keepdims=True)
        mx = s_self
        for s in ss:
            mx = jnp.maximum(mx, jnp.max(s, axis=1, keepdims=True)[:1])
        p_self = jnp.exp(s_self - mx)
        den = p_self
        acc8 = jnp.zeros((8, HEAD_DIM_A), F32)
        for s, vt in zip(ss, vts):
            p = jnp.exp(s - mx)
            den = den + jnp.sum(p, axis=1, keepdims=True)[:1]
            acc8 = acc8 + _mm(p, vt, True, nt=True)
        o_ref[0, h:h + 1, :] = (p_self * vn_ref[0, h:h + 1, :] + acc8[:1]) / den


def _sample_attn(sel_flat, pt_flat, q3, kn3, vn3, ck4, cv4, n_pages):
    nb = q3.shape[0]

    def pspec(slot):
        h, rem = divmod(slot, PAGES_PER_HEAD)
        r, half = divmod(rem, PAGES_PER_BLOCK)

        def imap(b, sel, pt):
            blk = sel[(b * N_HEADS_A + h) * MOBA_TOPK + r]
            return (pt[b * n_pages + blk * PAGES_PER_BLOCK + half], h, 0, 0)

        return pl.BlockSpec((1, 1, HEAD_DIM_A, PAGE_SIZE), imap)

    tok = pl.BlockSpec((1, N_HEADS_A, HEAD_DIM_A), lambda b, sel, pt: (b, 0, 0))
    return pl.pallas_call(
        _sattn_kernel,
        grid_spec=pltpu.PrefetchScalarGridSpec(
            num_scalar_prefetch=2, grid=(nb,),
            in_specs=[tok, tok, tok] + [pspec(s) for s in range(N_SEL_PAGES)] * 2,
            out_specs=tok),
        out_shape=jax.ShapeDtypeStruct((nb, N_HEADS_A, HEAD_DIM_A), F32),
        compiler_params=_cparams("arbitrary"), name="sample_attn",
    )(sel_flat, pt_flat, q3, kn3, vn3, *([ck4] * N_SEL_PAGES), *([cv4] * N_SEL_PAGES))


def _tail(x, attn, sga, sb, mk3, mv3, w, batch, seq, tm, tq, precise):
    x1, xq = _merge(x, attn, sga, sb, w['wpa'], w['wo'], w['gx'], w['wxq'], tm, precise)
    xq3 = xq.reshape(batch, seq, D_XATTN)
    pad = (-seq) % tq
    if pad:
        xq3 = jnp.pad(xq3, ((0, 0), (0, pad), (0, 0)))
    xo = _xattn(xq3, mk3, mv3, tq, precise)[:, :seq].reshape(batch * seq, D_XATTN)
    x2, xn, cmb = _post(x1, xo, w['wxo'], w['gf'], w['wrh'], w['wrl'], w['br'], tm, precise)
    return _moe(x2, xn, cmb, w['w13'], w['w2'], w['gfin'], tm)


def kernel(x_prompt, x_sample, mem_prompt, cache_k, cache_v, page_table, state_ssm_re, state_ssm_im,
           cache_mem_k, cache_mem_v, g_mix, w_in, ssm_lambda_re, ssm_lambda_im, ssm_log_dt,
           ssm_b_re, ssm_b_im, ssm_c_re, ssm_c_im, ssm_d, w_glu, w_pa, w_pb, w_o, g_x, g_mem,
           w_xq, w_xk, w_xv, w_xo, g_ffn, w_group, b_group, w_erouter, b_erouter, w1, w3, w2, g_final):
    depth = w_in.shape[0]
    assert depth == 1
    l = 0
    bp, sp, _ = x_prompt.shape
    bs, ss, _ = x_sample.shape
    assert ss == 1
    n_pages = page_table.shape[1]
    past_len = n_pages * PAGE_SIZE
    assert past_len % MOBA_BLOCK == 0 and sp % MOBA_BLOCK == 0

    row = lambda a: a.reshape(1, -1).astype(F32)
    bf = lambda a: a.astype(BF16)
    wr = jnp.zeros((D_MODEL, LANES), F32)
    wr = wr.at[:, :N_EXPERTS].set(w_erouter[l]).at[:, GROUP_LANE0:GROUP_LANE0 + N_EXPERT_GROUPS].set(w_group[l])
    wrh, wrl = _split_bf16(wr)
    br = jnp.zeros((1, LANES), F32)
    br = br.at[0, :N_EXPERTS].set(b_erouter[l]).at[0, GROUP_LANE0:GROUP_LANE0 + N_EXPERT_GROUPS].set(b_group[l])
    ws = dict(wpa=w_pa[l], wo=w_o[l], gx=row(g_x[l]), wxq=w_xq[l], wxo=w_xo[l], gf=row(g_ffn[l]),
              wrh=wrh, wrl=wrl, br=br, w13=bf(jnp.concatenate([w1[l], w3[l]], axis=-1)), w2=bf(w2[l]),
              gfin=row(g_final))
    wp = dict(ws, wpa=bf(w_pa[l]), wo=bf(w_o[l]), wxq=bf(w_xq[l]), wxo=bf(w_xo[l]))
    gmix = row(g_mix[l])

    ar, ai, bbt_re, bbt_im = _ssm_params(ssm_lambda_re[l], ssm_lambda_im[l], ssm_log_dt[l], ssm_b_re[l],
                                         ssm_b_im[l])
    bb, cc = _s5_weights(ar, ai, bbt_re, bbt_im, ssm_c_re[l], ssm_c_im[l])
    d_row = row(ssm_d[l])

    tp = bp * sp
    xp = x_prompt.reshape(tp, D_MODEL)
    q, k, v, kb, vb, u, sga, gb = _inproj(xp, gmix, bf(w_in[l]), 512)
    attn = _moba_prompt(q, kb, vb, bp, sp)
    h0 = jnp.zeros((bp, 2 * N_STATE), F32)
    sb3, ht = _s5(u.reshape(bp, sp, D_SSM), gb.reshape(bp, sp, D_MODEL), h0, ar, ai, bf(bb), bf(cc), d_row,
                  bf(w_glu[l]), bf(w_pb[l]), bn=bp, lc=128, precise=False)
    sr_p, si_p = _lanes_to_state(ht)
    mk_p, mv_p = _memkv(mem_prompt.reshape(bp * N_MEM, D_MODEL), row(g_mem[l]), bf(w_xk[l]), bf(w_xv[l]), 256)
    y_p = _tail(xp, attn, sga, sb3.reshape(tp, D_MODEL), mk_p.reshape(bp, N_MEM, D_XATTN),
                mv_p.reshape(bp, N_MEM, D_XATTN), wp, bp, sp, 1024, 512, False)

    xs = x_sample.reshape(bs, D_MODEL)
    proj_s = _inproj_precise(xs, gmix, w_in[l])
    q_s, k_s, v_s = (proj_s[:, D_ATTN * n:D_ATTN * (n + 1)] for n in range(3))
    u_s = proj_s[:, 3 * D_ATTN:GA_COL0]
    sga_s = proj_s[:, GA_COL0:GA_COL0 + D_MODEL]
    gb_s = proj_s[:, GA_COL0 + D_MODEL:]
    ck4 = jnp.transpose(cache_k[l], (0, 2, 3, 1))
    cv4 = jnp.transpose(cache_v[l], (0, 2, 3, 1))
    sel = _sample_scores(ck4, page_table, q_s.reshape(bs, 1, D_ATTN), past_len // MOBA_BLOCK)
    sel_flat = sel[:, :, :MOBA_TOPK].reshape(-1)
    hsplit = lambda a: a.reshape(bs, N_HEADS_A, HEAD_DIM_A)
    attn_s = _sample_attn(sel_flat, page_table.reshape(-1), hsplit(q_s), hsplit(k_s), hsplit(v_s), ck4, cv4,
                          n_pages)
    h0_s = _state_to_lanes(state_ssm_re[l].reshape(bs, N_STATE), state_ssm_im[l].reshape(bs, N_STATE))
    sb_s, ht_s = _s5(u_s.reshape(1, bs, D_SSM), gb_s.reshape(1, bs, D_MODEL), h0_s, ar, ai, bb, cc, d_row,
                     w_glu[l], w_pb[l], bn=bs, lc=1, precise=True)
    sr_s, si_s = _lanes_to_state(ht_s)
    y_s = _tail(xs, attn_s.reshape(bs, D_ATTN), sga_s, sb_s.reshape(bs, D_MODEL),
                cache_mem_k[l].reshape(bs, N_MEM, D_XATTN), cache_mem_v[l].reshape(bs, N_MEM, D_XATTN),
                ws, bs, 1, bs, 16, True)

    kv5 = lambda a, b, s: a.reshape(1, b, s, N_HEADS_A, HEAD_DIM_A)
    st4 = lambda a: a[None]
    mem5 = lambda a: a.reshape(1, bp, N_MEM, N_HEADS_X, HEAD_DIM_X)
    return (y_p.reshape(bp, sp, D_MODEL), y_s.reshape(bs, 1, D_MODEL),
            kv5(k, bp, sp), kv5(v, bp, sp), st4(sr_p), st4(si_p), mem5(mk_p), mem5(mv_p),
            kv5(k_s, bs, 1), kv5(v_s, bs, 1), st4(sr_s), st4(si_s))
```

```python
import functools

import jax
import jax.numpy as jnp
from jax import lax
from jax.experimental import pallas as pl
from jax.experimental.pallas import tpu as pltpu

F32 = jnp.float32
BF16 = jnp.bfloat16
I32 = jnp.int32

D_MODEL = 1024
N_HEADS_A = 8
HEAD_DIM_A = 64
D_ATTN = N_HEADS_A * HEAD_DIM_A
MOBA_BLOCK = 256
BLOCK_SHIFT = MOBA_BLOCK.bit_length() - 1
MOBA_TOPK = 3
D_SSM = 512
SSM_GROUP = 16
N_SSM_GROUPS = D_SSM // SSM_GROUP
SSM_STATE = 64
N_STATE = N_SSM_GROUPS * SSM_STATE
N_MEM = 256
N_HEADS_X = 4
HEAD_DIM_X = 128
D_XATTN = N_HEADS_X * HEAD_DIM_X
N_EXPERT_GROUPS = 4
EXPERTS_PER_GROUP = 4
N_EXPERTS = N_EXPERT_GROUPS * EXPERTS_PER_GROUP
D_FF_EXPERT = 256
D_IN_PROJ = 3 * D_ATTN + D_SSM + 2 * D_MODEL
RMS_EPS = 1e-6
NEG_INF = -1e30
PAGE_SIZE = 128

LANES = 128
VMEM_LIMIT = 56 * 1024 * 1024
BIG_NEG = -3e38
M_INIT = -1e29
NO_IDX = 1e9


def _cparams(*sem):
    return pltpu.CompilerParams(dimension_semantics=sem, vmem_limit_bytes=VMEM_LIMIT)


def _rms(x, g):
    return x * lax.rsqrt(jnp.mean(x * x, axis=-1, keepdims=True) + RMS_EPS) * g


def _dot(a, b):
    return jnp.dot(a, b, preferred_element_type=F32)


def _dot_nt(a, b):
    return lax.dot_general(a, b, (((1,), (1,)), ((), ())), preferred_element_type=F32)


def _split_bf16(x):
    hi = x.astype(BF16)
    return hi, (x - hi.astype(F32)).astype(BF16)


def _mm(a, w, precise, nt=False):
    dot = _dot_nt if nt else _dot
    if not precise:
        return dot(a.astype(BF16), w.astype(BF16))
    ah, al = _split_bf16(a.astype(F32))
    wh, wl = _split_bf16(w)
    return dot(ah, wh) + dot(al, wh) + dot(ah, wl)


def _inproj_kernel(x_ref, g_ref, w_ref, q_ref, k_ref, v_ref, kb_ref, vb_ref, u_ref, sga_ref, gb_ref):
    xn = _rms(x_ref[...], g_ref[...]).astype(BF16)

    def mm(c0, n):
        return _dot(xn, w_ref[:, c0:c0 + n])

    q_ref[...] = mm(0, D_ATTN)
    k = mm(D_ATTN, D_ATTN)
    k_ref[...] = k
    kb_ref[...] = k.astype(BF16)
    v = mm(2 * D_ATTN, D_ATTN)
    v_ref[...] = v
    vb_ref[...] = v.astype(BF16)
    u_ref[...] = mm(3 * D_ATTN, D_SSM)
    sga_ref[...] = jax.nn.sigmoid(mm(3 * D_ATTN + D_SSM, D_MODEL)).astype(BF16)
    gb_ref[...] = mm(3 * D_ATTN + D_SSM + D_MODEL, D_MODEL)


def _inproj(x, g, w_bf, tm):
    t = x.shape[0]
    row = lambda n: pl.BlockSpec((tm, n), lambda i: (i, 0))
    full = lambda a: pl.BlockSpec(a.shape, lambda i: (0,) * a.ndim)
    shp = lambda n, dt: jax.ShapeDtypeStruct((t, n), dt)
    return pl.pallas_call(
        _inproj_kernel,
        grid=(t // tm,),
        in_specs=[row(D_MODEL), full(g), full(w_bf)],
        out_specs=[row(D_ATTN)] * 5 + [row(D_SSM), row(D_MODEL), row(D_MODEL)],
        out_shape=[shp(D_ATTN, F32), shp(D_ATTN, F32), shp(D_ATTN, F32), shp(D_ATTN, BF16),
                   shp(D_ATTN, BF16), shp(D_SSM, F32), shp(D_MODEL, BF16), shp(D_MODEL, F32)],
        compiler_params=_cparams("parallel"),
        name="inproj",
    )(x, g, w_bf)


INPROJ_COLS = 512
GA_COL0 = 3 * D_ATTN + D_SSM


def _inproj_precise_kernel(x_ref, g_ref, w_ref, o_ref):
    c0 = pl.program_id(0) * INPROJ_COLS
    acc = _mm(_rms(x_ref[...], g_ref[...]), w_ref[...], True)
    is_ga = (c0 >= GA_COL0) & (c0 < GA_COL0 + D_MODEL)

    @pl.when(is_ga)
    def _():
        o_ref[...] = jax.nn.sigmoid(acc)

    @pl.when(jnp.logical_not(is_ga))
    def _():
        o_ref[...] = acc


def _inproj_precise(x, g, w):
    t = x.shape[0]
    full = lambda a: pl.BlockSpec(a.shape, lambda j: (0,) * a.ndim)
    return pl.pallas_call(
        _inproj_precise_kernel,
        grid=(D_IN_PROJ // INPROJ_COLS,),
        in_specs=[full(x), full(g), pl.BlockSpec((D_MODEL, INPROJ_COLS), lambda j: (0, j))],
        out_specs=pl.BlockSpec((t, INPROJ_COLS), lambda j: (0, j)),
        out_shape=jax.ShapeDtypeStruct((t, D_IN_PROJ), F32),
        compiler_params=_cparams("parallel"),
        name="inproj_sample",
    )(x, g, w)


def _ssm_param_kernel(lr_ref, li_ref, dt_ref, br_ref, bi_ref, ar_ref, ai_ref, bbr_ref, bbi_ref):
    lr = jnp.minimum(lr_ref[...], -1e-4)
    li = li_ref[...]
    dt = jnp.exp(dt_ref[...])
    mag = jnp.exp(lr * dt)
    ar = mag * jnp.cos(li * dt)
    ai = mag * jnp.sin(li * dt)
    den = lr * lr + li * li
    nr = ar - 1.0
    cr = (nr * lr + ai * li) / den
    ci = (ai * lr - nr * li) / den
    ar_ref[...] = ar
    ai_ref[...] = ai
    br = br_ref[...]
    bi = bi_ref[...]
    bbr_ref[...] = cr * br - ci * bi
    bbi_ref[...] = cr * bi + ci * br


def _ssm_params(lam_re, lam_im, log_dt, b_re, b_im):
    n = N_STATE
    lr = lam_re.reshape(1, n)
    li = lam_im.reshape(1, n)
    dt = jnp.broadcast_to(log_dt[:, None], (N_SSM_GROUPS, SSM_STATE)).reshape(1, n)
    brt = b_re.reshape(n, SSM_GROUP).T
    bit = b_im.reshape(n, SSM_GROUP).T
    row = jax.ShapeDtypeStruct((1, n), F32)
    mat = jax.ShapeDtypeStruct((SSM_GROUP, n), F32)
    return pl.pallas_call(_ssm_param_kernel, out_shape=[row, row, mat, mat], name="ssm_params")(
        lr, li, dt, brt, bit)


HALF_STATE = N_STATE // 2
SCAN_COLS = 512
SCAN_TILES = SCAN_COLS // 128


def _s5_kernel(u_ref, gb_ref, h0_ref, ar_ref, ai_ref, bb_ref, cc_ref, d_ref, wglu_ref, wpb_ref,
               sb_ref, ht_ref, s_ref, carry_ref, *, bn, lc, precise):
    m = bn * lc

    @pl.when(pl.program_id(0) == 0)
    def _():
        carry_ref[...] = h0_ref[...]

    u = u_ref[...].reshape(m, D_SSM)
    for j in range(16):
        sec, jj = divmod(j, 4)
        gbase = (sec // 2) * 16 + jj * 4
        lt = (gbase * SSM_GROUP) // LANES
        bu = _mm(u[:, LANES * lt:LANES * (lt + 1)], bb_ref[j], precise)
        s_ref[2 * j] = bu[:, :LANES]
        s_ref[2 * j + 1] = bu[:, LANES:]

    tiles_half = HALF_STATE // LANES
    for h in range(2):
        for c in range(HALF_STATE // SCAN_COLS):
            re_t = [2 * tiles_half * h + SCAN_TILES * c + n for n in range(SCAN_TILES)]
            im_t = [t + tiles_half for t in re_t]
            a_t = [tiles_half * h + SCAN_TILES * c + n for n in range(SCAN_TILES)]
            ars = [jnp.broadcast_to(ar_ref[:, LANES * t:LANES * (t + 1)], (bn, LANES)) for t in a_t]
            ais = [jnp.broadcast_to(ai_ref[:, LANES * t:LANES * (t + 1)], (bn, LANES)) for t in a_t]

            def body(t, carry, re_t=re_t, im_t=im_t, ars=ars, ais=ais):
                rows = pl.ds(t, bn, stride=lc) if lc > 1 else pl.ds(0, bn)
                out = []
                for n in range(SCAN_TILES):
                    xr, xi = carry[2 * n], carry[2 * n + 1]
                    nxr = ars[n] * xr - ais[n] * xi + s_ref[re_t[n], rows, :]
                    nxi = ars[n] * xi + ais[n] * xr + s_ref[im_t[n], rows, :]
                    s_ref[re_t[n], rows, :] = nxr
                    s_ref[im_t[n], rows, :] = nxi
                    out += [nxr, nxi]
                return tuple(out)

            x0 = []
            for n in range(SCAN_TILES):
                x0 += [carry_ref[:, LANES * re_t[n]:LANES * (re_t[n] + 1)],
                       carry_ref[:, LANES * im_t[n]:LANES * (im_t[n] + 1)]]
            xs = lax.fori_loop(0, lc, body, tuple(x0), unroll=min(lc, 8))
            for n in range(SCAN_TILES):
                carry_ref[:, LANES * re_t[n]:LANES * (re_t[n] + 1)] = xs[2 * n]
                carry_ref[:, LANES * im_t[n]:LANES * (im_t[n] + 1)] = xs[2 * n + 1]
    ht_ref[...] = carry_ref[...]

    ys = []
    for h in range(2):
        sdt = F32 if precise else BF16
        xh = jnp.concatenate([s_ref[2 * tiles_half * h + n].astype(sdt) for n in range(2 * tiles_half)], axis=1)
        ys.append(_mm(xh, cc_ref[h], precise))
    y = jnp.concatenate(ys, axis=1) + d_ref[...] * u
    z = jax.nn.gelu(y)
    s5 = z * jax.nn.sigmoid(_mm(z, wglu_ref[...], precise))
    pb = _mm(s5, wpb_ref[...], precise)
    gb = gb_ref[...].reshape(m, D_MODEL)
    sb_ref[...] = (jax.nn.sigmoid(gb) * pb).astype(sb_ref.dtype).reshape(sb_ref.shape)


def _s5(u3, gb3, h0, ar, ai, bb, cc, d, wglu, wpb, bn, lc, precise):
    nb, s, _ = u3.shape
    rows = bn * lc // nb
    nchunk = s // rows
    full = lambda a: pl.BlockSpec(a.shape, lambda c: (0,) * a.ndim)
    blk = lambda n: pl.BlockSpec((nb, rows, n), lambda c: (0, c, 0))
    return pl.pallas_call(
        functools.partial(_s5_kernel, bn=bn, lc=lc, precise=precise),
        grid=(nchunk,),
        in_specs=[blk(D_SSM), blk(D_MODEL), full(h0), full(ar), full(ai), full(bb), full(cc), full(d),
                  full(wglu), full(wpb)],
        out_specs=[blk(D_MODEL), full(h0)],
        out_shape=[jax.ShapeDtypeStruct((nb, s, D_MODEL), F32 if precise else BF16),
                   jax.ShapeDtypeStruct(h0.shape, F32)],
        scratch_shapes=[pltpu.VMEM((2 * N_STATE // LANES, bn * lc, LANES), F32),
                        pltpu.VMEM((bn, 2 * N_STATE), F32)],
        compiler_params=_cparams("arbitrary"),
        name="s5",
    )(u3, gb3, h0, ar, ai, bb, cc, d, wglu, wpb)


def _s5_weights(ar, ai, bbt_re, bbt_im, c_re, c_im):
    g, p, h = N_SSM_GROUPS, SSM_STATE, SSM_GROUP
    eye = jnp.eye(g, dtype=F32)

    def bfull(bt):
        b = bt.reshape(h, g, p)
        return jnp.einsum('hgp,gk->ghkp', b, eye).reshape(g * h, g * p)

    bre, bim = bfull(bbt_re), bfull(bbt_im)
    tiles = []
    for j in range(16):
        sec, jj = divmod(j, 4)
        src = bre if sec % 2 == 0 else bim
        gbase = (sec // 2) * 16 + jj * 4
        lt = (gbase * h) // LANES
        tiles.append(src[LANES * lt:LANES * (lt + 1), gbase * p:(gbase + 4) * p])
    bb = jnp.stack(tiles)

    def cfull(c):
        return jnp.einsum('ghp,gk->gpkh', c, eye).reshape(g * p, g * h)

    cre, cim = cfull(c_re), cfull(c_im)
    halves = []
    for hh in range(2):
        rs = slice(HALF_STATE * hh, HALF_STATE * (hh + 1))
        cs = slice(256 * hh, 256 * (hh + 1))
        halves.append(jnp.concatenate([cre[rs, cs], -cim[rs, cs]], axis=0))
    return bb, jnp.stack(halves)


def _state_to_lanes(re, im):
    b = re.shape[0]
    r = re.reshape(b, 2, HALF_STATE)
    i = im.reshape(b, 2, HALF_STATE)
    return jnp.concatenate([r[:, 0], i[:, 0], r[:, 1], i[:, 1]], axis=1)


def _lanes_to_state(h):
    b = h.shape[0]
    h4 = h.reshape(b, 4, HALF_STATE)
    re = jnp.concatenate([h4[:, 0], h4[:, 2]], axis=1).reshape(b, N_SSM_GROUPS, SSM_STATE)
    im = jnp.concatenate([h4[:, 1], h4[:, 3]], axis=1).reshape(b, N_SSM_GROUPS, SSM_STATE)
    return re, im


MOBA_CHUNK = 4


def _moba_kernel(q_ref, k_ref, v_ref, o_ref, ka0_ref, ka1_ref, selb_ref, *, seq):
    blk, hd = MOBA_BLOCK, HEAD_DIM_A
    nblk = seq // blk
    chunk_rows = MOBA_CHUNK * blk

    k = k_ref[...]
    rblk = lax.broadcasted_iota(I32, (seq, LANES), 0) >> BLOCK_SHIFT
    lane = lax.broadcasted_iota(I32, (seq, LANES), 1)
    kf = k.astype(F32)
    ka0_ref[...] = jnp.where(lane < hd, kf, jnp.where(lane - hd == rblk, 1.0, 0.0)).astype(BF16)
    ka1_ref[...] = jnp.where(lane >= hd, kf, jnp.where(lane == rblk, 1.0, 0.0)).astype(BF16)
    r = lax.broadcasted_iota(I32, (LANES, seq), 0)
    cblk = lax.broadcasted_iota(I32, (LANES, seq), 1) >> BLOCK_SHIFT
    ind = jnp.where((r == cblk) | (r - hd == cblk), 1.0, 0.0).astype(BF16)
    kmean = _dot(ind, k) * (1.0 / blk)
    rr = lax.broadcasted_iota(I32, (LANES, LANES), 0)
    ll = lax.broadcasted_iota(I32, (LANES, LANES), 1)
    keep = ((rr < hd) & (ll >= hd)) | ((rr >= hd) & (ll < hd))
    ahi, alo = _split_bf16(jnp.where(keep, kmean, 0.0))

    qhi, qlo = _split_bf16(q_ref[...])
    sc = _dot_nt(qhi, ahi) + _dot_nt(qlo, ahi) + _dot_nt(qhi, alo)
    sct = sc.T
    blk_i = lax.broadcasted_iota(I32, (nblk, seq), 0)
    blk_f = blk_i.astype(F32)
    own = lax.broadcasted_iota(I32, (nblk, seq), 1) >> BLOCK_SHIFT
    valid = blk_i < own
    biases = []
    for x in (sct[0:nblk], sct[hd:hd + nblk]):
        taken = jnp.zeros((nblk, seq), jnp.bool_)
        for _ in range(MOBA_TOPK):
            sm = jnp.where(valid & jnp.logical_not(taken), x, BIG_NEG)
            mx = jnp.max(sm, axis=0, keepdims=True)
            idx = jnp.min(jnp.where((sm == mx) & (sm > BIG_NEG), blk_f, NO_IDX), axis=0, keepdims=True)
            taken = taken | (blk_f == idx)
        biases.append(jnp.where(taken | (blk_i == own), 0.0, NEG_INF))
    pad = jnp.zeros((hd - nblk, seq), F32)
    selb_ref[...] = jnp.concatenate([biases[0], pad, biases[1], pad], axis=0).T.astype(BF16)

    lane_b = lax.broadcasted_iota(I32, (blk, LANES), 1)
    col_minus_row = (lax.broadcasted_iota(I32, (blk, chunk_rows), 1)
                     - lax.broadcasted_iota(I32, (blk, chunk_rows), 0))
    heads = ((ka0_ref, lane_b < hd), (ka1_ref, lane_b >= hd))
    chunk_shift = MOBA_CHUNK.bit_length() - 1

    def qblock(i, _):
        off = pl.multiple_of(i * blk, blk)
        qs = q_ref[pl.ds(off, blk), :] * (hd ** -0.5)
        sb = selb_ref[pl.ds(off, blk), :].astype(F32)
        qas = [jnp.where(mine, qs, sb).astype(BF16) for _, mine in heads]

        def scores(c, qa, ka_ref):
            coff = pl.multiple_of(c * chunk_rows, chunk_rows)
            return _dot_nt(qa, ka_ref[pl.ds(coff, chunk_rows), :])

        def values(c):
            return v_ref[pl.ds(pl.multiple_of(c * chunk_rows, chunk_rows), chunk_rows), :]

        dc = i >> chunk_shift
        thr = (i - (dc << chunk_shift)) * blk
        vc = values(dc)
        init = []
        for (ka_ref, _), qa in zip(heads, qas):
            s = jnp.where(col_minus_row <= thr, scores(dc, qa, ka_ref), NEG_INF)
            m = jnp.max(s, axis=1, keepdims=True)
            p = jnp.exp(s - m)
            init.append((m, jnp.sum(p, axis=1, keepdims=True), _dot(p.astype(BF16), vc)))

        def chunk(c, carry):
            vc = values(c)
            out = []
            for (ka_ref, _), qa, (m, l, acc) in zip(heads, qas, carry):
                s = scores(c, qa, ka_ref)
                m_new = jnp.maximum(m, jnp.max(s, axis=1, keepdims=True))
                alpha = jnp.exp(m - m_new)
                p = jnp.exp(s - m_new)
                out.append((m_new, alpha * l + jnp.sum(p, axis=1, keepdims=True),
                            alpha * acc + _dot(p.astype(BF16), vc)))
            return tuple(out)

        (_, l0, a0), (_, l1, a1) = lax.fori_loop(0, dc, chunk, tuple(init))
        o_ref[pl.ds(off, blk), :] = jnp.where(lane_b < hd, a0 / l0, a1 / l1).astype(BF16)
        return 0

    lax.fori_loop(0, nblk, qblock, 0)


def _moba_prompt(q, kb, vb, batch, seq):
    spec = pl.BlockSpec((seq, LANES), lambda b, hp: (b, hp))
    return pl.pallas_call(
        functools.partial(_moba_kernel, seq=seq),
        grid=(batch, D_ATTN // LANES),
        in_specs=[spec, spec, spec],
        out_specs=spec,
        out_shape=jax.ShapeDtypeStruct((batch * seq, D_ATTN), BF16),
        scratch_shapes=[pltpu.VMEM((seq, LANES), BF16)] * 3,
        compiler_params=_cparams("parallel", "parallel"),
        name="moba_prompt",
    )(q, kb, vb)


def _memkv_kernel(m_ref, g_ref, wk_ref, wv_ref, mk_ref, mv_ref):
    mn = _rms(m_ref[...], g_ref[...]).astype(BF16)
    mk_ref[...] = _dot(mn, wk_ref[...])
    mv_ref[...] = _dot(mn, wv_ref[...])


def _memkv(mem, g, wk, wv, tm):
    t = mem.shape[0]
    row = lambda n: pl.BlockSpec((tm, n), lambda i: (i, 0))
    full = lambda a: pl.BlockSpec(a.shape, lambda i: (0,) * a.ndim)
    shp = jax.ShapeDtypeStruct((t, D_XATTN), F32)
    return pl.pallas_call(
        _memkv_kernel, grid=(t // tm,),
        in_specs=[row(D_MODEL), full(g), full(wk), full(wv)],
        out_specs=[row(D_XATTN)] * 2, out_shape=[shp, shp],
        compiler_params=_cparams("parallel"), name="memkv",
    )(mem, g, wk, wv)


def _merge_kernel(x_ref, at_ref, sga_ref, sb_ref, wpa_ref, wo_ref, gx_ref, wxq_ref, x1_ref, xq_ref, *, precise):
    pa = _mm(at_ref[...], wpa_ref[...], precise)
    merged = sga_ref[...].astype(F32) * pa + sb_ref[...].astype(F32)
    x1 = x_ref[...] + _mm(merged, wo_ref[...], precise)
    x1_ref[...] = x1
    xq_ref[...] = _mm(_rms(x1, gx_ref[...]), wxq_ref[...], precise).astype(xq_ref.dtype)


def _merge(x, attn, sga, sb, wpa, wo, gx, wxq, tm, precise):
    t = x.shape[0]
    row = lambda n: pl.BlockSpec((tm, n), lambda i: (i, 0))
    full = lambda a: pl.BlockSpec(a.shape, lambda i: (0,) * a.ndim)
    return pl.pallas_call(
        functools.partial(_merge_kernel, precise=precise), grid=(t // tm,),
        in_specs=[row(D_MODEL), row(D_ATTN), row(D_MODEL), row(D_MODEL), full(wpa), full(wo), full(gx),
                  full(wxq)],
        out_specs=[row(D_MODEL), row(D_XATTN)],
        out_shape=[jax.ShapeDtypeStruct((t, D_MODEL), F32),
                   jax.ShapeDtypeStruct((t, D_XATTN), F32 if precise else BF16)],
        compiler_params=_cparams("parallel"), name="merge",
    )(x, attn, sga, sb, wpa, wo, gx, wxq)


def _xattn_kernel(q_ref, mk_ref, mv_ref, o_ref, *, precise):
    q = q_ref[0]
    mk = mk_ref[0]
    mv = mv_ref[0]
    outs = []
    for h in range(N_HEADS_X):
        cs = slice(HEAD_DIM_X * h, HEAD_DIM_X * (h + 1))
        s = _mm(q[:, cs], mk[:, cs], precise, nt=True) * (HEAD_DIM_X ** -0.5)
        e = jnp.exp(s - jnp.max(s, axis=1, keepdims=True))
        p = e / jnp.sum(e, axis=1, keepdims=True)
        outs.append(_mm(p, mv[:, cs], precise))
    o_ref[0] = jnp.concatenate(outs, axis=1).astype(o_ref.dtype)


def _xattn(q3, mk3, mv3, tq, precise):
    b, s, _ = q3.shape
    qspec = pl.BlockSpec((1, tq, D_XATTN), lambda bi, si: (bi, si, 0))
    mspec = pl.BlockSpec((1, N_MEM, D_XATTN), lambda bi, si: (bi, 0, 0))
    return pl.pallas_call(
        functools.partial(_xattn_kernel, precise=precise), grid=(b, s // tq),
        in_specs=[qspec, mspec, mspec], out_specs=qspec,
        out_shape=jax.ShapeDtypeStruct(q3.shape, q3.dtype),
        compiler_params=_cparams("parallel", "parallel"), name="xattn",
    )(q3, mk3, mv3)


GROUP_LANE0 = N_EXPERTS


def _post_kernel(x1_ref, xo_ref, wxo_ref, gf_ref, wrh_ref, wrl_ref, br_ref, x2_ref, xn_ref, cmb_ref, *, precise):
    x2 = x1_ref[...] + _mm(xo_ref[...], wxo_ref[...], precise)
    x2_ref[...] = x2
    t = _rms(x2, gf_ref[...])
    thi, tlo = _split_bf16(t)
    xn_ref[...] = thi
    logits = _dot(thi, wrh_ref[...]) + _dot(tlo, wrh_ref[...]) + _dot(thi, wrl_ref[...]) + br_ref[...]

    lane_i = lax.broadcasted_iota(I32, logits.shape, 1)
    lane = lane_i.astype(F32)
    lane_group = (lane_i >> (EXPERTS_PER_GROUP.bit_length() - 1)).astype(F32)
    isg = (lane_i >= GROUP_LANE0) & (lane_i < GROUP_LANE0 + N_EXPERT_GROUPS)
    gmax = jnp.max(jnp.where(isg, logits, BIG_NEG), axis=1, keepdims=True)
    eg = jnp.where(isg, jnp.exp(jnp.where(isg, logits, gmax) - gmax), 0.0)
    gp = eg / jnp.sum(eg, axis=1, keepdims=True)
    pg = jnp.max(jnp.where(isg, gp, -1.0), axis=1, keepdims=True)
    gi = jnp.min(jnp.where(isg & (gp == pg), lane, NO_IDX), axis=1, keepdims=True) - GROUP_LANE0

    insel = (lane_i < N_EXPERTS) & (lane_group == gi)
    el = jnp.where(insel, logits, BIG_NEG)
    m1 = jnp.max(el, axis=1, keepdims=True)
    i1 = jnp.min(jnp.where(insel & (el == m1), lane, NO_IDX), axis=1, keepdims=True)
    rest = insel & (lane != i1)
    el2 = jnp.where(rest, logits, BIG_NEG)
    m2 = jnp.max(el2, axis=1, keepdims=True)
    i2 = jnp.min(jnp.where(rest & (el2 == m2), lane, NO_IDX), axis=1, keepdims=True)
    e2 = jnp.exp(m2 - m1)
    den = 1.0 + e2
    cmb_ref[...] = jnp.where(lane == i1, (1.0 / den) * pg, jnp.where(lane == i2, (e2 / den) * pg, 0.0))


def _post(x1, xo, wxo, gf, wrh, wrl, br, tm, precise):
    t = x1.shape[0]
    row = lambda n: pl.BlockSpec((tm, n), lambda i: (i, 0))
    full = lambda a: pl.BlockSpec(a.shape, lambda i: (0,) * a.ndim)
    return pl.pallas_call(
        functools.partial(_post_kernel, precise=precise), grid=(t // tm,),
        in_specs=[row(D_MODEL), row(D_XATTN), full(wxo), full(gf), full(wrh), full(wrl), full(br)],
        out_specs=[row(D_MODEL), row(D_MODEL), row(LANES)],
        out_shape=[jax.ShapeDtypeStruct((t, D_MODEL), F32), jax.ShapeDtypeStruct((t, D_MODEL), BF16),
                   jax.ShapeDtypeStruct((t, LANES), F32)],
        compiler_params=_cparams("parallel"), name="post",
    )(x1, xo, wxo, gf, wrh, wrl, br)


PAGES_PER_STEP = 16
PAGES_PER_BLOCK = MOBA_BLOCK // PAGE_SIZE
BLOCKS_PER_STEP = PAGES_PER_STEP // PAGES_PER_BLOCK


def _block_sum_step(s, pages, ks_ref):
    lane = lax.broadcasted_iota(I32, (D_ATTN, LANES), 1)

    @pl.when(s == 0)
    def _():
        ks_ref[0] = jnp.zeros((D_ATTN, LANES), F32)

    ks = ks_ref[0]
    for r in range(BLOCKS_PER_STEP):
        acc = pages[PAGES_PER_BLOCK * r][0].reshape(D_ATTN, PAGE_SIZE)
        for t in range(1, PAGES_PER_BLOCK):
            acc = acc + pages[PAGES_PER_BLOCK * r + t][0].reshape(D_ATTN, PAGE_SIZE)
        ks = jnp.where(lane == s * BLOCKS_PER_STEP + r, jnp.sum(acc, axis=1, keepdims=True), ks)
    ks_ref[0] = ks


def _moe_stream_kernel(pt_ref, x2_ref, xn_ref, cmb_ref, w13_ref, w2_ref, gfin_ref, *refs, steps_per_seq):
    del pt_ref
    pages, (y_ref, ks_ref, acc_ref) = refs[:PAGES_PER_STEP], refs[PAGES_PER_STEP:]
    _moe_kernel(x2_ref, xn_ref, cmb_ref, w13_ref, w2_ref, gfin_ref, y_ref, acc_ref)
    g = pl.program_id(0) * N_EXPERTS + pl.program_id(1)
    _block_sum_step(lax.rem(g, steps_per_seq), pages, ks_ref)


def _moe_kernel(x2_ref, xn_ref, cmb_ref, w13_ref, w2_ref, gfin_ref, y_ref, acc_ref):
    e = pl.program_id(1)

    @pl.when(e == 0)
    def _():
        acc_ref[...] = jnp.zeros_like(acc_ref)

    h = _dot(xn_ref[...], w13_ref[0])
    cmb = cmb_ref[...]
    lane = lax.broadcasted_iota(I32, cmb.shape, 1)
    cw = jnp.sum(jnp.where(lane == e, cmb, 0.0), axis=1, keepdims=True)
    hd = jax.nn.silu(h[:, :D_FF_EXPERT]) * h[:, D_FF_EXPERT:] * cw
    acc_ref[...] += _dot(hd.astype(BF16), w2_ref[0])

    @pl.when(e == N_EXPERTS - 1)
    def _():
        y_ref[...] = _rms(x2_ref[...] + acc_ref[...], gfin_ref[...])


def _moe(x2, xn, cmb, w13, w2, gfin, tm):
    t = x2.shape[0]
    row = lambda n: pl.BlockSpec((tm, n), lambda i, e: (i, 0))
    return pl.pallas_call(
        _moe_kernel, grid=(t // tm, N_EXPERTS),
        in_specs=[row(D_MODEL), row(D_MODEL), row(LANES),
                  pl.BlockSpec((1, D_MODEL, 2 * D_FF_EXPERT), lambda i, e: (e, 0, 0)),
                  pl.BlockSpec((1, D_FF_EXPERT, D_MODEL), lambda i, e: (e, 0, 0)),
                  pl.BlockSpec(gfin.shape, lambda i, e: (0, 0))],
        out_specs=row(D_MODEL),
        out_shape=jax.ShapeDtypeStruct((t, D_MODEL), F32),
        scratch_shapes=[pltpu.VMEM((tm, D_MODEL), F32)],
        compiler_params=_cparams("parallel", "arbitrary"), name="moe",
    )(x2, xn, cmb, w13, w2, gfin)


def _moe_with_key_sums(x2, xn, cmb, w13, w2, gfin, tm, cache4, page_table):
    t = x2.shape[0]
    nb, npages = page_table.shape
    steps_per_seq = npages // PAGES_PER_STEP
    n_tiles = t // tm
    assert n_tiles * N_EXPERTS == nb * steps_per_seq and npages // PAGES_PER_BLOCK <= LANES

    def seq_of(i, e):
        return (i * N_EXPERTS + e) // steps_per_seq

    def pspec(r):
        def imap(i, e, pt):
            g = i * N_EXPERTS + e
            return (pt[g // steps_per_seq, lax.rem(g, steps_per_seq) * PAGES_PER_STEP + r], 0, 0, 0)
        return pl.BlockSpec((1, N_HEADS_A, HEAD_DIM_A, PAGE_SIZE), imap)

    row = lambda n: pl.BlockSpec((tm, n), lambda i, e, pt: (i, 0))
    return pl.pallas_call(
        functools.partial(_moe_stream_kernel, steps_per_seq=steps_per_seq),
        grid_spec=pltpu.PrefetchScalarGridSpec(
            num_scalar_prefetch=1, grid=(n_tiles, N_EXPERTS),
            in_specs=[row(D_MODEL), row(D_MODEL), row(LANES),
                      pl.BlockSpec((1, D_MODEL, 2 * D_FF_EXPERT), lambda i, e, pt: (e, 0, 0)),
                      pl.BlockSpec((1, D_FF_EXPERT, D_MODEL), lambda i, e, pt: (e, 0, 0)),
                      pl.BlockSpec(gfin.shape, lambda i, e, pt: (0, 0))]
            + [pspec(r) for r in range(PAGES_PER_STEP)],
            out_specs=[row(D_MODEL),
                       pl.BlockSpec((1, D_ATTN, LANES), lambda i, e, pt: (seq_of(i, e), 0, 0))],
            scratch_shapes=[pltpu.VMEM((tm, D_MODEL), F32)]),
        out_shape=[jax.ShapeDtypeStruct((t, D_MODEL), F32), jax.ShapeDtypeStruct((nb, D_ATTN, LANES), F32)],
        compiler_params=_cparams("arbitrary", "arbitrary"), name="moe_keysums",
    )(page_table, x2, xn, cmb, w13, w2, gfin, *([cache4] * PAGES_PER_STEP))


def _ssel_kernel(q_ref, ks_ref, o_ref, *, n_past):
    hrow = lax.broadcasted_iota(I32, (N_HEADS_A, D_ATTN), 0)
    hcol = lax.broadcasted_iota(I32, (N_HEADS_A, D_ATTN), 1) >> (HEAD_DIM_A.bit_length() - 1)
    qbd = jnp.where(hrow == hcol, jnp.broadcast_to(q_ref[0], (N_HEADS_A, D_ATTN)), 0.0)
    sc = _mm(qbd, ks_ref[0], True) * (1.0 / MOBA_BLOCK)
    bl_i = lax.broadcasted_iota(I32, sc.shape, 1)
    bl = bl_i.astype(F32)
    valid = bl_i < n_past
    taken = jnp.zeros(sc.shape, jnp.bool_)
    out = jnp.zeros(sc.shape, F32)
    for r in range(MOBA_TOPK):
        sm = jnp.where(valid & jnp.logical_not(taken), sc, BIG_NEG)
        mx = jnp.max(sm, axis=1, keepdims=True)
        idx = jnp.min(jnp.where((sm == mx) & (sm > BIG_NEG), bl, NO_IDX), axis=1, keepdims=True)
        taken = taken | (bl == idx)
        out = jnp.where(bl_i == r, idx, out)
    o_ref[0] = out.astype(I32)


def _sample_select(q3, ksum, n_past):
    nb = q3.shape[0]
    return pl.pallas_call(
        functools.partial(_ssel_kernel, n_past=n_past), grid=(nb,),
        in_specs=[pl.BlockSpec((1, 1, D_ATTN), lambda b: (b, 0, 0)),
                  pl.BlockSpec((1, D_ATTN, LANES), lambda b: (b, 0, 0))],
        out_specs=pl.BlockSpec((1, N_HEADS_A, LANES), lambda b: (b, 0, 0)),
        out_shape=jax.ShapeDtypeStruct((nb, N_HEADS_A, LANES), I32),
        compiler_params=_cparams("parallel"), name="sample_select",
    )(q3, ksum)


PAGES_PER_HEAD = MOBA_TOPK * PAGES_PER_BLOCK
N_SEL_PAGES = N_HEADS_A * PAGES_PER_HEAD


def _sattn_kernel(sel_ref, pt_ref, q_ref, kn_ref, vn_ref, *refs):
    del sel_ref, pt_ref
    kp, vp, o_ref = refs[:N_SEL_PAGES], refs[N_SEL_PAGES:2 * N_SEL_PAGES], refs[2 * N_SEL_PAGES]
    for h in range(N_HEADS_A):
        qc = q_ref[0][:, h:h + 1] * (HEAD_DIM_A ** -0.5)
        kts = [kp[h * PAGES_PER_HEAD + r][0, 0] for r in range(PAGES_PER_HEAD)]
        vts = [vp[h * PAGES_PER_HEAD + r][0, 0] for r in range(PAGES_PER_HEAD)]
        ss = [jnp.sum(kt * qc, axis=0, keepdims=True) for kt in kts]
        s_self = jnp.sum(qc * kn_ref[0][:, h:h + 1], axis=0, keepdims=True)
        mx = s_self
        for s in ss:
            mx = jnp.maximum(mx, jnp.max(s, axis=1, keepdims=True))
        p_self = jnp.exp(s_self - mx)
        den = p_self
        acc = jnp.zeros((HEAD_DIM_A, PAGE_SIZE), F32)
        for s, vt in zip(ss, vts):
            p = jnp.exp(s - mx)
            den = den + jnp.sum(p, axis=1, keepdims=True)
            acc = acc + vt * p
        out = p_self * vn_ref[0][:, h:h + 1] + jnp.sum(acc, axis=1, keepdims=True)
        o_ref[0, :, h:h + 1] = out / den


def _sample_attn(sel_flat, pt_flat, q3, kn3, vn3, ck4, cv4, n_pages):
    nb = q3.shape[0]

    def pspec(slot):
        h, rem = divmod(slot, PAGES_PER_HEAD)
        r, half = divmod(rem, PAGES_PER_BLOCK)

        def imap(b, sel, pt):
            blk = sel[(b * N_HEADS_A + h) * MOBA_TOPK + r]
            return (pt[b * n_pages + blk * PAGES_PER_BLOCK + half], h, 0, 0)

        return pl.BlockSpec((1, 1, HEAD_DIM_A, PAGE_SIZE), imap)

    tok = pl.BlockSpec((1, HEAD_DIM_A, N_HEADS_A), lambda b, sel, pt: (b, 0, 0))
    return pl.pallas_call(
        _sattn_kernel,
        grid_spec=pltpu.PrefetchScalarGridSpec(
            num_scalar_prefetch=2, grid=(nb,),
            in_specs=[tok, tok, tok] + [pspec(s) for s in range(N_SEL_PAGES)] * 2,
            out_specs=tok),
        out_shape=jax.ShapeDtypeStruct((nb, HEAD_DIM_A, N_HEADS_A), F32),
        compiler_params=_cparams("arbitrary"), name="sample_attn",
    )(sel_flat, pt_flat, q3, kn3, vn3, *([ck4] * N_SEL_PAGES), *([cv4] * N_SEL_PAGES))


def _tail(x, attn, sga, sb, mk3, mv3, w, batch, seq, tm, tq, precise, key_stream=None):
    x1, xq = _merge(x, attn, sga, sb, w['wpa'], w['wo'], w['gx'], w['wxq'], tm, precise)
    xq3 = xq.reshape(batch, seq, D_XATTN)
    pad = (-seq) % tq
    if pad:
        xq3 = jnp.pad(xq3, ((0, 0), (0, pad), (0, 0)))
    xo = _xattn(xq3, mk3, mv3, tq, precise)[:, :seq].reshape(batch * seq, D_XATTN)
    x2, xn, cmb = _post(x1, xo, w['wxo'], w['gf'], w['wrh'], w['wrl'], w['br'], tm, precise)
    if key_stream is None:
        return _moe(x2, xn, cmb, w['w13'], w['w2'], w['gfin'], tm)
    return _moe_with_key_sums(x2, xn, cmb, w['w13'], w['w2'], w['gfin'], tm, *key_stream)


def kernel(x_prompt, x_sample, mem_prompt, cache_k, cache_v, page_table, state_ssm_re, state_ssm_im,
           cache_mem_k, cache_mem_v, g_mix, w_in, ssm_lambda_re, ssm_lambda_im, ssm_log_dt,
           ssm_b_re, ssm_b_im, ssm_c_re, ssm_c_im, ssm_d, w_glu, w_pa, w_pb, w_o, g_x, g_mem,
           w_xq, w_xk, w_xv, w_xo, g_ffn, w_group, b_group, w_erouter, b_erouter, w1, w3, w2, g_final):
    depth = w_in.shape[0]
    assert depth == 1
    l = 0
    bp, sp, _ = x_prompt.shape
    bs, ss, _ = x_sample.shape
    assert ss == 1
    n_pages = page_table.shape[1]
    past_len = n_pages * PAGE_SIZE
    assert past_len % MOBA_BLOCK == 0 and sp % MOBA_BLOCK == 0

    row = lambda a: a.reshape(1, -1).astype(F32)
    bf = lambda a: a.astype(BF16)
    wr = jnp.zeros((D_MODEL, LANES), F32)
    wr = wr.at[:, :N_EXPERTS].set(w_erouter[l]).at[:, GROUP_LANE0:GROUP_LANE0 + N_EXPERT_GROUPS].set(w_group[l])
    wrh, wrl = _split_bf16(wr)
    br = jnp.zeros((1, LANES), F32)
    br = br.at[0, :N_EXPERTS].set(b_erouter[l]).at[0, GROUP_LANE0:GROUP_LANE0 + N_EXPERT_GROUPS].set(b_group[l])
    ws = dict(wpa=w_pa[l], wo=w_o[l], gx=row(g_x[l]), wxq=w_xq[l], wxo=w_xo[l], gf=row(g_ffn[l]),
              wrh=wrh, wrl=wrl, br=br, w13=bf(jnp.concatenate([w1[l], w3[l]], axis=-1)), w2=bf(w2[l]),
              gfin=row(g_final))
    wp = dict(ws, wpa=bf(w_pa[l]), wo=bf(w_o[l]), wxq=bf(w_xq[l]), wxo=bf(w_xo[l]))
    gmix = row(g_mix[l])

    ar, ai, bbt_re, bbt_im = _ssm_params(ssm_lambda_re[l], ssm_lambda_im[l], ssm_log_dt[l], ssm_b_re[l],
                                         ssm_b_im[l])
    bb, cc = _s5_weights(ar, ai, bbt_re, bbt_im, ssm_c_re[l], ssm_c_im[l])
    d_row = row(ssm_d[l])

    tp = bp * sp
    xp = x_prompt.reshape(tp, D_MODEL)
    q, k, v, kb, vb, u, sga, gb = _inproj(xp, gmix, bf(w_in[l]), 512)
    attn = _moba_prompt(q, kb, vb, bp, sp)
    h0 = jnp.zeros((bp, 2 * N_STATE), F32)
    sb3, ht = _s5(u.reshape(bp, sp, D_SSM), gb.reshape(bp, sp, D_MODEL), h0, ar, ai, bf(bb), bf(cc), d_row,
                  bf(w_glu[l]), bf(w_pb[l]), bn=bp, lc=128, precise=False)
    sr_p, si_p = _lanes_to_state(ht)
    mk_p, mv_p = _memkv(mem_prompt.reshape(bp * N_MEM, D_MODEL), row(g_mem[l]), bf(w_xk[l]), bf(w_xv[l]), 256)
    ck4 = jnp.transpose(cache_k[l], (0, 2, 3, 1))
    cv4 = jnp.transpose(cache_v[l], (0, 2, 3, 1))
    y_p, ksum = _tail(xp, attn, sga, sb3.reshape(tp, D_MODEL), mk_p.reshape(bp, N_MEM, D_XATTN),
                      mv_p.reshape(bp, N_MEM, D_XATTN), wp, bp, sp, 1024, 512, False,
                      key_stream=(ck4, page_table))

    xs = x_sample.reshape(bs, D_MODEL)
    proj_s = _inproj_precise(xs, gmix, w_in[l])
    q_s, k_s, v_s = (proj_s[:, D_ATTN * n:D_ATTN * (n + 1)] for n in range(3))
    u_s = proj_s[:, 3 * D_ATTN:GA_COL0]
    sga_s = proj_s[:, GA_COL0:GA_COL0 + D_MODEL]
    gb_s = proj_s[:, GA_COL0 + D_MODEL:]
    sel = _sample_select(q_s.reshape(bs, 1, D_ATTN), ksum, past_len // MOBA_BLOCK)
    sel_flat = sel[:, :, :MOBA_TOPK].reshape(-1)
    hsplit = lambda a: a.reshape(bs, N_HEADS_A, HEAD_DIM_A).transpose(0, 2, 1)
    attn_s = _sample_attn(sel_flat, page_table.reshape(-1), hsplit(q_s), hsplit(k_s), hsplit(v_s), ck4, cv4,
                          n_pages).transpose(0, 2, 1)
    h0_s = _state_to_lanes(state_ssm_re[l].reshape(bs, N_STATE), state_ssm_im[l].reshape(bs, N_STATE))
    sb_s, ht_s = _s5(u_s.reshape(1, bs, D_SSM), gb_s.reshape(1, bs, D_MODEL), h0_s, ar, ai, bb, cc, d_row,
                     w_glu[l], w_pb[l], bn=bs, lc=1, precise=True)
    sr_s, si_s = _lanes_to_state(ht_s)
    y_s = _tail(xs, attn_s.reshape(bs, D_ATTN), sga_s, sb_s.reshape(bs, D_MODEL),
                cache_mem_k[l].reshape(bs, N_MEM, D_XATTN), cache_mem_v[l].reshape(bs, N_MEM, D_XATTN),
                ws, bs, 1, bs, 16, True)

    kv5 = lambda a, b, s: a.reshape(1, b, s, N_HEADS_A, HEAD_DIM_A)
    st4 = lambda a: a[None]
    mem5 = lambda a: a.reshape(1, bp, N_MEM, N_HEADS_X, HEAD_DIM_X)
    return (y_p.reshape(bp, sp, D_MODEL), y_s.reshape(bs, 1, D_MODEL),
            kv5(k, bp, sp), kv5(v, bp, sp), st4(sr_p), st4(si_p), mem5(mk_p), mem5(mv_p),
            kv5(k_s, bs, 1), kv5(v_s, bs, 1), st4(sr_s), st4(si_s))
```

```python
import functools

import jax
import jax.numpy as jnp
from jax import lax
from jax.experimental import pallas as pl
from jax.experimental.pallas import tpu as pltpu

F32 = jnp.float32
BF16 = jnp.bfloat16
I32 = jnp.int32

D_MODEL = 1024
N_HEADS_A = 8
HEAD_DIM_A = 64
D_ATTN = N_HEADS_A * HEAD_DIM_A
MOBA_BLOCK = 256
BLOCK_SHIFT = MOBA_BLOCK.bit_length() - 1
MOBA_TOPK = 3
D_SSM = 512
SSM_GROUP = 16
N_SSM_GROUPS = D_SSM // SSM_GROUP
SSM_STATE = 64
N_STATE = N_SSM_GROUPS * SSM_STATE
N_MEM = 256
N_HEADS_X = 4
HEAD_DIM_X = 128
D_XATTN = N_HEADS_X * HEAD_DIM_X
N_EXPERT_GROUPS = 4
EXPERTS_PER_GROUP = 4
N_EXPERTS = N_EXPERT_GROUPS * EXPERTS_PER_GROUP
D_FF_EXPERT = 256
D_IN_PROJ = 3 * D_ATTN + D_SSM + 2 * D_MODEL
RMS_EPS = 1e-6
NEG_INF = -1e30
PAGE_SIZE = 128

LANES = 128
VMEM_LIMIT = 56 * 1024 * 1024
BIG_NEG = -3e38
M_INIT = -1e29
NO_IDX = 1e9


def _cparams(*sem):
    return pltpu.CompilerParams(dimension_semantics=sem, vmem_limit_bytes=VMEM_LIMIT)


def _rms(x, g):
    return x * lax.rsqrt(jnp.mean(x * x, axis=-1, keepdims=True) + RMS_EPS) * g


def _dot(a, b):
    return jnp.dot(a, b, preferred_element_type=F32)


def _dot_nt(a, b):
    return lax.dot_general(a, b, (((1,), (1,)), ((), ())), preferred_element_type=F32)


def _split_bf16(x):
    hi = x.astype(BF16)
    return hi, (x - hi.astype(F32)).astype(BF16)


def _split3_bf16(x):
    hi, rest = x.astype(BF16), None
    rest = x - hi.astype(F32)
    mid = rest.astype(BF16)
    return hi, mid, (rest - mid.astype(F32)).astype(BF16)


def _mm(a, w, precise, nt=False):
    dot = _dot_nt if nt else _dot
    if not precise:
        return dot(a.astype(BF16), w.astype(BF16))
    ah, am, al = _split3_bf16(a.astype(F32))
    wh, wm, wl = _split3_bf16(w)
    return ((dot(al, wh) + dot(ah, wl)) + (dot(am, wm) + dot(am, wh) + dot(ah, wm))) + dot(ah, wh)


def _inproj_kernel(x_ref, g_ref, w_ref, q_ref, k_ref, v_ref, kb_ref, vb_ref, u_ref, sga_ref, gb_ref):
    xn = _rms(x_ref[...], g_ref[...]).astype(BF16)

    def mm(c0, n):
        return _dot(xn, w_ref[:, c0:c0 + n])

    q_ref[...] = mm(0, D_ATTN)
    k = mm(D_ATTN, D_ATTN)
    k_ref[0] = k.T
    kb_ref[...] = k.astype(BF16)
    v = mm(2 * D_ATTN, D_ATTN)
    v_ref[0] = v.T
    vb_ref[...] = v.astype(BF16)
    u_ref[...] = mm(3 * D_ATTN, D_SSM)
    sga_ref[...] = jax.nn.sigmoid(mm(3 * D_ATTN + D_SSM, D_MODEL)).astype(BF16)
    gb_ref[...] = mm(3 * D_ATTN + D_SSM + D_MODEL, D_MODEL)


def _inproj(x, g, w_bf, tm, batch):
    t = x.shape[0]
    seq = t // batch
    tiles = seq // tm
    row = lambda n: pl.BlockSpec((tm, n), lambda i: (i, 0))
    col = pl.BlockSpec((1, D_ATTN, tm), lambda i: (i // tiles, 0, i % tiles))
    full = lambda a: pl.BlockSpec(a.shape, lambda i: (0,) * a.ndim)
    shp = lambda n, dt: jax.ShapeDtypeStruct((t, n), dt)
    tshp = jax.ShapeDtypeStruct((batch, D_ATTN, seq), F32)
    return pl.pallas_call(
        _inproj_kernel,
        grid=(t // tm,),
        in_specs=[row(D_MODEL), full(g), full(w_bf)],
        out_specs=[row(D_ATTN), col, col, row(D_ATTN), row(D_ATTN), row(D_SSM), row(D_MODEL), row(D_MODEL)],
        out_shape=[shp(D_ATTN, F32), tshp, tshp, shp(D_ATTN, BF16),
                   shp(D_ATTN, BF16), shp(D_SSM, F32), shp(D_MODEL, BF16), shp(D_MODEL, F32)],
        compiler_params=_cparams("parallel"),
        name="inproj",
    )(x, g, w_bf)


INPROJ_COLS = 512
GA_COL0 = 3 * D_ATTN + D_SSM


def _inproj_precise_kernel(x_ref, g_ref, w_ref, o_ref):
    c0 = pl.program_id(0) * INPROJ_COLS
    acc = _mm(_rms(x_ref[...], g_ref[...]), w_ref[...], True)
    is_ga = (c0 >= GA_COL0) & (c0 < GA_COL0 + D_MODEL)

    @pl.when(is_ga)
    def _():
        o_ref[...] = jax.nn.sigmoid(acc)

    @pl.when(jnp.logical_not(is_ga))
    def _():
        o_ref[...] = acc


def _inproj_precise(x, g, w):
    t = x.shape[0]
    full = lambda a: pl.BlockSpec(a.shape, lambda j: (0,) * a.ndim)
    return pl.pallas_call(
        _inproj_precise_kernel,
        grid=(D_IN_PROJ // INPROJ_COLS,),
        in_specs=[full(x), full(g), pl.BlockSpec((D_MODEL, INPROJ_COLS), lambda j: (0, j))],
        out_specs=pl.BlockSpec((t, INPROJ_COLS), lambda j: (0, j)),
        out_shape=jax.ShapeDtypeStruct((t, D_IN_PROJ), F32),
        compiler_params=_cparams("parallel"),
        name="inproj_sample",
    )(x, g, w)


def _ssm_param_kernel(lr_ref, li_ref, dt_ref, br_ref, bi_ref, ar_ref, ai_ref, bbr_ref, bbi_ref):
    lr = jnp.minimum(lr_ref[...], -1e-4)
    li = li_ref[...]
    dt = jnp.exp(dt_ref[...])
    mag = jnp.exp(lr * dt)
    ar = mag * jnp.cos(li * dt)
    ai = mag * jnp.sin(li * dt)
    den = lr * lr + li * li
    nr = ar - 1.0
    cr = (nr * lr + ai * li) / den
    ci = (ai * lr - nr * li) / den
    ar_ref[...] = ar
    ai_ref[...] = ai
    br = br_ref[...]
    bi = bi_ref[...]
    bbr_ref[...] = cr * br - ci * bi
    bbi_ref[...] = cr * bi + ci * br


def _ssm_params(lam_re, lam_im, log_dt, b_re, b_im):
    n = N_STATE
    lr = lam_re.reshape(1, n)
    li = lam_im.reshape(1, n)
    dt = jnp.broadcast_to(log_dt[:, None], (N_SSM_GROUPS, SSM_STATE)).reshape(1, n)
    brt = b_re.reshape(n, SSM_GROUP).T
    bit = b_im.reshape(n, SSM_GROUP).T
    row = jax.ShapeDtypeStruct((1, n), F32)
    mat = jax.ShapeDtypeStruct((SSM_GROUP, n), F32)
    return pl.pallas_call(_ssm_param_kernel, out_shape=[row, row, mat, mat], name="ssm_params")(
        lr, li, dt, brt, bit)


HALF_STATE = N_STATE // 2
SCAN_COLS = 512
SCAN_TILES = SCAN_COLS // 128


def _s5_kernel(u_ref, gb_ref, h0_ref, ar_ref, ai_ref, bb_ref, cc_ref, d_ref, wglu_ref, wpb_ref, *refs,
               bn, lc, precise):
    m = bn * lc
    if lc > 1:
        perm_ref, sb_ref, ht_ref, s_ref, yt_ref, carry_ref = refs
    else:
        sb_ref, ht_ref, s_ref, yt_ref, carry_ref = refs

    @pl.when(pl.program_id(0) == 0)
    def _():
        carry_ref[...] = h0_ref[...]

    u = u_ref[...].reshape(m, D_SSM)
    us = _dot(perm_ref[...], u.astype(BF16)).astype(BF16) if lc > 1 else u
    for j in range(16):
        sec, jj = divmod(j, 4)
        gbase = (sec // 2) * 16 + jj * 4
        lt = (gbase * SSM_GROUP) // LANES
        bu = _mm(us[:, LANES * lt:LANES * (lt + 1)], bb_ref[j], precise)
        s_ref[2 * j] = bu[:, :LANES]
        s_ref[2 * j + 1] = bu[:, LANES:]

    tiles_half = HALF_STATE // LANES
    for h in range(2):
        for c in range(HALF_STATE // SCAN_COLS):
            re_t = [2 * tiles_half * h + SCAN_TILES * c + n for n in range(SCAN_TILES)]
            im_t = [t + tiles_half for t in re_t]
            a_t = [tiles_half * h + SCAN_TILES * c + n for n in range(SCAN_TILES)]
            ars = [jnp.broadcast_to(ar_ref[:, LANES * t:LANES * (t + 1)], (bn, LANES)) for t in a_t]
            ais = [jnp.broadcast_to(ai_ref[:, LANES * t:LANES * (t + 1)], (bn, LANES)) for t in a_t]

            def body(t, carry, re_t=re_t, im_t=im_t, ars=ars, ais=ais):
                rows = pl.ds(pl.multiple_of(t * bn, bn), bn)
                out = []
                for n in range(SCAN_TILES):
                    xr, xi = carry[2 * n], carry[2 * n + 1]
                    nxr = ars[n] * xr - ais[n] * xi + s_ref[re_t[n], rows, :]
                    nxi = ars[n] * xi + ais[n] * xr + s_ref[im_t[n], rows, :]
                    s_ref[re_t[n], rows, :] = nxr
                    s_ref[im_t[n], rows, :] = nxi
                    out += [nxr, nxi]
                return tuple(out)

            x0 = []
            for n in range(SCAN_TILES):
                x0 += [carry_ref[:, LANES * re_t[n]:LANES * (re_t[n] + 1)],
                       carry_ref[:, LANES * im_t[n]:LANES * (im_t[n] + 1)]]
            xs = lax.fori_loop(0, lc, body, tuple(x0), unroll=min(lc, 8))
            for n in range(SCAN_TILES):
                carry_ref[:, LANES * re_t[n]:LANES * (re_t[n] + 1)] = xs[2 * n]
                carry_ref[:, LANES * im_t[n]:LANES * (im_t[n] + 1)] = xs[2 * n + 1]
    ht_ref[...] = carry_ref[...]

    ys = []
    for h in range(2):
        sdt = F32 if precise else BF16
        xh = jnp.concatenate([s_ref[2 * tiles_half * h + n].astype(sdt) for n in range(2 * tiles_half)], axis=1)
        ys.append(_mm(xh, cc_ref[h], precise))
    y = jnp.concatenate(ys, axis=1)
    if lc > 1:
        for n in range(D_SSM // LANES):
            yt_ref[n] = y[:, LANES * n:LANES * (n + 1)]
        y = jnp.concatenate(
            [jnp.concatenate([yt_ref[n, pl.ds(b, lc, stride=bn), :] for n in range(D_SSM // LANES)], axis=1)
             for b in range(bn)], axis=0)
    y = y + d_ref[...] * u
    z = jax.nn.gelu(y)
    s5 = z * jax.nn.sigmoid(_mm(z, wglu_ref[...], precise))
    pb = _mm(s5, wpb_ref[...], precise)
    gb = gb_ref[...].reshape(m, D_MODEL)
    sb_ref[...] = (jax.nn.sigmoid(gb) * pb).astype(sb_ref.dtype).reshape(sb_ref.shape)


def _s5(u3, gb3, h0, ar, ai, bb, cc, d, wglu, wpb, bn, lc, precise):
    nb, s, _ = u3.shape
    rows = bn * lc // nb
    nchunk = s // rows
    full = lambda a: pl.BlockSpec(a.shape, lambda c: (0,) * a.ndim)
    blk = lambda n: pl.BlockSpec((nb, rows, n), lambda c: (0, c, 0))
    perms = []
    if lc > 1:
        r = jnp.arange(bn * lc)
        perms = [(((r % bn) * lc + r // bn)[:, None] == r[None, :]).astype(BF16)]
    return pl.pallas_call(
        functools.partial(_s5_kernel, bn=bn, lc=lc, precise=precise),
        grid=(nchunk,),
        in_specs=[blk(D_SSM), blk(D_MODEL), full(h0), full(ar), full(ai), full(bb), full(cc), full(d),
                  full(wglu), full(wpb)] + [full(p) for p in perms],
        out_specs=[blk(D_MODEL), full(h0)],
        out_shape=[jax.ShapeDtypeStruct((nb, s, D_MODEL), F32 if precise else BF16),
                   jax.ShapeDtypeStruct(h0.shape, F32)],
        scratch_shapes=[pltpu.VMEM((2 * N_STATE // LANES, bn * lc, LANES), F32),
                        pltpu.VMEM((D_SSM // LANES, bn * lc, LANES), F32),
                        pltpu.VMEM((bn, 2 * N_STATE), F32)],
        compiler_params=_cparams("arbitrary"),
        name="s5",
    )(u3, gb3, h0, ar, ai, bb, cc, d, wglu, wpb, *perms)


def _s5_weights(ar, ai, bbt_re, bbt_im, c_re, c_im):
    g, p, h = N_SSM_GROUPS, SSM_STATE, SSM_GROUP
    eye = jnp.eye(g, dtype=F32)

    def bfull(bt):
        b = bt.reshape(h, g, p)
        return jnp.einsum('hgp,gk->ghkp', b, eye).reshape(g * h, g * p)

    bre, bim = bfull(bbt_re), bfull(bbt_im)
    tiles = []
    for j in range(16):
        sec, jj = divmod(j, 4)
        src = bre if sec % 2 == 0 else bim
        gbase = (sec // 2) * 16 + jj * 4
        lt = (gbase * h) // LANES
        tiles.append(src[LANES * lt:LANES * (lt + 1), gbase * p:(gbase + 4) * p])
    bb = jnp.stack(tiles)

    def cfull(c):
        return jnp.einsum('ghp,gk->gpkh', c, eye).reshape(g * p, g * h)

    cre, cim = cfull(c_re), cfull(c_im)
    halves = []
    for hh in range(2):
        rs = slice(HALF_STATE * hh, HALF_STATE * (hh + 1))
        cs = slice(256 * hh, 256 * (hh + 1))
        halves.append(jnp.concatenate([cre[rs, cs], -cim[rs, cs]], axis=0))
    return bb, jnp.stack(halves)


def _state_to_lanes(re, im):
    b = re.shape[0]
    r = re.reshape(b, 2, HALF_STATE)
    i = im.reshape(b, 2, HALF_STATE)
    return jnp.concatenate([r[:, 0], i[:, 0], r[:, 1], i[:, 1]], axis=1)


def _lanes_to_state(h):
    b = h.shape[0]
    h4 = h.reshape(b, 4, HALF_STATE)
    re = jnp.concatenate([h4[:, 0], h4[:, 2]], axis=1).reshape(b, N_SSM_GROUPS, SSM_STATE)
    im = jnp.concatenate([h4[:, 1], h4[:, 3]], axis=1).reshape(b, N_SSM_GROUPS, SSM_STATE)
    return re, im


MOBA_CHUNK = 4


def _moba_kernel(q_ref, k_ref, v_ref, o_ref, ka0_ref, ka1_ref, selb_ref, *, seq):
    blk, hd = MOBA_BLOCK, HEAD_DIM_A
    nblk = seq // blk
    chunk_rows = MOBA_CHUNK * blk

    k = k_ref[...]
    rblk = lax.broadcasted_iota(I32, (seq, LANES), 0) >> BLOCK_SHIFT
    lane = lax.broadcasted_iota(I32, (seq, LANES), 1)
    kf = k.astype(F32)
    ka0_ref[...] = jnp.where(lane < hd, kf, jnp.where(lane - hd == rblk, 1.0, 0.0)).astype(BF16)
    ka1_ref[...] = jnp.where(lane >= hd, kf, jnp.where(lane == rblk, 1.0, 0.0)).astype(BF16)
    r = lax.broadcasted_iota(I32, (LANES, seq), 0)
    cblk = lax.broadcasted_iota(I32, (LANES, seq), 1) >> BLOCK_SHIFT
    ind = jnp.where((r == cblk) | (r - hd == cblk), 1.0, 0.0).astype(BF16)
    kmean = _dot(ind, k) * (1.0 / blk)
    rr = lax.broadcasted_iota(I32, (LANES, LANES), 0)
    ll = lax.broadcasted_iota(I32, (LANES, LANES), 1)
    keep = ((rr < hd) & (ll >= hd)) | ((rr >= hd) & (ll < hd))
    ahi, alo = _split_bf16(jnp.where(keep, kmean, 0.0))

    qhi, qlo = _split_bf16(q_ref[...])
    sc = _dot_nt(qhi, ahi) + _dot_nt(qlo, ahi) + _dot_nt(qhi, alo)
    sct = sc.T
    blk_i = lax.broadcasted_iota(I32, (nblk, seq), 0)
    blk_f = blk_i.astype(F32)
    own = lax.broadcasted_iota(I32, (nblk, seq), 1) >> BLOCK_SHIFT
    valid = blk_i < own
    biases = []
    for x in (sct[0:nblk], sct[hd:hd + nblk]):
        taken = jnp.zeros((nblk, seq), jnp.bool_)
        for _ in range(MOBA_TOPK):
            sm = jnp.where(valid & jnp.logical_not(taken), x, BIG_NEG)
            mx = jnp.max(sm, axis=0, keepdims=True)
            idx = jnp.min(jnp.where((sm == mx) & (sm > BIG_NEG), blk_f, NO_IDX), axis=0, keepdims=True)
            taken = taken | (blk_f == idx)
        biases.append(jnp.where(taken | (blk_i == own), 0.0, NEG_INF))
    pad = jnp.zeros((hd - nblk, seq), F32)
    selb_ref[...] = jnp.concatenate([biases[0], pad, biases[1], pad], axis=0).T.astype(BF16)

    lane_b = lax.broadcasted_iota(I32, (blk, LANES), 1)
    col_minus_row = (lax.broadcasted_iota(I32, (blk, chunk_rows), 1)
                     - lax.broadcasted_iota(I32, (blk, chunk_rows), 0))
    heads = ((ka0_ref, lane_b < hd), (ka1_ref, lane_b >= hd))
    chunk_shift = MOBA_CHUNK.bit_length() - 1

    def qblock(i, _):
        off = pl.multiple_of(i * blk, blk)
        qs = q_ref[pl.ds(off, blk), :] * (hd ** -0.5)
        sb = selb_ref[pl.ds(off, blk), :].astype(F32)
        qas = [jnp.where(mine, qs, sb).astype(BF16) for _, mine in heads]

        def scores(c, qa, ka_ref):
            coff = pl.multiple_of(c * chunk_rows, chunk_rows)
            return _dot_nt(qa, ka_ref[pl.ds(coff, chunk_rows), :])

        def values(c):
            return v_ref[pl.ds(pl.multiple_of(c * chunk_rows, chunk_rows), chunk_rows), :]

        dc = i >> chunk_shift
        thr = (i - (dc << chunk_shift)) * blk
        vc = values(dc)
        init = []
        for (ka_ref, _), qa in zip(heads, qas):
            s = jnp.where(col_minus_row <= thr, scores(dc, qa, ka_ref), NEG_INF)
            m = jnp.max(s, axis=1, keepdims=True)
            p = jnp.exp(s - m)
            init.append((m, jnp.sum(p, axis=1, keepdims=True), _dot(p.astype(BF16), vc)))

        def chunk(c, carry):
            vc = values(c)
            out = []
            for (ka_ref, _), qa, (m, l, acc) in zip(heads, qas, carry):
                s = scores(c, qa, ka_ref)
                m_new = jnp.maximum(m, jnp.max(s, axis=1, keepdims=True))
                alpha = jnp.exp(m - m_new)
                p = jnp.exp(s - m_new)
                out.append((m_new, alpha * l + jnp.sum(p, axis=1, keepdims=True),
                            alpha * acc + _dot(p.astype(BF16), vc)))
            return tuple(out)

        (_, l0, a0), (_, l1, a1) = lax.fori_loop(0, dc, chunk, tuple(init))
        o_ref[pl.ds(off, blk), :] = jnp.where(lane_b < hd, a0 / l0, a1 / l1).astype(BF16)
        return 0

    lax.fori_loop(0, nblk, qblock, 0)


def _moba_prompt(q, kb, vb, batch, seq):
    spec = pl.BlockSpec((seq, LANES), lambda b, hp: (b, hp))
    return pl.pallas_call(
        functools.partial(_moba_kernel, seq=seq),
        grid=(batch, D_ATTN // LANES),
        in_specs=[spec, spec, spec],
        out_specs=spec,
        out_shape=jax.ShapeDtypeStruct((batch * seq, D_ATTN), BF16),
        scratch_shapes=[pltpu.VMEM((seq, LANES), BF16)] * 3,
        compiler_params=_cparams("parallel", "parallel"),
        name="moba_prompt",
    )(q, kb, vb)


def _memkv_kernel(m_ref, g_ref, wk_ref, wv_ref, mk_ref, mv_ref):
    mn = _rms(m_ref[...], g_ref[...]).astype(BF16)
    mk_ref[...] = _dot(mn, wk_ref[...])
    mv_ref[...] = _dot(mn, wv_ref[...])


def _memkv(mem, g, wk, wv, tm):
    t = mem.shape[0]
    row = lambda n: pl.BlockSpec((tm, n), lambda i: (i, 0))
    full = lambda a: pl.BlockSpec(a.shape, lambda i: (0,) * a.ndim)
    shp = jax.ShapeDtypeStruct((t, D_XATTN), F32)
    return pl.pallas_call(
        _memkv_kernel, grid=(t // tm,),
        in_specs=[row(D_MODEL), full(g), full(wk), full(wv)],
        out_specs=[row(D_XATTN)] * 2, out_shape=[shp, shp],
        compiler_params=_cparams("parallel"), name="memkv",
    )(mem, g, wk, wv)


def _merge_kernel(x_ref, at_ref, sga_ref, sb_ref, wpa_ref, wo_ref, gx_ref, wxq_ref, x1_ref, xq_ref, *, precise):
    pa = _mm(at_ref[...], wpa_ref[...], precise)
    merged = sga_ref[...].astype(F32) * pa + sb_ref[...].astype(F32)
    x1 = x_ref[...] + _mm(merged, wo_ref[...], precise)
    x1_ref[...] = x1
    xq_ref[...] = _mm(_rms(x1, gx_ref[...]), wxq_ref[...], precise).astype(xq_ref.dtype)


def _merge(x, attn, sga, sb, wpa, wo, gx, wxq, tm, precise):
    t = x.shape[0]
    row = lambda n: pl.BlockSpec((tm, n), lambda i: (i, 0))
    full = lambda a: pl.BlockSpec(a.shape, lambda i: (0,) * a.ndim)
    return pl.pallas_call(
        functools.partial(_merge_kernel, precise=precise), grid=(t // tm,),
        in_specs=[row(D_MODEL), row(D_ATTN), row(D_MODEL), row(D_MODEL), full(wpa), full(wo), full(gx),
                  full(wxq)],
        out_specs=[row(D_MODEL), row(D_XATTN)],
        out_shape=[jax.ShapeDtypeStruct((t, D_MODEL), F32),
                   jax.ShapeDtypeStruct((t, D_XATTN), F32 if precise else BF16)],
        compiler_params=_cparams("parallel"), name="merge",
    )(x, attn, sga, sb, wpa, wo, gx, wxq)


def _xattn_kernel(q_ref, mk_ref, mv_ref, o_ref, *, precise):
    q = q_ref[0]
    mk = mk_ref[0]
    mv = mv_ref[0]
    outs = []
    for h in range(N_HEADS_X):
        cs = slice(HEAD_DIM_X * h, HEAD_DIM_X * (h + 1))
        s = _mm(q[:, cs], mk[:, cs], precise, nt=True) * (HEAD_DIM_X ** -0.5)
        e = jnp.exp(s - jnp.max(s, axis=1, keepdims=True))
        p = e / jnp.sum(e, axis=1, keepdims=True)
        outs.append(_mm(p, mv[:, cs], precise))
    o_ref[0] = jnp.concatenate(outs, axis=1).astype(o_ref.dtype)


def _xattn(q3, mk3, mv3, tq, precise):
    b, s, _ = q3.shape
    qspec = pl.BlockSpec((1, tq, D_XATTN), lambda bi, si: (bi, si, 0))
    mspec = pl.BlockSpec((1, N_MEM, D_XATTN), lambda bi, si: (bi, 0, 0))
    return pl.pallas_call(
        functools.partial(_xattn_kernel, precise=precise), grid=(b, s // tq),
        in_specs=[qspec, mspec, mspec], out_specs=qspec,
        out_shape=jax.ShapeDtypeStruct(q3.shape, q3.dtype),
        compiler_params=_cparams("parallel", "parallel"), name="xattn",
    )(q3, mk3, mv3)


GROUP_LANE0 = N_EXPERTS


def _post_kernel(x1_ref, xo_ref, wxo_ref, gf_ref, wr_ref, br_ref, x2_ref, xn_ref, cmb_ref, *, precise):
    x2 = x1_ref[...] + _mm(xo_ref[...], wxo_ref[...], precise)
    x2_ref[...] = x2
    t = _rms(x2, gf_ref[...])
    thi, tlo = _split_bf16(t)
    xn_ref[...] = thi
    if precise:
        logits = _mm(t, wr_ref[...], True) + br_ref[...]
    else:
        wrh, wrl = _split_bf16(wr_ref[...])
        logits = _dot(thi, wrh) + _dot(tlo, wrh) + _dot(thi, wrl) + br_ref[...]

    lane_i = lax.broadcasted_iota(I32, logits.shape, 1)
    lane = lane_i.astype(F32)
    lane_group = (lane_i >> (EXPERTS_PER_GROUP.bit_length() - 1)).astype(F32)
    isg = (lane_i >= GROUP_LANE0) & (lane_i < GROUP_LANE0 + N_EXPERT_GROUPS)
    gmax = jnp.max(jnp.where(isg, logits, BIG_NEG), axis=1, keepdims=True)
    eg = jnp.where(isg, jnp.exp(jnp.where(isg, logits, gmax) - gmax), 0.0)
    gp = eg / jnp.sum(eg, axis=1, keepdims=True)
    pg = jnp.max(jnp.where(isg, gp, -1.0), axis=1, keepdims=True)
    gi = jnp.min(jnp.where(isg & (gp == pg), lane, NO_IDX), axis=1, keepdims=True) - GROUP_LANE0

    insel = (lane_i < N_EXPERTS) & (lane_group == gi)
    el = jnp.where(insel, logits, BIG_NEG)
    m1 = jnp.max(el, axis=1, keepdims=True)
    i1 = jnp.min(jnp.where(insel & (el == m1), lane, NO_IDX), axis=1, keepdims=True)
    rest = insel & (lane != i1)
    el2 = jnp.where(rest, logits, BIG_NEG)
    m2 = jnp.max(el2, axis=1, keepdims=True)
    i2 = jnp.min(jnp.where(rest & (el2 == m2), lane, NO_IDX), axis=1, keepdims=True)
    e2 = jnp.exp(m2 - m1)
    den = 1.0 + e2
    cmb_ref[...] = jnp.where(lane == i1, (1.0 / den) * pg, jnp.where(lane == i2, (e2 / den) * pg, 0.0))


def _post(x1, xo, wxo, gf, wr, br, tm, precise):
    t = x1.shape[0]
    row = lambda n: pl.BlockSpec((tm, n), lambda i: (i, 0))
    full = lambda a: pl.BlockSpec(a.shape, lambda i: (0,) * a.ndim)
    return pl.pallas_call(
        functools.partial(_post_kernel, precise=precise), grid=(t // tm,),
        in_specs=[row(D_MODEL), row(D_XATTN), full(wxo), full(gf), full(wr), full(br)],
        out_specs=[row(D_MODEL), row(D_MODEL), row(LANES)],
        out_shape=[jax.ShapeDtypeStruct((t, D_MODEL), F32), jax.ShapeDtypeStruct((t, D_MODEL), BF16),
                   jax.ShapeDtypeStruct((t, LANES), F32)],
        compiler_params=_cparams("parallel"), name="post",
    )(x1, xo, wxo, gf, wr, br)


EXPERTS_PER_STEP = 2
PAGES_PER_STEP = 32
PAGES_PER_BLOCK = MOBA_BLOCK // PAGE_SIZE
BLOCKS_PER_STEP = PAGES_PER_STEP // PAGES_PER_BLOCK


def _block_sum_step(s, pages, ks_ref):
    lane = lax.broadcasted_iota(I32, (D_ATTN, LANES), 1)

    @pl.when(s == 0)
    def _():
        ks_ref[0] = jnp.zeros((D_ATTN, LANES), F32)

    ks = ks_ref[0]
    for r in range(BLOCKS_PER_STEP):
        acc = pages[PAGES_PER_BLOCK * r][0].reshape(D_ATTN, PAGE_SIZE)
        for t in range(1, PAGES_PER_BLOCK):
            acc = acc + pages[PAGES_PER_BLOCK * r + t][0].reshape(D_ATTN, PAGE_SIZE)
        ks = jnp.where(lane == s * BLOCKS_PER_STEP + r, jnp.sum(acc, axis=1, keepdims=True), ks)
    ks_ref[0] = ks


def _moe_stream_kernel(pt_ref, x2_ref, xn_ref, cmb_ref, w13_ref, w2_ref, gfin_ref, *refs, steps_per_seq):
    del pt_ref
    pages, (y_ref, ks_ref, acc_ref) = refs[:PAGES_PER_STEP], refs[PAGES_PER_STEP:]
    _moe_kernel(x2_ref, xn_ref, cmb_ref, w13_ref, w2_ref, gfin_ref, y_ref, acc_ref)
    g = pl.program_id(0) * pl.num_programs(1) + pl.program_id(1)
    _block_sum_step(lax.rem(g, steps_per_seq), pages, ks_ref)


def _moe_kernel(x2_ref, xn_ref, cmb_ref, w13_ref, w2_ref, gfin_ref, y_ref, acc_ref):
    step = pl.program_id(1)

    @pl.when(step == 0)
    def _():
        acc_ref[...] = jnp.zeros_like(acc_ref)

    xn = xn_ref[...]
    cmb = cmb_ref[...]
    lane = lax.broadcasted_iota(I32, cmb.shape, 1)
    hds = []
    for k in range(EXPERTS_PER_STEP):
        h = _dot(xn, w13_ref[k])
        cw = jnp.sum(jnp.where(lane == step * EXPERTS_PER_STEP + k, cmb, 0.0), axis=1, keepdims=True)
        hds.append((jax.nn.silu(h[:, :D_FF_EXPERT]) * h[:, D_FF_EXPERT:] * cw).astype(BF16))
    w2 = w2_ref[...].reshape(EXPERTS_PER_STEP * D_FF_EXPERT, D_MODEL)
    acc_ref[...] += _dot(jnp.concatenate(hds, axis=1), w2)

    @pl.when(step == pl.num_programs(1) - 1)
    def _():
        y_ref[...] = _rms(x2_ref[...] + acc_ref[...], gfin_ref[...])


def _moe(x2, xn, cmb, w13, w2, gfin, tm):
    t = x2.shape[0]
    row = lambda n: pl.BlockSpec((tm, n), lambda i, e: (i, 0))
    return pl.pallas_call(
        _moe_kernel, grid=(t // tm, N_EXPERTS // EXPERTS_PER_STEP),
        in_specs=[row(D_MODEL), row(D_MODEL), row(LANES),
                  pl.BlockSpec((EXPERTS_PER_STEP, D_MODEL, 2 * D_FF_EXPERT), lambda i, e: (e, 0, 0)),
                  pl.BlockSpec((EXPERTS_PER_STEP, D_FF_EXPERT, D_MODEL), lambda i, e: (e, 0, 0)),
                  pl.BlockSpec(gfin.shape, lambda i, e: (0, 0))],
        out_specs=row(D_MODEL),
        out_shape=jax.ShapeDtypeStruct((t, D_MODEL), F32),
        scratch_shapes=[pltpu.VMEM((tm, D_MODEL), F32)],
        compiler_params=_cparams("parallel", "arbitrary"), name="moe",
    )(x2, xn, cmb, w13, w2, gfin)


def _moe_with_key_sums(x2, xn, cmb, w13, w2, gfin, tm, cache4, page_table):
    t = x2.shape[0]
    nb, npages = page_table.shape
    steps_per_seq = npages // PAGES_PER_STEP
    n_tiles = t // tm
    n_esteps = N_EXPERTS // EXPERTS_PER_STEP
    assert n_tiles * n_esteps == nb * steps_per_seq and npages // PAGES_PER_BLOCK <= LANES

    def seq_of(i, e):
        return (i * n_esteps + e) // steps_per_seq

    def pspec(r):
        def imap(i, e, pt):
            g = i * n_esteps + e
            return (pt[g // steps_per_seq, lax.rem(g, steps_per_seq) * PAGES_PER_STEP + r], 0, 0, 0)
        return pl.BlockSpec((1, N_HEADS_A, HEAD_DIM_A, PAGE_SIZE), imap)

    row = lambda n: pl.BlockSpec((tm, n), lambda i, e, pt: (i, 0))
    return pl.pallas_call(
        functools.partial(_moe_stream_kernel, steps_per_seq=steps_per_seq),
        grid_spec=pltpu.PrefetchScalarGridSpec(
            num_scalar_prefetch=1, grid=(n_tiles, n_esteps),
            in_specs=[row(D_MODEL), row(D_MODEL), row(LANES),
                      pl.BlockSpec((EXPERTS_PER_STEP, D_MODEL, 2 * D_FF_EXPERT), lambda i, e, pt: (e, 0, 0)),
                      pl.BlockSpec((EXPERTS_PER_STEP, D_FF_EXPERT, D_MODEL), lambda i, e, pt: (e, 0, 0)),
                      pl.BlockSpec(gfin.shape, lambda i, e, pt: (0, 0))]
            + [pspec(r) for r in range(PAGES_PER_STEP)],
            out_specs=[row(D_MODEL),
                       pl.BlockSpec((1, D_ATTN, LANES), lambda i, e, pt: (seq_of(i, e), 0, 0))],
            scratch_shapes=[pltpu.VMEM((tm, D_MODEL), F32)]),
        out_shape=[jax.ShapeDtypeStruct((t, D_MODEL), F32), jax.ShapeDtypeStruct((nb, D_ATTN, LANES), F32)],
        compiler_params=_cparams("arbitrary", "arbitrary"), name="moe_keysums",
    )(page_table, x2, xn, cmb, w13, w2, gfin, *([cache4] * PAGES_PER_STEP))


def _ssel_kernel(q_ref, ks_ref, o_ref, *, n_past):
    hrow = lax.broadcasted_iota(I32, (N_HEADS_A, D_ATTN), 0)
    hcol = lax.broadcasted_iota(I32, (N_HEADS_A, D_ATTN), 1) >> (HEAD_DIM_A.bit_length() - 1)
    qbd = jnp.where(hrow == hcol, jnp.broadcast_to(q_ref[0], (N_HEADS_A, D_ATTN)), 0.0)
    sc = _mm(qbd, ks_ref[0], True) * (1.0 / MOBA_BLOCK)
    bl_i = lax.broadcasted_iota(I32, sc.shape, 1)
    bl = bl_i.astype(F32)
    valid = bl_i < n_past
    taken = jnp.zeros(sc.shape, jnp.bool_)
    out = jnp.zeros(sc.shape, F32)
    for r in range(MOBA_TOPK):
        sm = jnp.where(valid & jnp.logical_not(taken), sc, BIG_NEG)
        mx = jnp.max(sm, axis=1, keepdims=True)
        idx = jnp.min(jnp.where((sm == mx) & (sm > BIG_NEG), bl, NO_IDX), axis=1, keepdims=True)
        taken = taken | (bl == idx)
        out = jnp.where(bl_i == r, idx, out)
    o_ref[0] = out.astype(I32)


def _sample_select(q3, ksum, n_past):
    nb = q3.shape[0]
    return pl.pallas_call(
        functools.partial(_ssel_kernel, n_past=n_past), grid=(nb,),
        in_specs=[pl.BlockSpec((1, 1, D_ATTN), lambda b: (b, 0, 0)),
                  pl.BlockSpec((1, D_ATTN, LANES), lambda b: (b, 0, 0))],
        out_specs=pl.BlockSpec((1, N_HEADS_A, LANES), lambda b: (b, 0, 0)),
        out_shape=jax.ShapeDtypeStruct((nb, N_HEADS_A, LANES), I32),
        compiler_params=_cparams("parallel"), name="sample_select",
    )(q3, ksum)


PAGES_PER_HEAD = MOBA_TOPK * PAGES_PER_BLOCK
N_SEL_PAGES = N_HEADS_A * PAGES_PER_HEAD


def _sattn_kernel(sel_ref, pt_ref, q_ref, kn_ref, vn_ref, *refs):
    del sel_ref, pt_ref
    kp, vp, o_ref = refs[:N_SEL_PAGES], refs[N_SEL_PAGES:2 * N_SEL_PAGES], refs[2 * N_SEL_PAGES]
    for h in range(N_HEADS_A):
        qc = q_ref[0][:, h:h + 1] * (HEAD_DIM_A ** -0.5)
        kts = [kp[h * PAGES_PER_HEAD + r][0, 0] for r in range(PAGES_PER_HEAD)]
        vts = [vp[h * PAGES_PER_HEAD + r][0, 0] for r in range(PAGES_PER_HEAD)]
        ss = [jnp.sum(kt * qc, axis=0, keepdims=True) for kt in kts]
        s_self = jnp.sum(qc * kn_ref[0][:, h:h + 1], axis=0, keepdims=True)
        mx = s_self
        for s in ss:
            mx = jnp.maximum(mx, jnp.max(s, axis=1, keepdims=True))
        p_self = jnp.exp(s_self - mx)
        den = p_self
        acc = jnp.zeros((HEAD_DIM_A, PAGE_SIZE), F32)
        for s, vt in zip(ss, vts):
            p = jnp.exp(s - mx)
            den = den + jnp.sum(p, axis=1, keepdims=True)
            acc = acc + vt * p
        out = p_self * vn_ref[0][:, h:h + 1] + jnp.sum(acc, axis=1, keepdims=True)
        o_ref[0, :, h:h + 1] = out / den


def _sample_attn(sel_flat, pt_flat, q3, kn3, vn3, ck4, cv4, n_pages):
    nb = q3.shape[0]

    def pspec(slot):
        h, rem = divmod(slot, PAGES_PER_HEAD)
        r, half = divmod(rem, PAGES_PER_BLOCK)

        def imap(b, sel, pt):
            blk = sel[(b * N_HEADS_A + h) * MOBA_TOPK + r]
            return (pt[b * n_pages + blk * PAGES_PER_BLOCK + half], h, 0, 0)

        return pl.BlockSpec((1, 1, HEAD_DIM_A, PAGE_SIZE), imap)

    tok = pl.BlockSpec((1, HEAD_DIM_A, N_HEADS_A), lambda b, sel, pt: (b, 0, 0))
    return pl.pallas_call(
        _sattn_kernel,
        grid_spec=pltpu.PrefetchScalarGridSpec(
            num_scalar_prefetch=2, grid=(nb,),
            in_specs=[tok, tok, tok] + [pspec(s) for s in range(N_SEL_PAGES)] * 2,
            out_specs=tok),
        out_shape=jax.ShapeDtypeStruct((nb, HEAD_DIM_A, N_HEADS_A), F32),
        compiler_params=_cparams("arbitrary"), name="sample_attn",
    )(sel_flat, pt_flat, q3, kn3, vn3, *([ck4] * N_SEL_PAGES), *([cv4] * N_SEL_PAGES))


def _tail(x, attn, sga, sb, mk3, mv3, w, batch, seq, tm, tq, precise, key_stream=None):
    x1, xq = _merge(x, attn, sga, sb, w['wpa'], w['wo'], w['gx'], w['wxq'], tm, precise)
    xq3 = xq.reshape(batch, seq, D_XATTN)
    pad = (-seq) % tq
    if pad:
        xq3 = jnp.pad(xq3, ((0, 0), (0, pad), (0, 0)))
    xo = _xattn(xq3, mk3, mv3, tq, precise)[:, :seq].reshape(batch * seq, D_XATTN)
    x2, xn, cmb = _post(x1, xo, w['wxo'], w['gf'], w['wr'], w['br'], tm, precise)
    if key_stream is None:
        return _moe(x2, xn, cmb, w['w13'], w['w2'], w['gfin'], tm)
    return _moe_with_key_sums(x2, xn, cmb, w['w13'], w['w2'], w['gfin'], tm, *key_stream)


def kernel(x_prompt, x_sample, mem_prompt, cache_k, cache_v, page_table, state_ssm_re, state_ssm_im,
           cache_mem_k, cache_mem_v, g_mix, w_in, ssm_lambda_re, ssm_lambda_im, ssm_log_dt,
           ssm_b_re, ssm_b_im, ssm_c_re, ssm_c_im, ssm_d, w_glu, w_pa, w_pb, w_o, g_x, g_mem,
           w_xq, w_xk, w_xv, w_xo, g_ffn, w_group, b_group, w_erouter, b_erouter, w1, w3, w2, g_final):
    depth = w_in.shape[0]
    assert depth == 1
    l = 0
    bp, sp, _ = x_prompt.shape
    bs, ss, _ = x_sample.shape
    assert ss == 1
    n_pages = page_table.shape[1]
    past_len = n_pages * PAGE_SIZE
    assert past_len % MOBA_BLOCK == 0 and sp % MOBA_BLOCK == 0

    row = lambda a: a.reshape(1, -1).astype(F32)
    bf = lambda a: a.astype(BF16)
    wr = jnp.zeros((D_MODEL, LANES), F32)
    wr = wr.at[:, :N_EXPERTS].set(w_erouter[l]).at[:, GROUP_LANE0:GROUP_LANE0 + N_EXPERT_GROUPS].set(w_group[l])
    br = jnp.zeros((1, LANES), F32)
    br = br.at[0, :N_EXPERTS].set(b_erouter[l]).at[0, GROUP_LANE0:GROUP_LANE0 + N_EXPERT_GROUPS].set(b_group[l])
    ws = dict(wpa=w_pa[l], wo=w_o[l], gx=row(g_x[l]), wxq=w_xq[l], wxo=w_xo[l], gf=row(g_ffn[l]),
              wr=wr, br=br, w13=bf(jnp.concatenate([w1[l], w3[l]], axis=-1)), w2=bf(w2[l]),
              gfin=row(g_final))
    wp = dict(ws, wpa=bf(w_pa[l]), wo=bf(w_o[l]), wxq=bf(w_xq[l]), wxo=bf(w_xo[l]))
    gmix = row(g_mix[l])

    ar, ai, bbt_re, bbt_im = _ssm_params(ssm_lambda_re[l], ssm_lambda_im[l], ssm_log_dt[l], ssm_b_re[l],
                                         ssm_b_im[l])
    bb, cc = _s5_weights(ar, ai, bbt_re, bbt_im, ssm_c_re[l], ssm_c_im[l])
    d_row = row(ssm_d[l])

    tp = bp * sp
    xp = x_prompt.reshape(tp, D_MODEL)
    q, kt, vt, kb, vb, u, sga, gb = _inproj(xp, gmix, bf(w_in[l]), 512, bp)
    attn = _moba_prompt(q, kb, vb, bp, sp)
    h0 = jnp.zeros((bp, 2 * N_STATE), F32)
    sb3, ht = _s5(u.reshape(bp, sp, D_SSM), gb.reshape(bp, sp, D_MODEL), h0, ar, ai, bf(bb), bf(cc), d_row,
                  bf(w_glu[l]), bf(w_pb[l]), bn=bp, lc=128, precise=False)
    sr_p, si_p = _lanes_to_state(ht)
    mk_p, mv_p = _memkv(mem_prompt.reshape(bp * N_MEM, D_MODEL), row(g_mem[l]), bf(w_xk[l]), bf(w_xv[l]), 256)
    ck4 = jnp.transpose(cache_k[l], (0, 2, 3, 1))
    cv4 = jnp.transpose(cache_v[l], (0, 2, 3, 1))
    y_p, ksum = _tail(xp, attn, sga, sb3.reshape(tp, D_MODEL), mk_p.reshape(bp, N_MEM, D_XATTN),
                      mv_p.reshape(bp, N_MEM, D_XATTN), wp, bp, sp, 1024, 512, False,
                      key_stream=(ck4, page_table))

    xs = x_sample.reshape(bs, D_MODEL)
    proj_s = _inproj_precise(xs, gmix, w_in[l])
    q_s, k_s, v_s = (proj_s[:, D_ATTN * n:D_ATTN * (n + 1)] for n in range(3))
    u_s = proj_s[:, 3 * D_ATTN:GA_COL0]
    sga_s = proj_s[:, GA_COL0:GA_COL0 + D_MODEL]
    gb_s = proj_s[:, GA_COL0 + D_MODEL:]
    sel = _sample_select(q_s.reshape(bs, 1, D_ATTN), ksum, past_len // MOBA_BLOCK)
    sel_flat = sel[:, :, :MOBA_TOPK].reshape(-1)
    hsplit = lambda a: a.reshape(bs, N_HEADS_A, HEAD_DIM_A).transpose(0, 2, 1)
    attn_s = _sample_attn(sel_flat, page_table.reshape(-1), hsplit(q_s), hsplit(k_s), hsplit(v_s), ck4, cv4,
                          n_pages).transpose(0, 2, 1)
    h0_s = _state_to_lanes(state_ssm_re[l].reshape(bs, N_STATE), state_ssm_im[l].reshape(bs, N_STATE))
    sb_s, ht_s = _s5(u_s.reshape(1, bs, D_SSM), gb_s.reshape(1, bs, D_MODEL), h0_s, ar, ai, bb, cc, d_row,
                     w_glu[l], w_pb[l], bn=bs, lc=1, precise=True)
    sr_s, si_s = _lanes_to_state(ht_s)
    y_s = _tail(xs, attn_s.reshape(bs, D_ATTN), sga_s, sb_s.reshape(bs, D_MODEL),
                cache_mem_k[l].reshape(bs, N_MEM, D_XATTN), cache_mem_v[l].reshape(bs, N_MEM, D_XATTN),
                ws, bs, 1, bs, 16, True)

    kv5 = lambda a, b, s: a.reshape(1, b, s, N_HEADS_A, HEAD_DIM_A)
    kvt5 = lambda a: a.reshape(bp, N_HEADS_A, HEAD_DIM_A, sp).transpose(0, 3, 1, 2)[None]
    st4 = lambda a: a[None]
    mem5 = lambda a: a.reshape(1, bp, N_MEM, N_HEADS_X, HEAD_DIM_X)
    return (y_p.reshape(bp, sp, D_MODEL), y_s.reshape(bs, 1, D_MODEL),
            kvt5(kt), kvt5(vt), st4(sr_p), st4(si_p), mem5(mk_p), mem5(mv_p),
            kv5(k_s, bs, 1), kv5(v_s, bs, 1), st4(sr_s), st4(si_s))
```

```python
import functools

import jax
import jax.numpy as jnp
from jax import lax
from jax.experimental import pallas as pl
from jax.experimental.pallas import tpu as pltpu

F32 = jnp.float32
BF16 = jnp.bfloat16
I32 = jnp.int32

D_MODEL = 1024
N_HEADS_A = 8
HEAD_DIM_A = 64
D_ATTN = N_HEADS_A * HEAD_DIM_A
MOBA_BLOCK = 256
BLOCK_SHIFT = MOBA_BLOCK.bit_length() - 1
MOBA_TOPK = 3
D_SSM = 512
SSM_GROUP = 16
N_SSM_GROUPS = D_SSM // SSM_GROUP
SSM_STATE = 64
N_STATE = N_SSM_GROUPS * SSM_STATE
N_MEM = 256
N_HEADS_X = 4
HEAD_DIM_X = 128
D_XATTN = N_HEADS_X * HEAD_DIM_X
N_EXPERT_GROUPS = 4
EXPERTS_PER_GROUP = 4
N_EXPERTS = N_EXPERT_GROUPS * EXPERTS_PER_GROUP
D_FF_EXPERT = 256
D_IN_PROJ = 3 * D_ATTN + D_SSM + 2 * D_MODEL
RMS_EPS = 1e-6
NEG_INF = -1e30
PAGE_SIZE = 128

LANES = 128
VMEM_LIMIT = 56 * 1024 * 1024
BIG_NEG = -3e38
M_INIT = -1e29
NO_IDX = 1e9


def _cparams(*sem):
    return pltpu.CompilerParams(dimension_semantics=sem, vmem_limit_bytes=VMEM_LIMIT)


def _rms(x, g):
    return x * lax.rsqrt(jnp.mean(x * x, axis=-1, keepdims=True) + RMS_EPS) * g


def _dot(a, b):
    return jnp.dot(a, b, preferred_element_type=F32)


def _dot_nt(a, b):
    return lax.dot_general(a, b, (((1,), (1,)), ((), ())), preferred_element_type=F32)


def _split_bf16(x):
    hi = x.astype(BF16)
    return hi, (x - hi.astype(F32)).astype(BF16)


def _split3_bf16(x):
    hi, rest = x.astype(BF16), None
    rest = x - hi.astype(F32)
    mid = rest.astype(BF16)
    return hi, mid, (rest - mid.astype(F32)).astype(BF16)


def _mm(a, w, precise, nt=False):
    dot = _dot_nt if nt else _dot
    if not precise:
        return dot(a.astype(BF16), w.astype(BF16))
    ah, am, al = _split3_bf16(a.astype(F32))
    wh, wm, wl = _split3_bf16(w)
    return ((dot(al, wh) + dot(ah, wl)) + (dot(am, wm) + dot(am, wh) + dot(ah, wm))) + dot(ah, wh)


def _inproj_kernel(x_ref, g_ref, w_ref, q_ref, k_ref, v_ref, kb_ref, vb_ref, u_ref, sga_ref, gb_ref):
    xn = _rms(x_ref[...], g_ref[...]).astype(BF16)

    def mm(c0, n):
        return _dot(xn, w_ref[:, c0:c0 + n])

    q_ref[...] = mm(0, D_ATTN)
    k = mm(D_ATTN, D_ATTN)
    k_ref[0] = k.T
    kb_ref[...] = k.astype(BF16)
    v = mm(2 * D_ATTN, D_ATTN)
    v_ref[0] = v.T
    vb_ref[...] = v.astype(BF16)
    u_ref[...] = mm(3 * D_ATTN, D_SSM)
    sga_ref[...] = jax.nn.sigmoid(mm(3 * D_ATTN + D_SSM, D_MODEL)).astype(BF16)
    gb_ref[...] = mm(3 * D_ATTN + D_SSM + D_MODEL, D_MODEL)


def _inproj(x, g, w_bf, tm, batch):
    t = x.shape[0]
    seq = t // batch
    tiles = seq // tm
    row = lambda n: pl.BlockSpec((tm, n), lambda i: (i, 0))
    col = pl.BlockSpec((1, D_ATTN, tm), lambda i: (i // tiles, 0, i % tiles))
    full = lambda a: pl.BlockSpec(a.shape, lambda i: (0,) * a.ndim)
    shp = lambda n, dt: jax.ShapeDtypeStruct((t, n), dt)
    tshp = jax.ShapeDtypeStruct((batch, D_ATTN, seq), F32)
    return pl.pallas_call(
        _inproj_kernel,
        grid=(t // tm,),
        in_specs=[row(D_MODEL), full(g), full(w_bf)],
        out_specs=[row(D_ATTN), col, col, row(D_ATTN), row(D_ATTN), row(D_SSM), row(D_MODEL), row(D_MODEL)],
        out_shape=[shp(D_ATTN, F32), tshp, tshp, shp(D_ATTN, BF16),
                   shp(D_ATTN, BF16), shp(D_SSM, F32), shp(D_MODEL, BF16), shp(D_MODEL, F32)],
        compiler_params=_cparams("parallel"),
        name="inproj",
    )(x, g, w_bf)


INPROJ_COLS = 512
GA_COL0 = 3 * D_ATTN + D_SSM


def _inproj_precise_kernel(x_ref, g_ref, w_ref, o_ref):
    c0 = pl.program_id(0) * INPROJ_COLS
    acc = _mm(_rms(x_ref[...], g_ref[...]), w_ref[...], True)
    is_ga = (c0 >= GA_COL0) & (c0 < GA_COL0 + D_MODEL)

    @pl.when(is_ga)
    def _():
        o_ref[...] = jax.nn.sigmoid(acc)

    @pl.when(jnp.logical_not(is_ga))
    def _():
        o_ref[...] = acc


def _inproj_precise(x, g, w):
    t = x.shape[0]
    full = lambda a: pl.BlockSpec(a.shape, lambda j: (0,) * a.ndim)
    return pl.pallas_call(
        _inproj_precise_kernel,
        grid=(D_IN_PROJ // INPROJ_COLS,),
        in_specs=[full(x), full(g), pl.BlockSpec((D_MODEL, INPROJ_COLS), lambda j: (0, j))],
        out_specs=pl.BlockSpec((t, INPROJ_COLS), lambda j: (0, j)),
        out_shape=jax.ShapeDtypeStruct((t, D_IN_PROJ), F32),
        compiler_params=_cparams("parallel"),
        name="inproj_sample",
    )(x, g, w)


def _ssm_param_kernel(lr_ref, li_ref, dt_ref, br_ref, bi_ref, ar_ref, ai_ref, bbr_ref, bbi_ref):
    lr = jnp.minimum(lr_ref[...], -1e-4)
    li = li_ref[...]
    dt = jnp.exp(dt_ref[...])
    mag = jnp.exp(lr * dt)
    ar = mag * jnp.cos(li * dt)
    ai = mag * jnp.sin(li * dt)
    den = lr * lr + li * li
    nr = ar - 1.0
    cr = (nr * lr + ai * li) / den
    ci = (ai * lr - nr * li) / den
    ar_ref[...] = ar
    ai_ref[...] = ai
    br = br_ref[...]
    bi = bi_ref[...]
    bbr_ref[...] = cr * br - ci * bi
    bbi_ref[...] = cr * bi + ci * br


def _ssm_params(lam_re, lam_im, log_dt, b_re, b_im):
    n = N_STATE
    lr = lam_re.reshape(1, n)
    li = lam_im.reshape(1, n)
    dt = jnp.broadcast_to(log_dt[:, None], (N_SSM_GROUPS, SSM_STATE)).reshape(1, n)
    brt = b_re.reshape(n, SSM_GROUP).T
    bit = b_im.reshape(n, SSM_GROUP).T
    row = jax.ShapeDtypeStruct((1, n), F32)
    mat = jax.ShapeDtypeStruct((SSM_GROUP, n), F32)
    return pl.pallas_call(_ssm_param_kernel, out_shape=[row, row, mat, mat], name="ssm_params")(
        lr, li, dt, brt, bit)


HALF_STATE = N_STATE // 2
SCAN_COLS = 512
SCAN_TILES = SCAN_COLS // 128


def _s5_kernel(u_ref, gb_ref, h0_ref, ar_ref, ai_ref, bb_ref, cc_ref, d_ref, wglu_ref, wpb_ref, *refs,
               bn, lc, precise):
    m = bn * lc
    if lc > 1:
        perm_ref, sb_ref, ht_ref, s_ref, yt_ref, carry_ref = refs
    else:
        sb_ref, ht_ref, s_ref, yt_ref, carry_ref = refs

    @pl.when(pl.program_id(0) == 0)
    def _():
        carry_ref[...] = h0_ref[...]

    u = u_ref[...].reshape(m, D_SSM)
    us = _dot(perm_ref[...], u.astype(BF16)).astype(BF16) if lc > 1 else u
    for j in range(16):
        sec, jj = divmod(j, 4)
        gbase = (sec // 2) * 16 + jj * 4
        lt = (gbase * SSM_GROUP) // LANES
        bu = _mm(us[:, LANES * lt:LANES * (lt + 1)], bb_ref[j], precise)
        s_ref[2 * j] = bu[:, :LANES]
        s_ref[2 * j + 1] = bu[:, LANES:]

    tiles_half = HALF_STATE // LANES
    for h in range(2):
        for c in range(HALF_STATE // SCAN_COLS):
            re_t = [2 * tiles_half * h + SCAN_TILES * c + n for n in range(SCAN_TILES)]
            im_t = [t + tiles_half for t in re_t]
            a_t = [tiles_half * h + SCAN_TILES * c + n for n in range(SCAN_TILES)]
            ars = [jnp.broadcast_to(ar_ref[:, LANES * t:LANES * (t + 1)], (bn, LANES)) for t in a_t]
            ais = [jnp.broadcast_to(ai_ref[:, LANES * t:LANES * (t + 1)], (bn, LANES)) for t in a_t]

            def body(t, carry, re_t=re_t, im_t=im_t, ars=ars, ais=ais):
                rows = pl.ds(pl.multiple_of(t * bn, bn), bn)
                out = []
                for n in range(SCAN_TILES):
                    xr, xi = carry[2 * n], carry[2 * n + 1]
                    nxr = ars[n] * xr - ais[n] * xi + s_ref[re_t[n], rows, :]
                    nxi = ars[n] * xi + ais[n] * xr + s_ref[im_t[n], rows, :]
                    s_ref[re_t[n], rows, :] = nxr
                    s_ref[im_t[n], rows, :] = nxi
                    out += [nxr, nxi]
                return tuple(out)

            x0 = []
            for n in range(SCAN_TILES):
                x0 += [carry_ref[:, LANES * re_t[n]:LANES * (re_t[n] + 1)],
                       carry_ref[:, LANES * im_t[n]:LANES * (im_t[n] + 1)]]
            xs = lax.fori_loop(0, lc, body, tuple(x0), unroll=min(lc, 8))
            for n in range(SCAN_TILES):
                carry_ref[:, LANES * re_t[n]:LANES * (re_t[n] + 1)] = xs[2 * n]
                carry_ref[:, LANES * im_t[n]:LANES * (im_t[n] + 1)] = xs[2 * n + 1]
    ht_ref[...] = carry_ref[...]

    ys = []
    for h in range(2):
        sdt = F32 if precise else BF16
        xh = jnp.concatenate([s_ref[2 * tiles_half * h + n].astype(sdt) for n in range(2 * tiles_half)], axis=1)
        ys.append(_mm(xh, cc_ref[h], precise))
    y = jnp.concatenate(ys, axis=1)
    if lc > 1:
        for n in range(D_SSM // LANES):
            yt_ref[n] = y[:, LANES * n:LANES * (n + 1)]
        y = jnp.concatenate(
            [jnp.concatenate([yt_ref[n, pl.ds(b, lc, stride=bn), :] for n in range(D_SSM // LANES)], axis=1)
             for b in range(bn)], axis=0)
    y = y + d_ref[...] * u
    z = jax.nn.gelu(y)
    s5 = z * jax.nn.sigmoid(_mm(z, wglu_ref[...], precise))
    pb = _mm(s5, wpb_ref[...], precise)
    gb = gb_ref[...].reshape(m, D_MODEL)
    sb_ref[...] = (jax.nn.sigmoid(gb) * pb).astype(sb_ref.dtype).reshape(sb_ref.shape)


def _s5(u3, gb3, h0, ar, ai, bb, cc, d, wglu, wpb, bn, lc, precise):
    nb, s, _ = u3.shape
    rows = bn * lc // nb
    nchunk = s // rows
    full = lambda a: pl.BlockSpec(a.shape, lambda c: (0,) * a.ndim)
    blk = lambda n: pl.BlockSpec((nb, rows, n), lambda c: (0, c, 0))
    perms = []
    if lc > 1:
        r = jnp.arange(bn * lc)
        perms = [(((r % bn) * lc + r // bn)[:, None] == r[None, :]).astype(BF16)]
    return pl.pallas_call(
        functools.partial(_s5_kernel, bn=bn, lc=lc, precise=precise),
        grid=(nchunk,),
        in_specs=[blk(D_SSM), blk(D_MODEL), full(h0), full(ar), full(ai), full(bb), full(cc), full(d),
                  full(wglu), full(wpb)] + [full(p) for p in perms],
        out_specs=[blk(D_MODEL), full(h0)],
        out_shape=[jax.ShapeDtypeStruct((nb, s, D_MODEL), F32 if precise else BF16),
                   jax.ShapeDtypeStruct(h0.shape, F32)],
        scratch_shapes=[pltpu.VMEM((2 * N_STATE // LANES, bn * lc, LANES), F32),
                        pltpu.VMEM((D_SSM // LANES, bn * lc, LANES), F32),
                        pltpu.VMEM((bn, 2 * N_STATE), F32)],
        compiler_params=_cparams("arbitrary"),
        name="s5",
    )(u3, gb3, h0, ar, ai, bb, cc, d, wglu, wpb, *perms)


def _s5_weights(ar, ai, bbt_re, bbt_im, c_re, c_im):
    g, p, h = N_SSM_GROUPS, SSM_STATE, SSM_GROUP
    eye = jnp.eye(g, dtype=F32)

    def bfull(bt):
        b = bt.reshape(h, g, p)
        return jnp.einsum('hgp,gk->ghkp', b, eye).reshape(g * h, g * p)

    bre, bim = bfull(bbt_re), bfull(bbt_im)
    tiles = []
    for j in range(16):
        sec, jj = divmod(j, 4)
        src = bre if sec % 2 == 0 else bim
        gbase = (sec // 2) * 16 + jj * 4
        lt = (gbase * h) // LANES
        tiles.append(src[LANES * lt:LANES * (lt + 1), gbase * p:(gbase + 4) * p])
    bb = jnp.stack(tiles)

    def cfull(c):
        return jnp.einsum('ghp,gk->gpkh', c, eye).reshape(g * p, g * h)

    cre, cim = cfull(c_re), cfull(c_im)
    halves = []
    for hh in range(2):
        rs = slice(HALF_STATE * hh, HALF_STATE * (hh + 1))
        cs = slice(256 * hh, 256 * (hh + 1))
        halves.append(jnp.concatenate([cre[rs, cs], -cim[rs, cs]], axis=0))
    return bb, jnp.stack(halves)


def _state_to_lanes(re, im):
    b = re.shape[0]
    r = re.reshape(b, 2, HALF_STATE)
    i = im.reshape(b, 2, HALF_STATE)
    return jnp.concatenate([r[:, 0], i[:, 0], r[:, 1], i[:, 1]], axis=1)


def _lanes_to_state(h):
    b = h.shape[0]
    h4 = h.reshape(b, 4, HALF_STATE)
    re = jnp.concatenate([h4[:, 0], h4[:, 2]], axis=1).reshape(b, N_SSM_GROUPS, SSM_STATE)
    im = jnp.concatenate([h4[:, 1], h4[:, 3]], axis=1).reshape(b, N_SSM_GROUPS, SSM_STATE)
    return re, im


MOBA_CHUNK = 4


def _moba_kernel(q_ref, k_ref, v_ref, o_ref, ka0_ref, ka1_ref, selb_ref, *, seq):
    blk, hd = MOBA_BLOCK, HEAD_DIM_A
    nblk = seq // blk
    chunk_rows = MOBA_CHUNK * blk

    k = k_ref[...]
    rblk = lax.broadcasted_iota(I32, (seq, LANES), 0) >> BLOCK_SHIFT
    lane = lax.broadcasted_iota(I32, (seq, LANES), 1)
    kf = k.astype(F32)
    ka0_ref[...] = jnp.where(lane < hd, kf, jnp.where(lane - hd == rblk, 1.0, 0.0)).astype(BF16)
    ka1_ref[...] = jnp.where(lane >= hd, kf, jnp.where(lane == rblk, 1.0, 0.0)).astype(BF16)
    r = lax.broadcasted_iota(I32, (LANES, seq), 0)
    cblk = lax.broadcasted_iota(I32, (LANES, seq), 1) >> BLOCK_SHIFT
    ind = jnp.where((r == cblk) | (r - hd == cblk), 1.0, 0.0).astype(BF16)
    kmean = _dot(ind, k) * (1.0 / blk)
    rr = lax.broadcasted_iota(I32, (LANES, LANES), 0)
    ll = lax.broadcasted_iota(I32, (LANES, LANES), 1)
    keep = ((rr < hd) & (ll >= hd)) | ((rr >= hd) & (ll < hd))
    ahi, alo = _split_bf16(jnp.where(keep, kmean, 0.0))

    qhi, qlo = _split_bf16(q_ref[...])
    sc = _dot_nt(qhi, ahi) + _dot_nt(qlo, ahi) + _dot_nt(qhi, alo)
    sct = sc.T
    blk_i = lax.broadcasted_iota(I32, (nblk, seq), 0)
    blk_f = blk_i.astype(F32)
    own = lax.broadcasted_iota(I32, (nblk, seq), 1) >> BLOCK_SHIFT
    valid = blk_i < own
    biases = []
    for x in (sct[0:nblk], sct[hd:hd + nblk]):
        taken = jnp.zeros((nblk, seq), jnp.bool_)
        for _ in range(MOBA_TOPK):
            sm = jnp.where(valid & jnp.logical_not(taken), x, BIG_NEG)
            mx = jnp.max(sm, axis=0, keepdims=True)
            idx = jnp.min(jnp.where((sm == mx) & (sm > BIG_NEG), blk_f, NO_IDX), axis=0, keepdims=True)
            taken = taken | (blk_f == idx)
        biases.append(jnp.where(taken | (blk_i == own), 0.0, NEG_INF))
    pad = jnp.zeros((hd - nblk, seq), F32)
    selb_ref[...] = jnp.concatenate([biases[0], pad, biases[1], pad], axis=0).T.astype(BF16)

    lane_b = lax.broadcasted_iota(I32, (blk, LANES), 1)
    col_minus_row = (lax.broadcasted_iota(I32, (blk, chunk_rows), 1)
                     - lax.broadcasted_iota(I32, (blk, chunk_rows), 0))
    heads = ((ka0_ref, lane_b < hd), (ka1_ref, lane_b >= hd))
    chunk_shift = MOBA_CHUNK.bit_length() - 1

    def qblock(i, _):
        off = pl.multiple_of(i * blk, blk)
        qs = q_ref[pl.ds(off, blk), :] * (hd ** -0.5)
        sb = selb_ref[pl.ds(off, blk), :].astype(F32)
        qas = [jnp.where(mine, qs, sb).astype(BF16) for _, mine in heads]

        dc = i >> chunk_shift
        thr = (i - (dc << chunk_shift)) * blk

        def attend(n_chunks):
            def run():
                outs = []
                for (ka_ref, _), qa in zip(heads, qas):
                    ss = [_dot_nt(qa, ka_ref[c * chunk_rows:(c + 1) * chunk_rows, :]) for c in range(n_chunks)]
                    ss[-1] = jnp.where(col_minus_row <= thr, ss[-1], NEG_INF)
                    m = functools.reduce(jnp.maximum, [jnp.max(s, axis=1, keepdims=True) for s in ss])
                    ps = [jnp.exp(s - m) for s in ss]
                    l = functools.reduce(lambda a, b: a + b, [jnp.sum(p, axis=1, keepdims=True) for p in ps])
                    pv = _dot(jnp.concatenate([p.astype(BF16) for p in ps], axis=1),
                              v_ref[0:n_chunks * chunk_rows, :])
                    outs.append(pv / l)
                return jnp.where(lane_b < hd, outs[0], outs[1]).astype(BF16)
            return run

        o_ref[pl.ds(off, blk), :] = lax.switch(dc, [attend(n) for n in range(1, nblk // MOBA_CHUNK + 1)])
        return 0

    lax.fori_loop(0, nblk, qblock, 0)


def _moba_prompt(q, kb, vb, batch, seq):
    spec = pl.BlockSpec((seq, LANES), lambda b, hp: (b, hp))
    return pl.pallas_call(
        functools.partial(_moba_kernel, seq=seq),
        grid=(batch, D_ATTN // LANES),
        in_specs=[spec, spec, spec],
        out_specs=spec,
        out_shape=jax.ShapeDtypeStruct((batch * seq, D_ATTN), BF16),
        scratch_shapes=[pltpu.VMEM((seq, LANES), BF16)] * 3,
        compiler_params=_cparams("parallel", "parallel"),
        name="moba_prompt",
    )(q, kb, vb)


def _memkv_kernel(m_ref, g_ref, wk_ref, wv_ref, mk_ref, mv_ref):
    mn = _rms(m_ref[...], g_ref[...]).astype(BF16)
    mk_ref[...] = _dot(mn, wk_ref[...])
    mv_ref[...] = _dot(mn, wv_ref[...])


def _memkv(mem, g, wk, wv, tm):
    t = mem.shape[0]
    row = lambda n: pl.BlockSpec((tm, n), lambda i: (i, 0))
    full = lambda a: pl.BlockSpec(a.shape, lambda i: (0,) * a.ndim)
    shp = jax.ShapeDtypeStruct((t, D_XATTN), F32)
    return pl.pallas_call(
        _memkv_kernel, grid=(t // tm,),
        in_specs=[row(D_MODEL), full(g), full(wk), full(wv)],
        out_specs=[row(D_XATTN)] * 2, out_shape=[shp, shp],
        compiler_params=_cparams("parallel"), name="memkv",
    )(mem, g, wk, wv)


def _merge_kernel(x_ref, at_ref, sga_ref, sb_ref, wpa_ref, wo_ref, gx_ref, wxq_ref, x1_ref, xq_ref, *, precise):
    pa = _mm(at_ref[...], wpa_ref[...], precise)
    merged = sga_ref[...].astype(F32) * pa + sb_ref[...].astype(F32)
    x1 = x_ref[...] + _mm(merged, wo_ref[...], precise)
    x1_ref[...] = x1
    xq_ref[...] = _mm(_rms(x1, gx_ref[...]), wxq_ref[...], precise).astype(xq_ref.dtype)


def _merge(x, attn, sga, sb, wpa, wo, gx, wxq, tm, precise):
    t = x.shape[0]
    row = lambda n: pl.BlockSpec((tm, n), lambda i: (i, 0))
    full = lambda a: pl.BlockSpec(a.shape, lambda i: (0,) * a.ndim)
    return pl.pallas_call(
        functools.partial(_merge_kernel, precise=precise), grid=(t // tm,),
        in_specs=[row(D_MODEL), row(D_ATTN), row(D_MODEL), row(D_MODEL), full(wpa), full(wo), full(gx),
                  full(wxq)],
        out_specs=[row(D_MODEL), row(D_XATTN)],
        out_shape=[jax.ShapeDtypeStruct((t, D_MODEL), F32),
                   jax.ShapeDtypeStruct((t, D_XATTN), F32 if precise else BF16)],
        compiler_params=_cparams("parallel"), name="merge",
    )(x, attn, sga, sb, wpa, wo, gx, wxq)


def _xattn_kernel(q_ref, mk_ref, mv_ref, o_ref, *, precise):
    q = q_ref[0]
    mk = mk_ref[0]
    mv = mv_ref[0]
    outs = []
    for h in range(N_HEADS_X):
        cs = slice(HEAD_DIM_X * h, HEAD_DIM_X * (h + 1))
        s = _mm(q[:, cs], mk[:, cs], precise, nt=True) * (HEAD_DIM_X ** -0.5)
        e = jnp.exp(s - jnp.max(s, axis=1, keepdims=True))
        p = e / jnp.sum(e, axis=1, keepdims=True)
        outs.append(_mm(p, mv[:, cs], precise))
    o_ref[0] = jnp.concatenate(outs, axis=1).astype(o_ref.dtype)


def _xattn(q3, mk3, mv3, tq, precise):
    b, s, _ = q3.shape
    qspec = pl.BlockSpec((1, tq, D_XATTN), lambda bi, si: (bi, si, 0))
    mspec = pl.BlockSpec((1, N_MEM, D_XATTN), lambda bi, si: (bi, 0, 0))
    return pl.pallas_call(
        functools.partial(_xattn_kernel, precise=precise), grid=(b, s // tq),
        in_specs=[qspec, mspec, mspec], out_specs=qspec,
        out_shape=jax.ShapeDtypeStruct(q3.shape, q3.dtype),
        compiler_params=_cparams("parallel", "parallel"), name="xattn",
    )(q3, mk3, mv3)


GROUP_LANE0 = N_EXPERTS


def _post_kernel(x1_ref, xo_ref, wxo_ref, gf_ref, wr_ref, br_ref, x2_ref, xn_ref, cmb_ref, *, precise):
    x2 = x1_ref[...] + _mm(xo_ref[...], wxo_ref[...], precise)
    x2_ref[...] = x2
    t = _rms(x2, gf_ref[...])
    thi, tlo = _split_bf16(t)
    xn_ref[...] = thi
    if precise:
        logits = _mm(t, wr_ref[...], True) + br_ref[...]
    else:
        wrh, wrl = _split_bf16(wr_ref[...])
        logits = _dot(thi, wrh) + _dot(tlo, wrh) + _dot(thi, wrl) + br_ref[...]

    lane_i = lax.broadcasted_iota(I32, logits.shape, 1)
    lane = lane_i.astype(F32)
    lane_group = (lane_i >> (EXPERTS_PER_GROUP.bit_length() - 1)).astype(F32)
    isg = (lane_i >= GROUP_LANE0) & (lane_i < GROUP_LANE0 + N_EXPERT_GROUPS)
    gmax = jnp.max(jnp.where(isg, logits, BIG_NEG), axis=1, keepdims=True)
    eg = jnp.where(isg, jnp.exp(jnp.where(isg, logits, gmax) - gmax), 0.0)
    gp = eg / jnp.sum(eg, axis=1, keepdims=True)
    pg = jnp.max(jnp.where(isg, gp, -1.0), axis=1, keepdims=True)
    gi = jnp.min(jnp.where(isg & (gp == pg), lane, NO_IDX), axis=1, keepdims=True) - GROUP_LANE0

    insel = (lane_i < N_EXPERTS) & (lane_group == gi)
    el = jnp.where(insel, logits, BIG_NEG)
    m1 = jnp.max(el, axis=1, keepdims=True)
    i1 = jnp.min(jnp.where(insel & (el == m1), lane, NO_IDX), axis=1, keepdims=True)
    rest = insel & (lane != i1)
    el2 = jnp.where(rest, logits, BIG_NEG)
    m2 = jnp.max(el2, axis=1, keepdims=True)
    i2 = jnp.min(jnp.where(rest & (el2 == m2), lane, NO_IDX), axis=1, keepdims=True)
    e2 = jnp.exp(m2 - m1)
    den = 1.0 + e2
    cmb_ref[...] = jnp.where(lane == i1, (1.0 / den) * pg, jnp.where(lane == i2, (e2 / den) * pg, 0.0))


def _post(x1, xo, wxo, gf, wr, br, tm, precise):
    t = x1.shape[0]
    row = lambda n: pl.BlockSpec((tm, n), lambda i: (i, 0))
    full = lambda a: pl.BlockSpec(a.shape, lambda i: (0,) * a.ndim)
    return pl.pallas_call(
        functools.partial(_post_kernel, precise=precise), grid=(t // tm,),
        in_specs=[row(D_MODEL), row(D_XATTN), full(wxo), full(gf), full(wr), full(br)],
        out_specs=[row(D_MODEL), row(D_MODEL), row(LANES)],
        out_shape=[jax.ShapeDtypeStruct((t, D_MODEL), F32), jax.ShapeDtypeStruct((t, D_MODEL), BF16),
                   jax.ShapeDtypeStruct((t, LANES), F32)],
        compiler_params=_cparams("parallel"), name="post",
    )(x1, xo, wxo, gf, wr, br)


EXPERTS_PER_STEP = 2
PAGES_PER_STEP = 32
PAGES_PER_BLOCK = MOBA_BLOCK // PAGE_SIZE
BLOCKS_PER_STEP = PAGES_PER_STEP // PAGES_PER_BLOCK


def _block_score_step(s, pages, qb_ref, sc_ref):
    lane = lax.broadcasted_iota(I32, (N_HEADS_A, LANES), 1)

    @pl.when(s == 0)
    def _():
        sc_ref[0] = jnp.zeros((N_HEADS_A, LANES), F32)

    sc = sc_ref[0]
    qb = qb_ref[0]
    for r in range(BLOCKS_PER_STEP):
        acc = pages[PAGES_PER_BLOCK * r][0].reshape(D_ATTN, PAGE_SIZE)
        for t in range(1, PAGES_PER_BLOCK):
            acc = acc + pages[PAGES_PER_BLOCK * r + t][0].reshape(D_ATTN, PAGE_SIZE)
        prod = (acc * qb).reshape(N_HEADS_A, HEAD_DIM_A // 8, 8, PAGE_SIZE)
        per_head = jnp.sum(jnp.sum(prod, axis=1), axis=1)
        sc = jnp.where(lane == s * BLOCKS_PER_STEP + r, jnp.sum(per_head, axis=1, keepdims=True), sc)
    sc_ref[0] = sc


def _moe_stream_kernel(pt_ref, x2_ref, xn_ref, cmb_ref, w13_ref, w2_ref, gfin_ref, qb_ref, *refs,
                       steps_per_seq):
    del pt_ref
    pages, (y_ref, sc_ref, acc_ref) = refs[:PAGES_PER_STEP], refs[PAGES_PER_STEP:]
    _moe_kernel(x2_ref, xn_ref, cmb_ref, w13_ref, w2_ref, gfin_ref, y_ref, acc_ref)
    g = pl.program_id(0) * pl.num_programs(1) + pl.program_id(1)
    _block_score_step(lax.rem(g, steps_per_seq), pages, qb_ref, sc_ref)


def _moe_kernel(x2_ref, xn_ref, cmb_ref, w13_ref, w2_ref, gfin_ref, y_ref, acc_ref):
    step = pl.program_id(1)

    @pl.when(step == 0)
    def _():
        acc_ref[...] = jnp.zeros_like(acc_ref)

    xn = xn_ref[...]
    cmb = cmb_ref[...]
    lane = lax.broadcasted_iota(I32, cmb.shape, 1)
    hds = []
    for k in range(EXPERTS_PER_STEP):
        h = _dot(xn, w13_ref[k])
        cw = jnp.sum(jnp.where(lane == step * EXPERTS_PER_STEP + k, cmb, 0.0), axis=1, keepdims=True)
        hds.append((jax.nn.silu(h[:, :D_FF_EXPERT]) * h[:, D_FF_EXPERT:] * cw).astype(BF16))
    w2 = w2_ref[...].reshape(EXPERTS_PER_STEP * D_FF_EXPERT, D_MODEL)
    acc_ref[...] += _dot(jnp.concatenate(hds, axis=1), w2)

    @pl.when(step == pl.num_programs(1) - 1)
    def _():
        y_ref[...] = _rms(x2_ref[...] + acc_ref[...], gfin_ref[...])


def _moe(x2, xn, cmb, w13, w2, gfin, tm):
    t = x2.shape[0]
    row = lambda n: pl.BlockSpec((tm, n), lambda i, e: (i, 0))
    return pl.pallas_call(
        _moe_kernel, grid=(t // tm, N_EXPERTS // EXPERTS_PER_STEP),
        in_specs=[row(D_MODEL), row(D_MODEL), row(LANES),
                  pl.BlockSpec((EXPERTS_PER_STEP, D_MODEL, 2 * D_FF_EXPERT), lambda i, e: (e, 0, 0)),
                  pl.BlockSpec((EXPERTS_PER_STEP, D_FF_EXPERT, D_MODEL), lambda i, e: (e, 0, 0)),
                  pl.BlockSpec(gfin.shape, lambda i, e: (0, 0))],
        out_specs=row(D_MODEL),
        out_shape=jax.ShapeDtypeStruct((t, D_MODEL), F32),
        scratch_shapes=[pltpu.VMEM((tm, D_MODEL), F32)],
        compiler_params=_cparams("parallel", "arbitrary"), name="moe",
    )(x2, xn, cmb, w13, w2, gfin)


def _moe_with_key_sums(x2, xn, cmb, w13, w2, gfin, tm, cache4, page_table, qb):
    t = x2.shape[0]
    nb, npages = page_table.shape
    steps_per_seq = npages // PAGES_PER_STEP
    n_tiles = t // tm
    n_esteps = N_EXPERTS // EXPERTS_PER_STEP
    assert n_tiles * n_esteps == nb * steps_per_seq and npages // PAGES_PER_BLOCK <= LANES

    def seq_of(i, e):
        return (i * n_esteps + e) // steps_per_seq

    def pspec(r):
        def imap(i, e, pt):
            g = i * n_esteps + e
            return (pt[g // steps_per_seq, lax.rem(g, steps_per_seq) * PAGES_PER_STEP + r], 0, 0, 0)
        return pl.BlockSpec((1, N_HEADS_A, HEAD_DIM_A, PAGE_SIZE), imap)

    row = lambda n: pl.BlockSpec((tm, n), lambda i, e, pt: (i, 0))
    return pl.pallas_call(
        functools.partial(_moe_stream_kernel, steps_per_seq=steps_per_seq),
        grid_spec=pltpu.PrefetchScalarGridSpec(
            num_scalar_prefetch=1, grid=(n_tiles, n_esteps),
            in_specs=[row(D_MODEL), row(D_MODEL), row(LANES),
                      pl.BlockSpec((EXPERTS_PER_STEP, D_MODEL, 2 * D_FF_EXPERT), lambda i, e, pt: (e, 0, 0)),
                      pl.BlockSpec((EXPERTS_PER_STEP, D_FF_EXPERT, D_MODEL), lambda i, e, pt: (e, 0, 0)),
                      pl.BlockSpec(gfin.shape, lambda i, e, pt: (0, 0)),
                      pl.BlockSpec((1, D_ATTN, LANES), lambda i, e, pt: (seq_of(i, e), 0, 0))]
            + [pspec(r) for r in range(PAGES_PER_STEP)],
            out_specs=[row(D_MODEL),
                       pl.BlockSpec((1, N_HEADS_A, LANES), lambda i, e, pt: (seq_of(i, e), 0, 0))],
            scratch_shapes=[pltpu.VMEM((tm, D_MODEL), F32)]),
        out_shape=[jax.ShapeDtypeStruct((t, D_MODEL), F32), jax.ShapeDtypeStruct((nb, N_HEADS_A, LANES), F32)],
        compiler_params=_cparams("arbitrary", "arbitrary"), name="moe_keysums",
    )(page_table, x2, xn, cmb, w13, w2, gfin, qb, *([cache4] * PAGES_PER_STEP))


def _ssel_kernel(sc_ref, o_ref, *, n_past):
    sc = sc_ref[...] * (1.0 / MOBA_BLOCK)
    bl_i = lax.broadcasted_iota(I32, sc.shape, 1)
    bl = bl_i.astype(F32)
    valid = bl_i < n_past
    taken = jnp.zeros(sc.shape, jnp.bool_)
    out = jnp.zeros(sc.shape, F32)
    for r in range(MOBA_TOPK):
        sm = jnp.where(valid & jnp.logical_not(taken), sc, BIG_NEG)
        mx = jnp.max(sm, axis=1, keepdims=True)
        idx = jnp.min(jnp.where((sm == mx) & (sm > BIG_NEG), bl, NO_IDX), axis=1, keepdims=True)
        taken = taken | (bl == idx)
        out = jnp.where(bl_i == r, idx, out)
    o_ref[...] = out.astype(I32)


def _sample_select(scores, n_past):
    return pl.pallas_call(
        functools.partial(_ssel_kernel, n_past=n_past),
        out_shape=jax.ShapeDtypeStruct(scores.shape, I32), name="sample_select",
    )(scores)


PAGES_PER_HEAD = MOBA_TOPK * PAGES_PER_BLOCK
N_SEL_PAGES = N_HEADS_A * PAGES_PER_HEAD


def _sattn_kernel(sel_ref, pt_ref, q_ref, kn_ref, vn_ref, *refs):
    del sel_ref, pt_ref
    kp, vp, o_ref = refs[:N_SEL_PAGES], refs[N_SEL_PAGES:2 * N_SEL_PAGES], refs[2 * N_SEL_PAGES]
    for h in range(N_HEADS_A):
        qc = q_ref[0][:, h:h + 1] * (HEAD_DIM_A ** -0.5)
        kts = [kp[h * PAGES_PER_HEAD + r][0, 0] for r in range(PAGES_PER_HEAD)]
        vts = [vp[h * PAGES_PER_HEAD + r][0, 0] for r in range(PAGES_PER_HEAD)]
        ss = [jnp.sum(kt * qc, axis=0, keepdims=True) for kt in kts]
        s_self = jnp.sum(qc * kn_ref[0][:, h:h + 1], axis=0, keepdims=True)
        mx = s_self
        for s in ss:
            mx = jnp.maximum(mx, jnp.max(s, axis=1, keepdims=True))
        p_self = jnp.exp(s_self - mx)
        den = p_self
        acc = jnp.zeros((HEAD_DIM_A, PAGE_SIZE), F32)
        for s, vt in zip(ss, vts):
            p = jnp.exp(s - mx)
            den = den + jnp.sum(p, axis=1, keepdims=True)
            acc = acc + vt * p
        out = p_self * vn_ref[0][:, h:h + 1] + jnp.sum(acc, axis=1, keepdims=True)
        o_ref[0, :, h:h + 1] = out / den


def _sample_attn(sel_flat, pt_flat, q3, kn3, vn3, ck4, cv4, n_pages):
    nb = q3.shape[0]

    def pspec(slot):
        h, rem = divmod(slot, PAGES_PER_HEAD)
        r, half = divmod(rem, PAGES_PER_BLOCK)

        def imap(b, sel, pt):
            blk = sel[(b * N_HEADS_A + h) * MOBA_TOPK + r]
            return (pt[b * n_pages + blk * PAGES_PER_BLOCK + half], h, 0, 0)

        return pl.BlockSpec((1, 1, HEAD_DIM_A, PAGE_SIZE), imap)

    tok = pl.BlockSpec((1, HEAD_DIM_A, N_HEADS_A), lambda b, sel, pt: (b, 0, 0))
    return pl.pallas_call(
        _sattn_kernel,
        grid_spec=pltpu.PrefetchScalarGridSpec(
            num_scalar_prefetch=2, grid=(nb,),
            in_specs=[tok, tok, tok] + [pspec(s) for s in range(N_SEL_PAGES)] * 2,
            out_specs=tok),
        out_shape=jax.ShapeDtypeStruct((nb, HEAD_DIM_A, N_HEADS_A), F32),
        compiler_params=_cparams("arbitrary"), name="sample_attn",
    )(sel_flat, pt_flat, q3, kn3, vn3, *([ck4] * N_SEL_PAGES), *([cv4] * N_SEL_PAGES))


def _tail(x, attn, sga, sb, mk3, mv3, w, batch, seq, tm, tq, precise, key_stream=None):
    x1, xq = _merge(x, attn, sga, sb, w['wpa'], w['wo'], w['gx'], w['wxq'], tm, precise)
    xq3 = xq.reshape(batch, seq, D_XATTN)
    pad = (-seq) % tq
    if pad:
        xq3 = jnp.pad(xq3, ((0, 0), (0, pad), (0, 0)))
    xo = _xattn(xq3, mk3, mv3, tq, precise)[:, :seq].reshape(batch * seq, D_XATTN)
    x2, xn, cmb = _post(x1, xo, w['wxo'], w['gf'], w['wr'], w['br'], tm, precise)
    if key_stream is None:
        return _moe(x2, xn, cmb, w['w13'], w['w2'], w['gfin'], tm)
    return _moe_with_key_sums(x2, xn, cmb, w['w13'], w['w2'], w['gfin'], tm, *key_stream)


def kernel(x_prompt, x_sample, mem_prompt, cache_k, cache_v, page_table, state_ssm_re, state_ssm_im,
           cache_mem_k, cache_mem_v, g_mix, w_in, ssm_lambda_re, ssm_lambda_im, ssm_log_dt,
           ssm_b_re, ssm_b_im, ssm_c_re, ssm_c_im, ssm_d, w_glu, w_pa, w_pb, w_o, g_x, g_mem,
           w_xq, w_xk, w_xv, w_xo, g_ffn, w_group, b_group, w_erouter, b_erouter, w1, w3, w2, g_final):
    depth = w_in.shape[0]
    assert depth == 1
    l = 0
    bp, sp, _ = x_prompt.shape
    bs, ss, _ = x_sample.shape
    assert ss == 1
    n_pages = page_table.shape[1]
    past_len = n_pages * PAGE_SIZE
    assert past_len % MOBA_BLOCK == 0 and sp % MOBA_BLOCK == 0

    row = lambda a: a.reshape(1, -1).astype(F32)
    bf = lambda a: a.astype(BF16)
    wr = jnp.zeros((D_MODEL, LANES), F32)
    wr = wr.at[:, :N_EXPERTS].set(w_erouter[l]).at[:, GROUP_LANE0:GROUP_LANE0 + N_EXPERT_GROUPS].set(w_group[l])
    br = jnp.zeros((1, LANES), F32)
    br = br.at[0, :N_EXPERTS].set(b_erouter[l]).at[0, GROUP_LANE0:GROUP_LANE0 + N_EXPERT_GROUPS].set(b_group[l])
    ws = dict(wpa=w_pa[l], wo=w_o[l], gx=row(g_x[l]), wxq=w_xq[l], wxo=w_xo[l], gf=row(g_ffn[l]),
              wr=wr, br=br, w13=bf(jnp.concatenate([w1[l], w3[l]], axis=-1)), w2=bf(w2[l]),
              gfin=row(g_final))
    wp = dict(ws, wpa=bf(w_pa[l]), wo=bf(w_o[l]), wxq=bf(w_xq[l]), wxo=bf(w_xo[l]))
    gmix = row(g_mix[l])

    ar, ai, bbt_re, bbt_im = _ssm_params(ssm_lambda_re[l], ssm_lambda_im[l], ssm_log_dt[l], ssm_b_re[l],
                                         ssm_b_im[l])
    bb, cc = _s5_weights(ar, ai, bbt_re, bbt_im, ssm_c_re[l], ssm_c_im[l])
    d_row = row(ssm_d[l])

    tp = bp * sp
    xp = x_prompt.reshape(tp, D_MODEL)
    q, kt, vt, kb, vb, u, sga, gb = _inproj(xp, gmix, bf(w_in[l]), 512, bp)
    attn = _moba_prompt(q, kb, vb, bp, sp)
    h0 = jnp.zeros((bp, 2 * N_STATE), F32)
    sb3, ht = _s5(u.reshape(bp, sp, D_SSM), gb.reshape(bp, sp, D_MODEL), h0, ar, ai, bf(bb), bf(cc), d_row,
                  bf(w_glu[l]), bf(w_pb[l]), bn=bp, lc=128, precise=False)
    sr_p, si_p = _lanes_to_state(ht)
    mk_p, mv_p = _memkv(mem_prompt.reshape(bp * N_MEM, D_MODEL), row(g_mem[l]), bf(w_xk[l]), bf(w_xv[l]), 256)
    ck4 = jnp.transpose(cache_k[l], (0, 2, 3, 1))
    cv4 = jnp.transpose(cache_v[l], (0, 2, 3, 1))
    xs = x_sample.reshape(bs, D_MODEL)
    proj_s = _inproj_precise(xs, gmix, w_in[l])
    q_s, k_s, v_s = (proj_s[:, D_ATTN * n:D_ATTN * (n + 1)] for n in range(3))
    u_s = proj_s[:, 3 * D_ATTN:GA_COL0]
    sga_s = proj_s[:, GA_COL0:GA_COL0 + D_MODEL]
    gb_s = proj_s[:, GA_COL0 + D_MODEL:]
    qb = jnp.broadcast_to(q_s[:, :, None], (bs, D_ATTN, LANES))
    y_p, scores = _tail(xp, attn, sga, sb3.reshape(tp, D_MODEL), mk_p.reshape(bp, N_MEM, D_XATTN),
                        mv_p.reshape(bp, N_MEM, D_XATTN), wp, bp, sp, 1024, 512, False,
                        key_stream=(ck4, page_table, qb))

    sel = _sample_select(scores.reshape(bs * N_HEADS_A, LANES), past_len // MOBA_BLOCK)
    sel_flat = sel[:, :MOBA_TOPK].reshape(-1)
    hsplit = lambda a: a.reshape(bs, N_HEADS_A, HEAD_DIM_A).transpose(0, 2, 1)
    attn_s = _sample_attn(sel_flat, page_table.reshape(-1), hsplit(q_s), hsplit(k_s), hsplit(v_s), ck4, cv4,
                          n_pages).transpose(0, 2, 1)
    h0_s = _state_to_lanes(state_ssm_re[l].reshape(bs, N_STATE), state_ssm_im[l].reshape(bs, N_STATE))
    sb_s, ht_s = _s5(u_s.reshape(1, bs, D_SSM), gb_s.reshape(1, bs, D_MODEL), h0_s, ar, ai, bb, cc, d_row,
                     w_glu[l], w_pb[l], bn=bs, lc=1, precise=True)
    sr_s, si_s = _lanes_to_state(ht_s)
    y_s = _tail(xs, attn_s.reshape(bs, D_ATTN), sga_s, sb_s.reshape(bs, D_MODEL),
                cache_mem_k[l].reshape(bs, N_MEM, D_XATTN), cache_mem_v[l].reshape(bs, N_MEM, D_XATTN),
                ws, bs, 1, bs, 16, True)

    kv5 = lambda a, b, s: a.reshape(1, b, s, N_HEADS_A, HEAD_DIM_A)
    kvt5 = lambda a: a.reshape(bp, N_HEADS_A, HEAD_DIM_A, sp).transpose(0, 3, 1, 2)[None]
    st4 = lambda a: a[None]
    mem5 = lambda a: a.reshape(1, bp, N_MEM, N_HEADS_X, HEAD_DIM_X)
    return (y_p.reshape(bp, sp, D_MODEL), y_s.reshape(bs, 1, D_MODEL),
            kvt5(kt), kvt5(vt), st4(sr_p), st4(si_p), mem5(mk_p), mem5(mv_p),
            kv5(k_s, bs, 1), kv5(v_s, bs, 1), st4(sr_s), st4(si_s))
```

```python
import functools

import jax
import jax.numpy as jnp
from jax import lax
from jax.experimental import pallas as pl
from jax.experimental.pallas import tpu as pltpu

F32 = jnp.float32
BF16 = jnp.bfloat16
I32 = jnp.int32

D_MODEL = 1024
N_HEADS_A = 8
HEAD_DIM_A = 64
D_ATTN = N_HEADS_A * HEAD_DIM_A
MOBA_BLOCK = 256
BLOCK_SHIFT = MOBA_BLOCK.bit_length() - 1
MOBA_TOPK = 3
D_SSM = 512
SSM_GROUP = 16
N_SSM_GROUPS = D_SSM // SSM_GROUP
SSM_STATE = 64
N_STATE = N_SSM_GROUPS * SSM_STATE
N_MEM = 256
N_HEADS_X = 4
HEAD_DIM_X = 128
D_XATTN = N_HEADS_X * HEAD_DIM_X
N_EXPERT_GROUPS = 4
EXPERTS_PER_GROUP = 4
N_EXPERTS = N_EXPERT_GROUPS * EXPERTS_PER_GROUP
D_FF_EXPERT = 256
D_IN_PROJ = 3 * D_ATTN + D_SSM + 2 * D_MODEL
RMS_EPS = 1e-6
NEG_INF = -1e30
PAGE_SIZE = 128

LANES = 128
VMEM_LIMIT = 56 * 1024 * 1024
BIG_NEG = -3e38
M_INIT = -1e29
NO_IDX = 1e9


def _cparams(*sem):
    return pltpu.CompilerParams(dimension_semantics=sem, vmem_limit_bytes=VMEM_LIMIT)


def _rms(x, g):
    return x * lax.rsqrt(jnp.mean(x * x, axis=-1, keepdims=True) + RMS_EPS) * g


def _dot(a, b):
    return jnp.dot(a, b, preferred_element_type=F32)


def _dot_nt(a, b):
    return lax.dot_general(a, b, (((1,), (1,)), ((), ())), preferred_element_type=F32)


def _split_bf16(x):
    hi = x.astype(BF16)
    return hi, (x - hi.astype(F32)).astype(BF16)


def _split3_bf16(x):
    hi, rest = x.astype(BF16), None
    rest = x - hi.astype(F32)
    mid = rest.astype(BF16)
    return hi, mid, (rest - mid.astype(F32)).astype(BF16)


def _mm(a, w, precise, nt=False):
    dot = _dot_nt if nt else _dot
    if not precise:
        return dot(a.astype(BF16), w.astype(BF16))
    ah, am, al = _split3_bf16(a.astype(F32))
    wh, wm, wl = _split3_bf16(w)
    return ((dot(al, wh) + dot(ah, wl)) + (dot(am, wm) + dot(am, wh) + dot(ah, wm))) + dot(ah, wh)


def _inproj_kernel(x_ref, g_ref, w_ref, q_ref, k_ref, v_ref, kb_ref, vb_ref, u_ref, sga_ref, gb_ref):
    xn = _rms(x_ref[...], g_ref[...]).astype(BF16)

    def mm(c0, n):
        return _dot(xn, w_ref[:, c0:c0 + n])

    q_ref[...] = mm(0, D_ATTN)
    k = mm(D_ATTN, D_ATTN)
    k_ref[0] = k.T
    kb_ref[...] = k.astype(BF16)
    v = mm(2 * D_ATTN, D_ATTN)
    v_ref[0] = v.T
    vb_ref[...] = v.astype(BF16)
    u_ref[...] = mm(3 * D_ATTN, D_SSM)
    sga_ref[...] = jax.nn.sigmoid(mm(3 * D_ATTN + D_SSM, D_MODEL)).astype(BF16)
    gb_ref[...] = mm(3 * D_ATTN + D_SSM + D_MODEL, D_MODEL)


def _inproj(x, g, w_bf, tm, batch):
    t = x.shape[0]
    seq = t // batch
    tiles = seq // tm
    row = lambda n: pl.BlockSpec((tm, n), lambda i: (i, 0))
    col = pl.BlockSpec((1, D_ATTN, tm), lambda i: (i // tiles, 0, i % tiles))
    full = lambda a: pl.BlockSpec(a.shape, lambda i: (0,) * a.ndim)
    shp = lambda n, dt: jax.ShapeDtypeStruct((t, n), dt)
    tshp = jax.ShapeDtypeStruct((batch, D_ATTN, seq), F32)
    return pl.pallas_call(
        _inproj_kernel,
        grid=(t // tm,),
        in_specs=[row(D_MODEL), full(g), full(w_bf)],
        out_specs=[row(D_ATTN), col, col, row(D_ATTN), row(D_ATTN), row(D_SSM), row(D_MODEL), row(D_MODEL)],
        out_shape=[shp(D_ATTN, F32), tshp, tshp, shp(D_ATTN, BF16),
                   shp(D_ATTN, BF16), shp(D_SSM, F32), shp(D_MODEL, BF16), shp(D_MODEL, F32)],
        compiler_params=_cparams("parallel"),
        name="inproj",
    )(x, g, w_bf)


INPROJ_COLS = 512
GA_COL0 = 3 * D_ATTN + D_SSM


def _inproj_precise_kernel(x_ref, g_ref, w_ref, o_ref):
    c0 = pl.program_id(0) * INPROJ_COLS
    acc = _mm(_rms(x_ref[...], g_ref[...]), w_ref[...], True)
    is_ga = (c0 >= GA_COL0) & (c0 < GA_COL0 + D_MODEL)

    @pl.when(is_ga)
    def _():
        o_ref[...] = jax.nn.sigmoid(acc)

    @pl.when(jnp.logical_not(is_ga))
    def _():
        o_ref[...] = acc


def _inproj_precise(x, g, w):
    t = x.shape[0]
    full = lambda a: pl.BlockSpec(a.shape, lambda j: (0,) * a.ndim)
    return pl.pallas_call(
        _inproj_precise_kernel,
        grid=(D_IN_PROJ // INPROJ_COLS,),
        in_specs=[full(x), full(g), pl.BlockSpec((D_MODEL, INPROJ_COLS), lambda j: (0, j))],
        out_specs=pl.BlockSpec((t, INPROJ_COLS), lambda j: (0, j)),
        out_shape=jax.ShapeDtypeStruct((t, D_IN_PROJ), F32),
        compiler_params=_cparams("parallel"),
        name="inproj_sample",
    )(x, g, w)


def _ssm_param_kernel(lr_ref, li_ref, dt_ref, br_ref, bi_ref, ar_ref, ai_ref, bbr_ref, bbi_ref):
    lr = jnp.minimum(lr_ref[...], -1e-4)
    li = li_ref[...]
    dt = jnp.exp(dt_ref[...])
    mag = jnp.exp(lr * dt)
    ar = mag * jnp.cos(li * dt)
    ai = mag * jnp.sin(li * dt)
    den = lr * lr + li * li
    nr = ar - 1.0
    cr = (nr * lr + ai * li) / den
    ci = (ai * lr - nr * li) / den
    ar_ref[...] = ar
    ai_ref[...] = ai
    br = br_ref[...]
    bi = bi_ref[...]
    bbr_ref[...] = cr * br - ci * bi
    bbi_ref[...] = cr * bi + ci * br


def _ssm_params(lam_re, lam_im, log_dt, b_re, b_im):
    n = N_STATE
    lr = lam_re.reshape(1, n)
    li = lam_im.reshape(1, n)
    dt = jnp.broadcast_to(log_dt[:, None], (N_SSM_GROUPS, SSM_STATE)).reshape(1, n)
    brt = b_re.reshape(n, SSM_GROUP).T
    bit = b_im.reshape(n, SSM_GROUP).T
    row = jax.ShapeDtypeStruct((1, n), F32)
    mat = jax.ShapeDtypeStruct((SSM_GROUP, n), F32)
    return pl.pallas_call(_ssm_param_kernel, out_shape=[row, row, mat, mat], name="ssm_params")(
        lr, li, dt, brt, bit)


HALF_STATE = N_STATE // 2
SCAN_COLS = 512
SCAN_TILES = SCAN_COLS // 128


def _s5_kernel(u_ref, gb_ref, h0_ref, ar_ref, ai_ref, bb_ref, cc_ref, d_ref, wglu_ref, wpb_ref, *refs,
               bn, lc, precise):
    m = bn * lc
    if lc > 1:
        perm_ref, sb_ref, ht_ref, s_ref, yt_ref, carry_ref = refs
    else:
        sb_ref, ht_ref, s_ref, yt_ref, carry_ref = refs

    @pl.when(pl.program_id(0) == 0)
    def _():
        carry_ref[...] = h0_ref[...]

    u = u_ref[...].reshape(m, D_SSM)
    us = _dot(perm_ref[...], u.astype(BF16)).astype(BF16) if lc > 1 else u
    for j in range(16):
        sec, jj = divmod(j, 4)
        gbase = (sec // 2) * 16 + jj * 4
        lt = (gbase * SSM_GROUP) // LANES
        bu = _mm(us[:, LANES * lt:LANES * (lt + 1)], bb_ref[j], precise)
        s_ref[2 * j] = bu[:, :LANES]
        s_ref[2 * j + 1] = bu[:, LANES:]

    tiles_half = HALF_STATE // LANES
    for h in range(2):
        for c in range(HALF_STATE // SCAN_COLS):
            re_t = [2 * tiles_half * h + SCAN_TILES * c + n for n in range(SCAN_TILES)]
            im_t = [t + tiles_half for t in re_t]
            a_t = [tiles_half * h + SCAN_TILES * c + n for n in range(SCAN_TILES)]
            ars = [jnp.broadcast_to(ar_ref[:, LANES * t:LANES * (t + 1)], (bn, LANES)) for t in a_t]
            ais = [jnp.broadcast_to(ai_ref[:, LANES * t:LANES * (t + 1)], (bn, LANES)) for t in a_t]

            def body(t, carry, re_t=re_t, im_t=im_t, ars=ars, ais=ais):
                rows = pl.ds(pl.multiple_of(t * bn, bn), bn)
                out = []
                for n in range(SCAN_TILES):
                    xr, xi = carry[2 * n], carry[2 * n + 1]
                    nxr = ars[n] * xr - ais[n] * xi + s_ref[re_t[n], rows, :]
                    nxi = ars[n] * xi + ais[n] * xr + s_ref[im_t[n], rows, :]
                    s_ref[re_t[n], rows, :] = nxr
                    s_ref[im_t[n], rows, :] = nxi
                    out += [nxr, nxi]
                return tuple(out)

            x0 = []
            for n in range(SCAN_TILES):
                x0 += [carry_ref[:, LANES * re_t[n]:LANES * (re_t[n] + 1)],
                       carry_ref[:, LANES * im_t[n]:LANES * (im_t[n] + 1)]]
            xs = lax.fori_loop(0, lc, body, tuple(x0), unroll=min(lc, 8))
            for n in range(SCAN_TILES):
                carry_ref[:, LANES * re_t[n]:LANES * (re_t[n] + 1)] = xs[2 * n]
                carry_ref[:, LANES * im_t[n]:LANES * (im_t[n] + 1)] = xs[2 * n + 1]
    ht_ref[...] = carry_ref[...]

    ys = []
    for h in range(2):
        sdt = F32 if precise else BF16
        xh = jnp.concatenate([s_ref[2 * tiles_half * h + n].astype(sdt) for n in range(2 * tiles_half)], axis=1)
        ys.append(_mm(xh, cc_ref[h], precise))
    y = jnp.concatenate(ys, axis=1)
    if lc > 1:
        for n in range(D_SSM // LANES):
            yt_ref[n] = y[:, LANES * n:LANES * (n + 1)]
        y = jnp.concatenate(
            [jnp.concatenate([yt_ref[n, pl.ds(b, lc, stride=bn), :] for n in range(D_SSM // LANES)], axis=1)
             for b in range(bn)], axis=0)
    y = y + d_ref[...] * u
    z = jax.nn.gelu(y)
    s5 = z * jax.nn.sigmoid(_mm(z, wglu_ref[...], precise))
    pb = _mm(s5, wpb_ref[...], precise)
    gb = gb_ref[...].reshape(m, D_MODEL)
    sb_ref[...] = (jax.nn.sigmoid(gb) * pb).astype(sb_ref.dtype).reshape(sb_ref.shape)


def _s5(u3, gb3, h0, ar, ai, bb, cc, d, wglu, wpb, bn, lc, precise):
    nb, s, _ = u3.shape
    rows = bn * lc // nb
    nchunk = s // rows
    full = lambda a: pl.BlockSpec(a.shape, lambda c: (0,) * a.ndim)
    blk = lambda n: pl.BlockSpec((nb, rows, n), lambda c: (0, c, 0))
    perms = []
    if lc > 1:
        r = jnp.arange(bn * lc)
        perms = [(((r % bn) * lc + r // bn)[:, None] == r[None, :]).astype(BF16)]
    return pl.pallas_call(
        functools.partial(_s5_kernel, bn=bn, lc=lc, precise=precise),
        grid=(nchunk,),
        in_specs=[blk(D_SSM), blk(D_MODEL), full(h0), full(ar), full(ai), full(bb), full(cc), full(d),
                  full(wglu), full(wpb)] + [full(p) for p in perms],
        out_specs=[blk(D_MODEL), full(h0)],
        out_shape=[jax.ShapeDtypeStruct((nb, s, D_MODEL), F32 if precise else BF16),
                   jax.ShapeDtypeStruct(h0.shape, F32)],
        scratch_shapes=[pltpu.VMEM((2 * N_STATE // LANES, bn * lc, LANES), F32),
                        pltpu.VMEM((D_SSM // LANES, bn * lc, LANES), F32),
                        pltpu.VMEM((bn, 2 * N_STATE), F32)],
        compiler_params=_cparams("arbitrary"),
        name="s5",
    )(u3, gb3, h0, ar, ai, bb, cc, d, wglu, wpb, *perms)


def _s5_weights(ar, ai, bbt_re, bbt_im, c_re, c_im):
    g, p, h = N_SSM_GROUPS, SSM_STATE, SSM_GROUP
    eye = jnp.eye(g, dtype=F32)

    def bfull(bt):
        b = bt.reshape(h, g, p)
        return jnp.einsum('hgp,gk->ghkp', b, eye).reshape(g * h, g * p)

    bre, bim = bfull(bbt_re), bfull(bbt_im)
    tiles = []
    for j in range(16):
        sec, jj = divmod(j, 4)
        src = bre if sec % 2 == 0 else bim
        gbase = (sec // 2) * 16 + jj * 4
        lt = (gbase * h) // LANES
        tiles.append(src[LANES * lt:LANES * (lt + 1), gbase * p:(gbase + 4) * p])
    bb = jnp.stack(tiles)

    def cfull(c):
        return jnp.einsum('ghp,gk->gpkh', c, eye).reshape(g * p, g * h)

    cre, cim = cfull(c_re), cfull(c_im)
    halves = []
    for hh in range(2):
        rs = slice(HALF_STATE * hh, HALF_STATE * (hh + 1))
        cs = slice(256 * hh, 256 * (hh + 1))
        halves.append(jnp.concatenate([cre[rs, cs], -cim[rs, cs]], axis=0))
    return bb, jnp.stack(halves)


def _state_to_lanes(re, im):
    b = re.shape[0]
    r = re.reshape(b, 2, HALF_STATE)
    i = im.reshape(b, 2, HALF_STATE)
    return jnp.concatenate([r[:, 0], i[:, 0], r[:, 1], i[:, 1]], axis=1)


def _lanes_to_state(h):
    b = h.shape[0]
    h4 = h.reshape(b, 4, HALF_STATE)
    re = jnp.concatenate([h4[:, 0], h4[:, 2]], axis=1).reshape(b, N_SSM_GROUPS, SSM_STATE)
    im = jnp.concatenate([h4[:, 1], h4[:, 3]], axis=1).reshape(b, N_SSM_GROUPS, SSM_STATE)
    return re, im


MOBA_CHUNK = 4


def _moba_kernel(q_ref, k_ref, v_ref, o_ref, ka0_ref, ka1_ref, selb_ref, *, seq):
    blk, hd = MOBA_BLOCK, HEAD_DIM_A
    nblk = seq // blk
    chunk_rows = MOBA_CHUNK * blk

    k = k_ref[...]
    rblk = lax.broadcasted_iota(I32, (seq, LANES), 0) >> BLOCK_SHIFT
    lane = lax.broadcasted_iota(I32, (seq, LANES), 1)
    kf = k.astype(F32)
    ka0_ref[...] = jnp.where(lane < hd, kf, jnp.where(lane - hd == rblk, 1.0, 0.0)).astype(BF16)
    ka1_ref[...] = jnp.where(lane >= hd, kf, jnp.where(lane == rblk, 1.0, 0.0)).astype(BF16)
    r = lax.broadcasted_iota(I32, (LANES, seq), 0)
    cblk = lax.broadcasted_iota(I32, (LANES, seq), 1) >> BLOCK_SHIFT
    ind = jnp.where((r == cblk) | (r - hd == cblk), 1.0, 0.0).astype(BF16)
    kmean = _dot(ind, k) * (1.0 / blk)
    rr = lax.broadcasted_iota(I32, (LANES, LANES), 0)
    ll = lax.broadcasted_iota(I32, (LANES, LANES), 1)
    keep = ((rr < hd) & (ll >= hd)) | ((rr >= hd) & (ll < hd))
    ahi, alo = _split_bf16(jnp.where(keep, kmean, 0.0))

    qhi, qlo = _split_bf16(q_ref[...])
    sc = _dot_nt(qhi, ahi) + _dot_nt(qlo, ahi) + _dot_nt(qhi, alo)
    sct = sc.T
    blk_i = lax.broadcasted_iota(I32, (nblk, seq), 0)
    blk_f = blk_i.astype(F32)
    own = lax.broadcasted_iota(I32, (nblk, seq), 1) >> BLOCK_SHIFT
    valid = blk_i < own
    biases = []
    for x in (sct[0:nblk], sct[hd:hd + nblk]):
        taken = jnp.zeros((nblk, seq), jnp.bool_)
        for _ in range(MOBA_TOPK):
            sm = jnp.where(valid & jnp.logical_not(taken), x, BIG_NEG)
            mx = jnp.max(sm, axis=0, keepdims=True)
            idx = jnp.min(jnp.where((sm == mx) & (sm > BIG_NEG), blk_f, NO_IDX), axis=0, keepdims=True)
            taken = taken | (blk_f == idx)
        biases.append(jnp.where(taken | (blk_i == own), 0.0, NEG_INF))
    pad = jnp.zeros((hd - nblk, seq), F32)
    selb_ref[...] = jnp.concatenate([biases[0], pad, biases[1], pad], axis=0).T.astype(BF16)

    lane_b = lax.broadcasted_iota(I32, (blk, LANES), 1)
    col_minus_row = (lax.broadcasted_iota(I32, (blk, chunk_rows), 1)
                     - lax.broadcasted_iota(I32, (blk, chunk_rows), 0))
    heads = ((ka0_ref, lane_b < hd), (ka1_ref, lane_b >= hd))
    chunk_shift = MOBA_CHUNK.bit_length() - 1

    def qblock(i, _):
        off = pl.multiple_of(i * blk, blk)
        qs = q_ref[pl.ds(off, blk), :] * (hd ** -0.5)
        sb = selb_ref[pl.ds(off, blk), :].astype(F32)
        qas = [jnp.where(mine, qs, sb).astype(BF16) for _, mine in heads]

        dc = i >> chunk_shift
        thr = (i - (dc << chunk_shift)) * blk

        def attend(n_chunks):
            def run():
                outs = []
                for (ka_ref, _), qa in zip(heads, qas):
                    ss = [_dot_nt(qa, ka_ref[c * chunk_rows:(c + 1) * chunk_rows, :]) for c in range(n_chunks)]
                    ss[-1] = jnp.where(col_minus_row <= thr, ss[-1], NEG_INF)
                    m = functools.reduce(jnp.maximum, [jnp.max(s, axis=1, keepdims=True) for s in ss])
                    ps = [jnp.exp(s - m) for s in ss]
                    l = functools.reduce(lambda a, b: a + b, [jnp.sum(p, axis=1, keepdims=True) for p in ps])
                    pv = _dot(jnp.concatenate([p.astype(BF16) for p in ps], axis=1),
                              v_ref[0:n_chunks * chunk_rows, :])
                    outs.append(pv / l)
                return jnp.where(lane_b < hd, outs[0], outs[1]).astype(BF16)
            return run

        o_ref[pl.ds(off, blk), :] = lax.switch(dc, [attend(n) for n in range(1, nblk // MOBA_CHUNK + 1)])
        return 0

    lax.fori_loop(0, nblk, qblock, 0)


def _moba_prompt(q, kb, vb, batch, seq):
    spec = pl.BlockSpec((seq, LANES), lambda b, hp: (b, hp))
    return pl.pallas_call(
        functools.partial(_moba_kernel, seq=seq),
        grid=(batch, D_ATTN // LANES),
        in_specs=[spec, spec, spec],
        out_specs=spec,
        out_shape=jax.ShapeDtypeStruct((batch * seq, D_ATTN), BF16),
        scratch_shapes=[pltpu.VMEM((seq, LANES), BF16)] * 3,
        compiler_params=_cparams("parallel", "parallel"),
        name="moba_prompt",
    )(q, kb, vb)


def _memkv_kernel(m_ref, g_ref, wk_ref, wv_ref, mk_ref, mv_ref):
    mn = _rms(m_ref[...], g_ref[...]).astype(BF16)
    mk_ref[...] = _dot(mn, wk_ref[...])
    mv_ref[...] = _dot(mn, wv_ref[...])


def _memkv(mem, g, wk, wv, tm):
    t = mem.shape[0]
    row = lambda n: pl.BlockSpec((tm, n), lambda i: (i, 0))
    full = lambda a: pl.BlockSpec(a.shape, lambda i: (0,) * a.ndim)
    shp = jax.ShapeDtypeStruct((t, D_XATTN), F32)
    return pl.pallas_call(
        _memkv_kernel, grid=(t // tm,),
        in_specs=[row(D_MODEL), full(g), full(wk), full(wv)],
        out_specs=[row(D_XATTN)] * 2, out_shape=[shp, shp],
        compiler_params=_cparams("parallel"), name="memkv",
    )(mem, g, wk, wv)


def _merge_kernel(x_ref, at_ref, sga_ref, sb_ref, wpa_ref, wo_ref, gx_ref, wxq_ref, x1_ref, xq_ref, *, precise):
    pa = _mm(at_ref[...], wpa_ref[...], precise)
    merged = sga_ref[...].astype(F32) * pa + sb_ref[...].astype(F32)
    x1 = x_ref[...] + _mm(merged, wo_ref[...], precise)
    x1_ref[...] = x1
    xq_ref[...] = _mm(_rms(x1, gx_ref[...]), wxq_ref[...], precise).astype(xq_ref.dtype)


def _merge(x, attn, sga, sb, wpa, wo, gx, wxq, tm, precise):
    t = x.shape[0]
    row = lambda n: pl.BlockSpec((tm, n), lambda i: (i, 0))
    full = lambda a: pl.BlockSpec(a.shape, lambda i: (0,) * a.ndim)
    return pl.pallas_call(
        functools.partial(_merge_kernel, precise=precise), grid=(t // tm,),
        in_specs=[row(D_MODEL), row(D_ATTN), row(D_MODEL), row(D_MODEL), full(wpa), full(wo), full(gx),
                  full(wxq)],
        out_specs=[row(D_MODEL), row(D_XATTN)],
        out_shape=[jax.ShapeDtypeStruct((t, D_MODEL), F32),
                   jax.ShapeDtypeStruct((t, D_XATTN), F32 if precise else BF16)],
        compiler_params=_cparams("parallel"), name="merge",
    )(x, attn, sga, sb, wpa, wo, gx, wxq)


def _xattn_kernel(q_ref, mk_ref, mv_ref, o_ref, *, precise):
    q = q_ref[0]
    mk = mk_ref[0]
    mv = mv_ref[0]
    outs = []
    for h in range(N_HEADS_X):
        cs = slice(HEAD_DIM_X * h, HEAD_DIM_X * (h + 1))
        s = _mm(q[:, cs], mk[:, cs], precise, nt=True) * (HEAD_DIM_X ** -0.5)
        e = jnp.exp(s - jnp.max(s, axis=1, keepdims=True))
        p = e / jnp.sum(e, axis=1, keepdims=True)
        outs.append(_mm(p, mv[:, cs], precise))
    o_ref[0] = jnp.concatenate(outs, axis=1).astype(o_ref.dtype)


def _xattn_one_kernel(q_ref, mk_ref, mv_ref, o_ref):
    q = q_ref[0]
    outs = []
    for h in range(N_HEADS_X):
        cs = slice(HEAD_DIM_X * h, HEAD_DIM_X * (h + 1))
        s = jnp.sum(mk_ref[0, :, cs] * q[:, cs], axis=1, keepdims=True) * (HEAD_DIM_X ** -0.5)
        e = jnp.exp(s - jnp.max(s, axis=0, keepdims=True))
        p = e / jnp.sum(e, axis=0, keepdims=True)
        outs.append(jnp.sum(p * mv_ref[0, :, cs], axis=0, keepdims=True))
    o_ref[0] = jnp.concatenate(outs, axis=1)


def _xattn_one(q3, mk3, mv3):
    b = q3.shape[0]
    qspec = pl.BlockSpec((1, 1, D_XATTN), lambda bi: (bi, 0, 0))
    mspec = pl.BlockSpec((1, N_MEM, D_XATTN), lambda bi: (bi, 0, 0))
    return pl.pallas_call(
        _xattn_one_kernel, grid=(b,), in_specs=[qspec, mspec, mspec], out_specs=qspec,
        out_shape=jax.ShapeDtypeStruct(q3.shape, F32),
        compiler_params=_cparams("parallel"), name="xattn_one",
    )(q3, mk3, mv3)


GROUP_LANE0 = N_EXPERTS


def _post_kernel(x1_ref, xo_ref, wxo_ref, gf_ref, wr_ref, br_ref, x2_ref, xn_ref, cmb_ref, *, precise):
    x2 = x1_ref[...] + _mm(xo_ref[...], wxo_ref[...], precise)
    x2_ref[...] = x2
    t = _rms(x2, gf_ref[...])
    thi, tlo = _split_bf16(t)
    xn_ref[...] = thi
    if precise:
        logits = _mm(t, wr_ref[...], True) + br_ref[...]
    else:
        wrh, wrl = _split_bf16(wr_ref[...])
        logits = _dot(thi, wrh) + _dot(tlo, wrh) + _dot(thi, wrl) + br_ref[...]

    lane_i = lax.broadcasted_iota(I32, logits.shape, 1)
    lane = lane_i.astype(F32)
    lane_group = (lane_i >> (EXPERTS_PER_GROUP.bit_length() - 1)).astype(F32)
    isg = (lane_i >= GROUP_LANE0) & (lane_i < GROUP_LANE0 + N_EXPERT_GROUPS)
    gmax = jnp.max(jnp.where(isg, logits, BIG_NEG), axis=1, keepdims=True)
    eg = jnp.where(isg, jnp.exp(jnp.where(isg, logits, gmax) - gmax), 0.0)
    gp = eg / jnp.sum(eg, axis=1, keepdims=True)
    pg = jnp.max(jnp.where(isg, gp, -1.0), axis=1, keepdims=True)
    gi = jnp.min(jnp.where(isg & (gp == pg), lane, NO_IDX), axis=1, keepdims=True) - GROUP_LANE0

    insel = (lane_i < N_EXPERTS) & (lane_group == gi)
    el = jnp.where(insel, logits, BIG_NEG)
    m1 = jnp.max(el, axis=1, keepdims=True)
    i1 = jnp.min(jnp.where(insel & (el == m1), lane, NO_IDX), axis=1, keepdims=True)
    rest = insel & (lane != i1)
    el2 = jnp.where(rest, logits, BIG_NEG)
    m2 = jnp.max(el2, axis=1, keepdims=True)
    i2 = jnp.min(jnp.where(rest & (el2 == m2), lane, NO_IDX), axis=1, keepdims=True)
    e2 = jnp.exp(m2 - m1)
    den = 1.0 + e2
    cmb_ref[...] = jnp.where(lane == i1, (1.0 / den) * pg, jnp.where(lane == i2, (e2 / den) * pg, 0.0))


def _mid_kernel(x_ref, at_ref, sga_ref, sb_ref, mk_ref, mv_ref, wpa_ref, wo_ref, gx_ref, wxq_ref, wxo_ref,
                gf_ref, wr_ref, br_ref, x2_ref, xn_ref, cmb_ref, x1_s, xq_s, xo_s):
    _merge_kernel(x_ref, at_ref, sga_ref, sb_ref, wpa_ref, wo_ref, gx_ref, wxq_ref, x1_s, xq_s.at[0],
                  precise=False)
    _xattn_kernel(xq_s, mk_ref, mv_ref, xo_s, precise=False)
    _post_kernel(x1_s, xo_s.at[0], wxo_ref, gf_ref, wr_ref, br_ref, x2_ref, xn_ref, cmb_ref, precise=False)


def _mid(x, attn, sga, sb, mk3, mv3, w, batch, seq, tm):
    tiles = seq // tm
    row = lambda n: pl.BlockSpec((tm, n), lambda b, s: (b * tiles + s, 0))
    full = lambda a: pl.BlockSpec(a.shape, lambda b, s: (0,) * a.ndim)
    mspec = pl.BlockSpec((1, N_MEM, D_XATTN), lambda b, s: (b, 0, 0))
    t = batch * seq
    weights = [w[k] for k in ('wpa', 'wo', 'gx', 'wxq', 'wxo', 'gf', 'wr', 'br')]
    return pl.pallas_call(
        _mid_kernel, grid=(batch, tiles),
        in_specs=[row(D_MODEL), row(D_ATTN), row(D_MODEL), row(D_MODEL), mspec, mspec]
        + [full(a) for a in weights],
        out_specs=[row(D_MODEL), row(D_MODEL), row(LANES)],
        out_shape=[jax.ShapeDtypeStruct((t, D_MODEL), F32), jax.ShapeDtypeStruct((t, D_MODEL), BF16),
                   jax.ShapeDtypeStruct((t, LANES), F32)],
        scratch_shapes=[pltpu.VMEM((tm, D_MODEL), F32), pltpu.VMEM((1, tm, D_XATTN), BF16),
                        pltpu.VMEM((1, tm, D_XATTN), BF16)],
        compiler_params=_cparams("parallel", "parallel"), name="mid",
    )(x, attn, sga, sb, mk3, mv3, *weights)


def _post(x1, xo, wxo, gf, wr, br, tm, precise):
    t = x1.shape[0]
    row = lambda n: pl.BlockSpec((tm, n), lambda i: (i, 0))
    full = lambda a: pl.BlockSpec(a.shape, lambda i: (0,) * a.ndim)
    return pl.pallas_call(
        functools.partial(_post_kernel, precise=precise), grid=(t // tm,),
        in_specs=[row(D_MODEL), row(D_XATTN), full(wxo), full(gf), full(wr), full(br)],
        out_specs=[row(D_MODEL), row(D_MODEL), row(LANES)],
        out_shape=[jax.ShapeDtypeStruct((t, D_MODEL), F32), jax.ShapeDtypeStruct((t, D_MODEL), BF16),
                   jax.ShapeDtypeStruct((t, LANES), F32)],
        compiler_params=_cparams("parallel"), name="post",
    )(x1, xo, wxo, gf, wr, br)


EXPERTS_PER_STEP = 2
PAGES_PER_STEP = 32
PAGES_PER_BLOCK = MOBA_BLOCK // PAGE_SIZE
BLOCKS_PER_STEP = PAGES_PER_STEP // PAGES_PER_BLOCK


def _block_score_step(s, pages, qb_ref, sc_ref):
    lane = lax.broadcasted_iota(I32, (N_HEADS_A, LANES), 1)
    sc = sc_ref[0]
    qb = qb_ref[0]
    for r in range(BLOCKS_PER_STEP):
        acc = pages[PAGES_PER_BLOCK * r][0].reshape(D_ATTN, PAGE_SIZE)
        for t in range(1, PAGES_PER_BLOCK):
            acc = acc + pages[PAGES_PER_BLOCK * r + t][0].reshape(D_ATTN, PAGE_SIZE)
        prod = (acc * qb).reshape(N_HEADS_A, HEAD_DIM_A // 8, 8, PAGE_SIZE)
        per_head = jnp.sum(jnp.sum(prod, axis=1), axis=1)
        sc = jnp.where(lane == s * BLOCKS_PER_STEP + r, jnp.sum(per_head, axis=1, keepdims=True), sc)
    sc_ref[0] = sc


def _moe_stream_kernel(pt_ref, x2_ref, xn_ref, cmb_ref, w13_ref, w2_ref, gfin_ref, qb_ref, *refs,
                       steps_per_seq):
    del pt_ref
    pages, (y_ref, sc_ref, acc_ref) = refs[:PAGES_PER_STEP], refs[PAGES_PER_STEP:]
    s = lax.rem(pl.program_id(0) * pl.num_programs(1) + pl.program_id(1), steps_per_seq)

    @pl.when(s == 0)
    def _():
        sc_ref[0] = jnp.zeros((N_HEADS_A, LANES), F32)

    _moe_kernel(x2_ref, xn_ref, cmb_ref, w13_ref, w2_ref, gfin_ref, y_ref, acc_ref,
                side_work=functools.partial(_block_score_step, s, pages, qb_ref, sc_ref))


def _moe_kernel(x2_ref, xn_ref, cmb_ref, w13_ref, w2_ref, gfin_ref, y_ref, acc_ref, side_work=None):
    step = pl.program_id(1)

    @pl.when(step == 0)
    def _():
        acc_ref[...] = jnp.zeros_like(acc_ref)

    if side_work is not None:
        side_work()
    xn = xn_ref[...]
    cmb = cmb_ref[...]
    lane = lax.broadcasted_iota(I32, cmb.shape, 1)
    hds = []
    for k in range(EXPERTS_PER_STEP):
        h = _dot(xn, w13_ref[k])
        cw = jnp.sum(jnp.where(lane == step * EXPERTS_PER_STEP + k, cmb, 0.0), axis=1, keepdims=True)
        hds.append((jax.nn.silu(h[:, :D_FF_EXPERT]) * h[:, D_FF_EXPERT:] * cw).astype(BF16))
    w2 = w2_ref[...].reshape(EXPERTS_PER_STEP * D_FF_EXPERT, D_MODEL)
    acc_ref[...] += _dot(jnp.concatenate(hds, axis=1), w2)

    @pl.when(step == pl.num_programs(1) - 1)
    def _():
        y_ref[...] = _rms(x2_ref[...] + acc_ref[...], gfin_ref[...])


def _moe(x2, xn, cmb, w13, w2, gfin, tm):
    t = x2.shape[0]
    row = lambda n: pl.BlockSpec((tm, n), lambda i, e: (i, 0))
    return pl.pallas_call(
        _moe_kernel, grid=(t // tm, N_EXPERTS // EXPERTS_PER_STEP),
        in_specs=[row(D_MODEL), row(D_MODEL), row(LANES),
                  pl.BlockSpec((EXPERTS_PER_STEP, D_MODEL, 2 * D_FF_EXPERT), lambda i, e: (e, 0, 0)),
                  pl.BlockSpec((EXPERTS_PER_STEP, D_FF_EXPERT, D_MODEL), lambda i, e: (e, 0, 0)),
                  pl.BlockSpec(gfin.shape, lambda i, e: (0, 0))],
        out_specs=row(D_MODEL),
        out_shape=jax.ShapeDtypeStruct((t, D_MODEL), F32),
        scratch_shapes=[pltpu.VMEM((tm, D_MODEL), F32)],
        compiler_params=_cparams("parallel", "arbitrary"), name="moe",
    )(x2, xn, cmb, w13, w2, gfin)


def _moe_with_key_sums(x2, xn, cmb, w13, w2, gfin, tm, cache4, page_table, qb):
    t = x2.shape[0]
    nb, npages = page_table.shape
    steps_per_seq = npages // PAGES_PER_STEP
    n_tiles = t // tm
    n_esteps = N_EXPERTS // EXPERTS_PER_STEP
    assert n_tiles * n_esteps == nb * steps_per_seq and npages // PAGES_PER_BLOCK <= LANES

    def seq_of(i, e):
        return (i * n_esteps + e) // steps_per_seq

    def pspec(r):
        def imap(i, e, pt):
            g = i * n_esteps + e
            return (pt[g // steps_per_seq, lax.rem(g, steps_per_seq) * PAGES_PER_STEP + r], 0, 0, 0)
        return pl.BlockSpec((1, N_HEADS_A, HEAD_DIM_A, PAGE_SIZE), imap)

    row = lambda n: pl.BlockSpec((tm, n), lambda i, e, pt: (i, 0))
    return pl.pallas_call(
        functools.partial(_moe_stream_kernel, steps_per_seq=steps_per_seq),
        grid_spec=pltpu.PrefetchScalarGridSpec(
            num_scalar_prefetch=1, grid=(n_tiles, n_esteps),
            in_specs=[row(D_MODEL), row(D_MODEL), row(LANES),
                      pl.BlockSpec((EXPERTS_PER_STEP, D_MODEL, 2 * D_FF_EXPERT), lambda i, e, pt: (e, 0, 0)),
                      pl.BlockSpec((EXPERTS_PER_STEP, D_FF_EXPERT, D_MODEL), lambda i, e, pt: (e, 0, 0)),
                      pl.BlockSpec(gfin.shape, lambda i, e, pt: (0, 0)),
                      pl.BlockSpec((1, D_ATTN, LANES), lambda i, e, pt: (seq_of(i, e), 0, 0))]
            + [pspec(r) for r in range(PAGES_PER_STEP)],
            out_specs=[row(D_MODEL),
                       pl.BlockSpec((1, N_HEADS_A, LANES), lambda i, e, pt: (seq_of(i, e), 0, 0))],
            scratch_shapes=[pltpu.VMEM((tm, D_MODEL), F32)]),
        out_shape=[jax.ShapeDtypeStruct((t, D_MODEL), F32), jax.ShapeDtypeStruct((nb, N_HEADS_A, LANES), F32)],
        compiler_params=_cparams("arbitrary", "arbitrary"), name="moe_keysums",
    )(page_table, x2, xn, cmb, w13, w2, gfin, qb, *([cache4] * PAGES_PER_STEP))


def _ssel_kernel(sc_ref, o_ref, *, n_past):
    sc = sc_ref[...] * (1.0 / MOBA_BLOCK)
    bl_i = lax.broadcasted_iota(I32, sc.shape, 1)
    bl = bl_i.astype(F32)
    valid = bl_i < n_past
    taken = jnp.zeros(sc.shape, jnp.bool_)
    out = jnp.zeros(sc.shape, F32)
    for r in range(MOBA_TOPK):
        sm = jnp.where(valid & jnp.logical_not(taken), sc, BIG_NEG)
        mx = jnp.max(sm, axis=1, keepdims=True)
        idx = jnp.min(jnp.where((sm == mx) & (sm > BIG_NEG), bl, NO_IDX), axis=1, keepdims=True)
        taken = taken | (bl == idx)
        out = jnp.where(bl_i == r, idx, out)
    o_ref[...] = out.astype(I32)


def _sample_select(scores, n_past):
    return pl.pallas_call(
        functools.partial(_ssel_kernel, n_past=n_past),
        out_shape=jax.ShapeDtypeStruct(scores.shape, I32), name="sample_select",
    )(scores)


PAGES_PER_HEAD = MOBA_TOPK * PAGES_PER_BLOCK
N_SEL_PAGES = N_HEADS_A * PAGES_PER_HEAD


def _sattn_kernel(sel_ref, pt_ref, q_ref, kn_ref, vn_ref, *refs):
    del sel_ref, pt_ref
    kp, vp, o_ref = refs[:N_SEL_PAGES], refs[N_SEL_PAGES:2 * N_SEL_PAGES], refs[2 * N_SEL_PAGES]
    for h in range(N_HEADS_A):
        qc = q_ref[0][:, h:h + 1] * (HEAD_DIM_A ** -0.5)
        kts = [kp[h * PAGES_PER_HEAD + r][0, 0] for r in range(PAGES_PER_HEAD)]
        vts = [vp[h * PAGES_PER_HEAD + r][0, 0] for r in range(PAGES_PER_HEAD)]
        ss = [jnp.sum(kt * qc, axis=0, keepdims=True) for kt in kts]
        s_self = jnp.sum(qc * kn_ref[0][:, h:h + 1], axis=0, keepdims=True)
        mx = s_self
        for s in ss:
            mx = jnp.maximum(mx, jnp.max(s, axis=1, keepdims=True))
        p_self = jnp.exp(s_self - mx)
        den = p_self
        acc = jnp.zeros((HEAD_DIM_A, PAGE_SIZE), F32)
        for s, vt in zip(ss, vts):
            p = jnp.exp(s - mx)
            den = den + jnp.sum(p, axis=1, keepdims=True)
            acc = acc + vt * p
        out = p_self * vn_ref[0][:, h:h + 1] + jnp.sum(acc, axis=1, keepdims=True)
        o_ref[0, :, h:h + 1] = out / den


def _sample_attn(sel_flat, pt_flat, q3, kn3, vn3, ck4, cv4, n_pages):
    nb = q3.shape[0]

    def pspec(slot):
        h, rem = divmod(slot, PAGES_PER_HEAD)
        r, half = divmod(rem, PAGES_PER_BLOCK)

        def imap(b, sel, pt):
            blk = sel[(b * N_HEADS_A + h) * MOBA_TOPK + r]
            return (pt[b * n_pages + blk * PAGES_PER_BLOCK + half], h, 0, 0)

        return pl.BlockSpec((1, 1, HEAD_DIM_A, PAGE_SIZE), imap)

    tok = pl.BlockSpec((1, HEAD_DIM_A, N_HEADS_A), lambda b, sel, pt: (b, 0, 0))
    return pl.pallas_call(
        _sattn_kernel,
        grid_spec=pltpu.PrefetchScalarGridSpec(
            num_scalar_prefetch=2, grid=(nb,),
            in_specs=[tok, tok, tok] + [pspec(s) for s in range(N_SEL_PAGES)] * 2,
            out_specs=tok),
        out_shape=jax.ShapeDtypeStruct((nb, HEAD_DIM_A, N_HEADS_A), F32),
        compiler_params=_cparams("arbitrary"), name="sample_attn",
    )(sel_flat, pt_flat, q3, kn3, vn3, *([ck4] * N_SEL_PAGES), *([cv4] * N_SEL_PAGES))


MID_ROWS = 512
MOE_ROWS = 1024


def _tail_prompt(x, attn, sga, sb, mk3, mv3, w, batch, seq, key_stream):
    x2, xn, cmb = _mid(x, attn, sga, sb, mk3, mv3, w, batch, seq, MID_ROWS)
    return _moe_with_key_sums(x2, xn, cmb, w['w13'], w['w2'], w['gfin'], MOE_ROWS, *key_stream)


def _tail_sample(x, attn, sga, sb, mk3, mv3, w):
    t = x.shape[0]
    x1, xq = _merge(x, attn, sga, sb, w['wpa'], w['wo'], w['gx'], w['wxq'], t, True)
    xo = _xattn_one(xq.reshape(t, 1, D_XATTN), mk3, mv3).reshape(t, D_XATTN)
    x2, xn, cmb = _post(x1, xo, w['wxo'], w['gf'], w['wr'], w['br'], t, True)
    return _moe(x2, xn, cmb, w['w13'], w['w2'], w['gfin'], t)


def kernel(x_prompt, x_sample, mem_prompt, cache_k, cache_v, page_table, state_ssm_re, state_ssm_im,
           cache_mem_k, cache_mem_v, g_mix, w_in, ssm_lambda_re, ssm_lambda_im, ssm_log_dt,
           ssm_b_re, ssm_b_im, ssm_c_re, ssm_c_im, ssm_d, w_glu, w_pa, w_pb, w_o, g_x, g_mem,
           w_xq, w_xk, w_xv, w_xo, g_ffn, w_group, b_group, w_erouter, b_erouter, w1, w3, w2, g_final):
    depth = w_in.shape[0]
    assert depth == 1
    l = 0
    bp, sp, _ = x_prompt.shape
    bs, ss, _ = x_sample.shape
    assert ss == 1
    n_pages = page_table.shape[1]
    past_len = n_pages * PAGE_SIZE
    assert past_len % MOBA_BLOCK == 0 and sp % MOBA_BLOCK == 0

    row = lambda a: a.reshape(1, -1).astype(F32)
    bf = lambda a: a.astype(BF16)
    wr = jnp.zeros((D_MODEL, LANES), F32)
    wr = wr.at[:, :N_EXPERTS].set(w_erouter[l]).at[:, GROUP_LANE0:GROUP_LANE0 + N_EXPERT_GROUPS].set(w_group[l])
    br = jnp.zeros((1, LANES), F32)
    br = br.at[0, :N_EXPERTS].set(b_erouter[l]).at[0, GROUP_LANE0:GROUP_LANE0 + N_EXPERT_GROUPS].set(b_group[l])
    ws = dict(wpa=w_pa[l], wo=w_o[l], gx=row(g_x[l]), wxq=w_xq[l], wxo=w_xo[l], gf=row(g_ffn[l]),
              wr=wr, br=br, w13=bf(jnp.concatenate([w1[l], w3[l]], axis=-1)), w2=bf(w2[l]),
              gfin=row(g_final))
    wp = dict(ws, wpa=bf(w_pa[l]), wo=bf(w_o[l]), wxq=bf(w_xq[l]), wxo=bf(w_xo[l]))
    gmix = row(g_mix[l])

    ar, ai, bbt_re, bbt_im = _ssm_params(ssm_lambda_re[l], ssm_lambda_im[l], ssm_log_dt[l], ssm_b_re[l],
                                         ssm_b_im[l])
    bb, cc = _s5_weights(ar, ai, bbt_re, bbt_im, ssm_c_re[l], ssm_c_im[l])
    d_row = row(ssm_d[l])

    tp = bp * sp
    xp = x_prompt.reshape(tp, D_MODEL)
    q, kt, vt, kb, vb, u, sga, gb = _inproj(xp, gmix, bf(w_in[l]), 512, bp)
    attn = _moba_prompt(q, kb, vb, bp, sp)
    h0 = jnp.zeros((bp, 2 * N_STATE), F32)
    sb3, ht = _s5(u.reshape(bp, sp, D_SSM), gb.reshape(bp, sp, D_MODEL), h0, ar, ai, bf(bb), bf(cc), d_row,
                  bf(w_glu[l]), bf(w_pb[l]), bn=bp, lc=128, precise=False)
    sr_p, si_p = _lanes_to_state(ht)
    mk_p, mv_p = _memkv(mem_prompt.reshape(bp * N_MEM, D_MODEL), row(g_mem[l]), bf(w_xk[l]), bf(w_xv[l]), 256)
    ck4 = jnp.transpose(cache_k[l], (0, 2, 3, 1))
    cv4 = jnp.transpose(cache_v[l], (0, 2, 3, 1))
    xs = x_sample.reshape(bs, D_MODEL)
    proj_s = _inproj_precise(xs, gmix, w_in[l])
    q_s, k_s, v_s = (proj_s[:, D_ATTN * n:D_ATTN * (n + 1)] for n in range(3))
    u_s = proj_s[:, 3 * D_ATTN:GA_COL0]
    sga_s = proj_s[:, GA_COL0:GA_COL0 + D_MODEL]
    gb_s = proj_s[:, GA_COL0 + D_MODEL:]
    qb = jnp.broadcast_to(q_s[:, :, None], (bs, D_ATTN, LANES))
    y_p, scores = _tail_prompt(xp, attn, sga, sb3.reshape(tp, D_MODEL), mk_p.reshape(bp, N_MEM, D_XATTN),
                               mv_p.reshape(bp, N_MEM, D_XATTN), wp, bp, sp, (ck4, page_table, qb))

    sel = _sample_select(scores.reshape(bs * N_HEADS_A, LANES), past_len // MOBA_BLOCK)
    sel_flat = sel[:, :MOBA_TOPK].reshape(-1)
    hsplit = lambda a: a.reshape(bs, N_HEADS_A, HEAD_DIM_A).transpose(0, 2, 1)
    attn_s = _sample_attn(sel_flat, page_table.reshape(-1), hsplit(q_s), hsplit(k_s), hsplit(v_s), ck4, cv4,
                          n_pages).transpose(0, 2, 1)
    h0_s = _state_to_lanes(state_ssm_re[l].reshape(bs, N_STATE), state_ssm_im[l].reshape(bs, N_STATE))
    sb_s, ht_s = _s5(u_s.reshape(1, bs, D_SSM), gb_s.reshape(1, bs, D_MODEL), h0_s, ar, ai, bb, cc, d_row,
                     w_glu[l], w_pb[l], bn=bs, lc=1, precise=True)
    sr_s, si_s = _lanes_to_state(ht_s)
    y_s = _tail_sample(xs, attn_s.reshape(bs, D_ATTN), sga_s, sb_s.reshape(bs, D_MODEL),
                       cache_mem_k[l].reshape(bs, N_MEM, D_XATTN), cache_mem_v[l].reshape(bs, N_MEM, D_XATTN), ws)

    kv5 = lambda a, b, s: a.reshape(1, b, s, N_HEADS_A, HEAD_DIM_A)
    kvt5 = lambda a: a.reshape(bp, N_HEADS_A, HEAD_DIM_A, sp).transpose(0, 3, 1, 2)[None]
    st4 = lambda a: a[None]
    mem5 = lambda a: a.reshape(1, bp, N_MEM, N_HEADS_X, HEAD_DIM_X)
    return (y_p.reshape(bp, sp, D_MODEL), y_s.reshape(bs, 1, D_MODEL),
            kvt5(kt), kvt5(vt), st4(sr_p), st4(si_p), mem5(mk_p), mem5(mv_p),
            kv5(k_s, bs, 1), kv5(v_s, bs, 1), st4(sr_s), st4(si_s))
```

```python
import functools

import jax
import jax.numpy as jnp
from jax import lax
from jax.experimental import pallas as pl
from jax.experimental.pallas import tpu as pltpu

F32 = jnp.float32
BF16 = jnp.bfloat16
I32 = jnp.int32

D_MODEL = 1024
N_HEADS_A = 8
HEAD_DIM_A = 64
D_ATTN = N_HEADS_A * HEAD_DIM_A
MOBA_BLOCK = 256
BLOCK_SHIFT = MOBA_BLOCK.bit_length() - 1
MOBA_TOPK = 3
D_SSM = 512
SSM_GROUP = 16
N_SSM_GROUPS = D_SSM // SSM_GROUP
SSM_STATE = 64
N_STATE = N_SSM_GROUPS * SSM_STATE
N_MEM = 256
N_HEADS_X = 4
HEAD_DIM_X = 128
D_XATTN = N_HEADS_X * HEAD_DIM_X
N_EXPERT_GROUPS = 4
EXPERTS_PER_GROUP = 4
N_EXPERTS = N_EXPERT_GROUPS * EXPERTS_PER_GROUP
D_FF_EXPERT = 256
D_IN_PROJ = 3 * D_ATTN + D_SSM + 2 * D_MODEL
RMS_EPS = 1e-6
NEG_INF = -1e30
PAGE_SIZE = 128

LANES = 128
VMEM_LIMIT = 56 * 1024 * 1024
BIG_NEG = -3e38
NO_IDX = 1e9

INPROJ_ROWS = 512
S5_STEPS = 128
MEMKV_ROWS = 256
MID_ROWS = 512
MOE_ROWS = 1024


def _cparams(*sem):
    return pltpu.CompilerParams(dimension_semantics=sem, vmem_limit_bytes=VMEM_LIMIT)


def _rms(x, g):
    return x * lax.rsqrt(jnp.mean(x * x, axis=-1, keepdims=True) + RMS_EPS) * g


def _dot(a, b):
    return jnp.dot(a, b, preferred_element_type=F32)


def _dot_nt(a, b):
    return lax.dot_general(a, b, (((1,), (1,)), ((), ())), preferred_element_type=F32)


def _split_bf16(x):
    hi = x.astype(BF16)
    return hi, (x - hi.astype(F32)).astype(BF16)


def _split3_bf16(x):
    hi, rest = x.astype(BF16), None
    rest = x - hi.astype(F32)
    mid = rest.astype(BF16)
    return hi, mid, (rest - mid.astype(F32)).astype(BF16)


def _mm(a, w, precise, nt=False):
    dot = _dot_nt if nt else _dot
    if not precise:
        return dot(a.astype(BF16), w.astype(BF16))
    ah, am, al = _split3_bf16(a.astype(F32))
    wh, wm, wl = _split3_bf16(w)
    return ((dot(al, wh) + dot(ah, wl)) + (dot(am, wm) + dot(am, wh) + dot(ah, wm))) + dot(ah, wh)


def _inproj_kernel(x_ref, g_ref, w_ref, q_ref, k_ref, v_ref, kb_ref, vb_ref, u_ref, sga_ref, gb_ref):
    xn = _rms(x_ref[...], g_ref[...]).astype(BF16)

    def mm(c0, n):
        return _dot(xn, w_ref[:, c0:c0 + n])

    q_ref[...] = mm(0, D_ATTN)
    k = mm(D_ATTN, D_ATTN)
    k_ref[0] = k.T
    kb_ref[...] = k.astype(BF16)
    v = mm(2 * D_ATTN, D_ATTN)
    v_ref[0] = v.T
    vb_ref[...] = v.astype(BF16)
    u_ref[...] = mm(3 * D_ATTN, D_SSM)
    sga_ref[...] = jax.nn.sigmoid(mm(3 * D_ATTN + D_SSM, D_MODEL)).astype(BF16)
    gb_ref[...] = mm(3 * D_ATTN + D_SSM + D_MODEL, D_MODEL)


def _inproj(x, g, w_bf, tm, batch):
    t = x.shape[0]
    seq = t // batch
    tiles = seq // tm
    row = lambda n: pl.BlockSpec((tm, n), lambda i: (i, 0))
    col = pl.BlockSpec((1, D_ATTN, tm), lambda i: (i // tiles, 0, i % tiles))
    full = lambda a: pl.BlockSpec(a.shape, lambda i: (0,) * a.ndim)
    shp = lambda n, dt: jax.ShapeDtypeStruct((t, n), dt)
    tshp = jax.ShapeDtypeStruct((batch, D_ATTN, seq), F32)
    return pl.pallas_call(
        _inproj_kernel,
        grid=(t // tm,),
        in_specs=[row(D_MODEL), full(g), full(w_bf)],
        out_specs=[row(D_ATTN), col, col, row(D_ATTN), row(D_ATTN), row(D_SSM), row(D_MODEL), row(D_MODEL)],
        out_shape=[shp(D_ATTN, F32), tshp, tshp, shp(D_ATTN, BF16),
                   shp(D_ATTN, BF16), shp(D_SSM, F32), shp(D_MODEL, BF16), shp(D_MODEL, F32)],
        compiler_params=_cparams("parallel"),
        name="inproj",
    )(x, g, w_bf)


INPROJ_COLS = 512
GA_COL0 = 3 * D_ATTN + D_SSM


def _inproj_precise_kernel(x_ref, g_ref, w_ref, o_ref):
    c0 = pl.program_id(0) * INPROJ_COLS
    acc = _mm(_rms(x_ref[...], g_ref[...]), w_ref[...], True)
    is_ga = (c0 >= GA_COL0) & (c0 < GA_COL0 + D_MODEL)

    @pl.when(is_ga)
    def _():
        o_ref[...] = jax.nn.sigmoid(acc)

    @pl.when(jnp.logical_not(is_ga))
    def _():
        o_ref[...] = acc


def _inproj_precise(x, g, w):
    t = x.shape[0]
    full = lambda a: pl.BlockSpec(a.shape, lambda j: (0,) * a.ndim)
    return pl.pallas_call(
        _inproj_precise_kernel,
        grid=(D_IN_PROJ // INPROJ_COLS,),
        in_specs=[full(x), full(g), pl.BlockSpec((D_MODEL, INPROJ_COLS), lambda j: (0, j))],
        out_specs=pl.BlockSpec((t, INPROJ_COLS), lambda j: (0, j)),
        out_shape=jax.ShapeDtypeStruct((t, D_IN_PROJ), F32),
        compiler_params=_cparams("parallel"),
        name="inproj_sample",
    )(x, g, w)


def _ssm_param_kernel(lr_ref, li_ref, dt_ref, br_ref, bi_ref, ar_ref, ai_ref, bbr_ref, bbi_ref):
    lr = jnp.minimum(lr_ref[...], -1e-4)
    li = li_ref[...]
    dt = jnp.exp(dt_ref[...])
    mag = jnp.exp(lr * dt)
    ar = mag * jnp.cos(li * dt)
    ai = mag * jnp.sin(li * dt)
    den = lr * lr + li * li
    nr = ar - 1.0
    cr = (nr * lr + ai * li) / den
    ci = (ai * lr - nr * li) / den
    ar_ref[...] = ar
    ai_ref[...] = ai
    br = br_ref[...]
    bi = bi_ref[...]
    bbr_ref[...] = cr * br - ci * bi
    bbi_ref[...] = cr * bi + ci * br


def _ssm_params(lam_re, lam_im, log_dt, b_re, b_im):
    n = N_STATE
    lr = lam_re.reshape(1, n)
    li = lam_im.reshape(1, n)
    dt = jnp.broadcast_to(log_dt[:, None], (N_SSM_GROUPS, SSM_STATE)).reshape(1, n)
    brt = b_re.reshape(n, SSM_GROUP).T
    bit = b_im.reshape(n, SSM_GROUP).T
    row = jax.ShapeDtypeStruct((1, n), F32)
    mat = jax.ShapeDtypeStruct((SSM_GROUP, n), F32)
    return pl.pallas_call(_ssm_param_kernel, out_shape=[row, row, mat, mat], name="ssm_params")(
        lr, li, dt, brt, bit)


HALF_STATE = N_STATE // 2
SCAN_COLS = 512
SCAN_TILES = SCAN_COLS // 128


def _s5_kernel(u_ref, gb_ref, h0_ref, ar_ref, ai_ref, bb_ref, cc_ref, d_ref, wglu_ref, wpb_ref, *refs,
               bn, lc, precise):
    m = bn * lc
    if lc > 1:
        perm_ref, sb_ref, ht_ref, s_ref, yt_ref, carry_ref = refs
    else:
        sb_ref, ht_ref, s_ref, yt_ref, carry_ref = refs

    @pl.when(pl.program_id(0) == 0)
    def _():
        carry_ref[...] = h0_ref[...]

    u = u_ref[...].reshape(m, D_SSM)
    us = _dot(perm_ref[...], u.astype(BF16)).astype(BF16) if lc > 1 else u
    for j in range(16):
        sec, jj = divmod(j, 4)
        gbase = (sec // 2) * 16 + jj * 4
        lt = (gbase * SSM_GROUP) // LANES
        bu = _mm(us[:, LANES * lt:LANES * (lt + 1)], bb_ref[j], precise)
        s_ref[2 * j] = bu[:, :LANES]
        s_ref[2 * j + 1] = bu[:, LANES:]

    tiles_half = HALF_STATE // LANES
    for h in range(2):
        for c in range(HALF_STATE // SCAN_COLS):
            re_t = [2 * tiles_half * h + SCAN_TILES * c + n for n in range(SCAN_TILES)]
            im_t = [t + tiles_half for t in re_t]
            a_t = [tiles_half * h + SCAN_TILES * c + n for n in range(SCAN_TILES)]
            ars = [jnp.broadcast_to(ar_ref[:, LANES * t:LANES * (t + 1)], (bn, LANES)) for t in a_t]
            ais = [jnp.broadcast_to(ai_ref[:, LANES * t:LANES * (t + 1)], (bn, LANES)) for t in a_t]

            def body(t, carry, re_t=re_t, im_t=im_t, ars=ars, ais=ais):
                rows = pl.ds(pl.multiple_of(t * bn, bn), bn)
                out = []
                for n in range(SCAN_TILES):
                    xr, xi = carry[2 * n], carry[2 * n + 1]
                    nxr = ars[n] * xr - ais[n] * xi + s_ref[re_t[n], rows, :]
                    nxi = ars[n] * xi + ais[n] * xr + s_ref[im_t[n], rows, :]
                    s_ref[re_t[n], rows, :] = nxr
                    s_ref[im_t[n], rows, :] = nxi
                    out += [nxr, nxi]
                return tuple(out)

            x0 = []
            for n in range(SCAN_TILES):
                x0 += [carry_ref[:, LANES * re_t[n]:LANES * (re_t[n] + 1)],
                       carry_ref[:, LANES * im_t[n]:LANES * (im_t[n] + 1)]]
            xs = lax.fori_loop(0, lc, body, tuple(x0), unroll=min(lc, 8))
            for n in range(SCAN_TILES):
                carry_ref[:, LANES * re_t[n]:LANES * (re_t[n] + 1)] = xs[2 * n]
                carry_ref[:, LANES * im_t[n]:LANES * (im_t[n] + 1)] = xs[2 * n + 1]
    ht_ref[...] = carry_ref[...]

    ys = []
    for h in range(2):
        sdt = F32 if precise else BF16
        xh = jnp.concatenate([s_ref[2 * tiles_half * h + n].astype(sdt) for n in range(2 * tiles_half)], axis=1)
        ys.append(_mm(xh, cc_ref[h], precise))
    y = jnp.concatenate(ys, axis=1)
    if lc > 1:
        for n in range(D_SSM // LANES):
            yt_ref[n] = y[:, LANES * n:LANES * (n + 1)]
        y = jnp.concatenate(
            [jnp.concatenate([yt_ref[n, pl.ds(b, lc, stride=bn), :] for n in range(D_SSM // LANES)], axis=1)
             for b in range(bn)], axis=0)
    y = y + d_ref[...] * u
    z = jax.nn.gelu(y)
    s5 = z * jax.nn.sigmoid(_mm(z, wglu_ref[...], precise))
    pb = _mm(s5, wpb_ref[...], precise)
    gb = gb_ref[...].reshape(m, D_MODEL)
    sb_ref[...] = (jax.nn.sigmoid(gb) * pb).astype(sb_ref.dtype).reshape(sb_ref.shape)


def _s5(u3, gb3, h0, ar, ai, bb, cc, d, wglu, wpb, bn, lc, precise):
    nb, s, _ = u3.shape
    rows = bn * lc // nb
    nchunk = s // rows
    full = lambda a: pl.BlockSpec(a.shape, lambda c: (0,) * a.ndim)
    blk = lambda n: pl.BlockSpec((nb, rows, n), lambda c: (0, c, 0))
    perms = []
    if lc > 1:
        r = jnp.arange(bn * lc)
        perms = [(((r % bn) * lc + r // bn)[:, None] == r[None, :]).astype(BF16)]
    return pl.pallas_call(
        functools.partial(_s5_kernel, bn=bn, lc=lc, precise=precise),
        grid=(nchunk,),
        in_specs=[blk(D_SSM), blk(D_MODEL), full(h0), full(ar), full(ai), full(bb), full(cc), full(d),
                  full(wglu), full(wpb)] + [full(p) for p in perms],
        out_specs=[blk(D_MODEL), full(h0)],
        out_shape=[jax.ShapeDtypeStruct((nb, s, D_MODEL), F32 if precise else BF16),
                   jax.ShapeDtypeStruct(h0.shape, F32)],
        scratch_shapes=[pltpu.VMEM((2 * N_STATE // LANES, bn * lc, LANES), F32),
                        pltpu.VMEM((D_SSM // LANES, bn * lc, LANES), F32),
                        pltpu.VMEM((bn, 2 * N_STATE), F32)],
        compiler_params=_cparams("arbitrary"),
        name="s5",
    )(u3, gb3, h0, ar, ai, bb, cc, d, wglu, wpb, *perms)


def _s5_weights(ar, ai, bbt_re, bbt_im, c_re, c_im):
    g, p, h = N_SSM_GROUPS, SSM_STATE, SSM_GROUP
    eye = jnp.eye(g, dtype=F32)

    def bfull(bt):
        b = bt.reshape(h, g, p)
        return jnp.einsum('hgp,gk->ghkp', b, eye).reshape(g * h, g * p)

    bre, bim = bfull(bbt_re), bfull(bbt_im)
    tiles = []
    for j in range(16):
        sec, jj = divmod(j, 4)
        src = bre if sec % 2 == 0 else bim
        gbase = (sec // 2) * 16 + jj * 4
        lt = (gbase * h) // LANES
        tiles.append(src[LANES * lt:LANES * (lt + 1), gbase * p:(gbase + 4) * p])
    bb = jnp.stack(tiles)

    def cfull(c):
        return jnp.einsum('ghp,gk->gpkh', c, eye).reshape(g * p, g * h)

    cre, cim = cfull(c_re), cfull(c_im)
    halves = []
    for hh in range(2):
        rs = slice(HALF_STATE * hh, HALF_STATE * (hh + 1))
        cs = slice(256 * hh, 256 * (hh + 1))
        halves.append(jnp.concatenate([cre[rs, cs], -cim[rs, cs]], axis=0))
    return bb, jnp.stack(halves)


def _state_to_lanes(re, im):
    b = re.shape[0]
    r = re.reshape(b, 2, HALF_STATE)
    i = im.reshape(b, 2, HALF_STATE)
    return jnp.concatenate([r[:, 0], i[:, 0], r[:, 1], i[:, 1]], axis=1)


def _lanes_to_state(h):
    b = h.shape[0]
    h4 = h.reshape(b, 4, HALF_STATE)
    re = jnp.concatenate([h4[:, 0], h4[:, 2]], axis=1).reshape(b, N_SSM_GROUPS, SSM_STATE)
    im = jnp.concatenate([h4[:, 1], h4[:, 3]], axis=1).reshape(b, N_SSM_GROUPS, SSM_STATE)
    return re, im


MOBA_CHUNK = 4


def _moba_kernel(q_ref, k_ref, v_ref, o_ref, ka0_ref, ka1_ref, selb_ref, *, seq):
    blk, hd = MOBA_BLOCK, HEAD_DIM_A
    nblk = seq // blk
    chunk_rows = MOBA_CHUNK * blk

    k = k_ref[...]
    rblk = lax.broadcasted_iota(I32, (seq, LANES), 0) >> BLOCK_SHIFT
    lane = lax.broadcasted_iota(I32, (seq, LANES), 1)
    kf = k.astype(F32)
    ka0_ref[...] = jnp.where(lane < hd, kf, jnp.where(lane - hd == rblk, 1.0, 0.0)).astype(BF16)
    ka1_ref[...] = jnp.where(lane >= hd, kf, jnp.where(lane == rblk, 1.0, 0.0)).astype(BF16)
    r = lax.broadcasted_iota(I32, (LANES, seq), 0)
    cblk = lax.broadcasted_iota(I32, (LANES, seq), 1) >> BLOCK_SHIFT
    ind = jnp.where((r == cblk) | (r - hd == cblk), 1.0, 0.0).astype(BF16)
    kmean = _dot(ind, k) * (1.0 / blk)
    rr = lax.broadcasted_iota(I32, (LANES, LANES), 0)
    ll = lax.broadcasted_iota(I32, (LANES, LANES), 1)
    keep = ((rr < hd) & (ll >= hd)) | ((rr >= hd) & (ll < hd))
    ahi, alo = _split_bf16(jnp.where(keep, kmean, 0.0))

    qhi, qlo = _split_bf16(q_ref[...])
    sc = _dot_nt(qhi, ahi) + _dot_nt(qlo, ahi) + _dot_nt(qhi, alo)
    sct = sc.T
    blk_i = lax.broadcasted_iota(I32, (nblk, seq), 0)
    blk_f = blk_i.astype(F32)
    own = lax.broadcasted_iota(I32, (nblk, seq), 1) >> BLOCK_SHIFT
    valid = blk_i < own
    biases = []
    for x in (sct[0:nblk], sct[hd:hd + nblk]):
        taken = jnp.zeros((nblk, seq), jnp.bool_)
        for _ in range(MOBA_TOPK):
            sm = jnp.where(valid & jnp.logical_not(taken), x, BIG_NEG)
            mx = jnp.max(sm, axis=0, keepdims=True)
            idx = jnp.min(jnp.where((sm == mx) & (sm > BIG_NEG), blk_f, NO_IDX), axis=0, keepdims=True)
            taken = taken | (blk_f == idx)
        biases.append(jnp.where(taken | (blk_i == own), 0.0, NEG_INF))
    pad = jnp.zeros((hd - nblk, seq), F32)
    selb_ref[...] = jnp.concatenate([biases[0], pad, biases[1], pad], axis=0).T.astype(BF16)

    lane_b = lax.broadcasted_iota(I32, (blk, LANES), 1)
    col_minus_row = (lax.broadcasted_iota(I32, (blk, chunk_rows), 1)
                     - lax.broadcasted_iota(I32, (blk, chunk_rows), 0))
    heads = ((ka0_ref, lane_b < hd), (ka1_ref, lane_b >= hd))
    chunk_shift = MOBA_CHUNK.bit_length() - 1

    def qblock(i, _):
        off = pl.multiple_of(i * blk, blk)
        qs = q_ref[pl.ds(off, blk), :] * (hd ** -0.5)
        sb = selb_ref[pl.ds(off, blk), :].astype(F32)
        qas = [jnp.where(mine, qs, sb).astype(BF16) for _, mine in heads]

        dc = i >> chunk_shift
        thr = (i - (dc << chunk_shift)) * blk

        def attend(n_chunks):
            def run():
                outs = []
                for (ka_ref, _), qa in zip(heads, qas):
                    ss = [_dot_nt(qa, ka_ref[c * chunk_rows:(c + 1) * chunk_rows, :]) for c in range(n_chunks)]
                    ss[-1] = jnp.where(col_minus_row <= thr, ss[-1], NEG_INF)
                    m = functools.reduce(jnp.maximum, [jnp.max(s, axis=1, keepdims=True) for s in ss])
                    ps = [jnp.exp(s - m) for s in ss]
                    l = functools.reduce(lambda a, b: a + b, [jnp.sum(p, axis=1, keepdims=True) for p in ps])
                    pv = _dot(jnp.concatenate([p.astype(BF16) for p in ps], axis=1),
                              v_ref[0:n_chunks * chunk_rows, :])
                    outs.append(pv / l)
                return jnp.where(lane_b < hd, outs[0], outs[1]).astype(BF16)
            return run

        o_ref[pl.ds(off, blk), :] = lax.switch(dc, [attend(n) for n in range(1, nblk // MOBA_CHUNK + 1)])
        return 0

    lax.fori_loop(0, nblk, qblock, 0)


def _moba_prompt(q, kb, vb, batch, seq):
    spec = pl.BlockSpec((seq, LANES), lambda b, hp: (b, hp))
    return pl.pallas_call(
        functools.partial(_moba_kernel, seq=seq),
        grid=(batch, D_ATTN // LANES),
        in_specs=[spec, spec, spec],
        out_specs=spec,
        out_shape=jax.ShapeDtypeStruct((batch * seq, D_ATTN), BF16),
        scratch_shapes=[pltpu.VMEM((seq, LANES), BF16)] * 3,
        compiler_params=_cparams("parallel", "parallel"),
        name="moba_prompt",
    )(q, kb, vb)


def _memkv_kernel(m_ref, g_ref, wk_ref, wv_ref, mk_ref, mv_ref):
    mn = _rms(m_ref[...], g_ref[...]).astype(BF16)
    mk_ref[...] = _dot(mn, wk_ref[...])
    mv_ref[...] = _dot(mn, wv_ref[...])


def _memkv(mem, g, wk, wv, tm):
    t = mem.shape[0]
    row = lambda n: pl.BlockSpec((tm, n), lambda i: (i, 0))
    full = lambda a: pl.BlockSpec(a.shape, lambda i: (0,) * a.ndim)
    shp = jax.ShapeDtypeStruct((t, D_XATTN), F32)
    return pl.pallas_call(
        _memkv_kernel, grid=(t // tm,),
        in_specs=[row(D_MODEL), full(g), full(wk), full(wv)],
        out_specs=[row(D_XATTN)] * 2, out_shape=[shp, shp],
        compiler_params=_cparams("parallel"), name="memkv",
    )(mem, g, wk, wv)


def _merge_kernel(x_ref, at_ref, sga_ref, sb_ref, wpa_ref, wo_ref, gx_ref, wxq_ref, x1_ref, xq_ref, *, precise):
    pa = _mm(at_ref[...], wpa_ref[...], precise)
    merged = sga_ref[...].astype(F32) * pa + sb_ref[...].astype(F32)
    x1 = x_ref[...] + _mm(merged, wo_ref[...], precise)
    x1_ref[...] = x1
    xq_ref[...] = _mm(_rms(x1, gx_ref[...]), wxq_ref[...], precise).astype(xq_ref.dtype)


def _merge_sample(x, attn, sga, sb, wpa, wo, gx, wxq):
    t = x.shape[0]
    full = lambda a: pl.BlockSpec(a.shape, lambda i: (0,) * a.ndim)
    args = (x, attn, sga, sb, wpa, wo, gx, wxq)
    return pl.pallas_call(
        functools.partial(_merge_kernel, precise=True), grid=(1,),
        in_specs=[full(a) for a in args],
        out_specs=[pl.BlockSpec((t, D_MODEL), lambda i: (0, 0)), pl.BlockSpec((t, D_XATTN), lambda i: (0, 0))],
        out_shape=[jax.ShapeDtypeStruct((t, D_MODEL), F32), jax.ShapeDtypeStruct((t, D_XATTN), F32)],
        compiler_params=_cparams("arbitrary"), name="merge_sample",
    )(*args)


def _xattn_kernel(q_ref, mk_ref, mv_ref, o_ref):
    q = q_ref[0]
    mk = mk_ref[0].astype(BF16)
    mv = mv_ref[0].astype(BF16)
    outs = []
    for h in range(N_HEADS_X):
        cs = slice(HEAD_DIM_X * h, HEAD_DIM_X * (h + 1))
        s = _dot_nt(q[:, cs], mk[:, cs]) * (HEAD_DIM_X ** -0.5)
        e = jnp.exp(s - jnp.max(s, axis=1, keepdims=True))
        p = e / jnp.sum(e, axis=1, keepdims=True)
        outs.append(_dot(p.astype(BF16), mv[:, cs]))
    o_ref[0] = jnp.concatenate(outs, axis=1).astype(o_ref.dtype)


def _xattn_one_kernel(q_ref, mk_ref, mv_ref, o_ref):
    q = q_ref[0]
    outs = []
    for h in range(N_HEADS_X):
        cs = slice(HEAD_DIM_X * h, HEAD_DIM_X * (h + 1))
        s = jnp.sum(mk_ref[0, :, cs] * q[:, cs], axis=1, keepdims=True) * (HEAD_DIM_X ** -0.5)
        e = jnp.exp(s - jnp.max(s, axis=0, keepdims=True))
        p = e / jnp.sum(e, axis=0, keepdims=True)
        outs.append(jnp.sum(p * mv_ref[0, :, cs], axis=0, keepdims=True))
    o_ref[0] = jnp.concatenate(outs, axis=1)


def _xattn_one(q3, mk3, mv3):
    b = q3.shape[0]
    qspec = pl.BlockSpec((1, 1, D_XATTN), lambda bi: (bi, 0, 0))
    mspec = pl.BlockSpec((1, N_MEM, D_XATTN), lambda bi: (bi, 0, 0))
    return pl.pallas_call(
        _xattn_one_kernel, grid=(b,), in_specs=[qspec, mspec, mspec], out_specs=qspec,
        out_shape=jax.ShapeDtypeStruct(q3.shape, F32),
        compiler_params=_cparams("parallel"), name="xattn_one",
    )(q3, mk3, mv3)


GROUP_LANE0 = N_EXPERTS


def _post_kernel(x1_ref, xo_ref, wxo_ref, gf_ref, wr_ref, br_ref, x2_ref, xn_ref, cmb_ref, *, precise):
    x2 = x1_ref[...] + _mm(xo_ref[...], wxo_ref[...], precise)
    x2_ref[...] = x2
    t = _rms(x2, gf_ref[...])
    thi, tlo = _split_bf16(t)
    xn_ref[...] = thi
    if precise:
        logits = _mm(t, wr_ref[...], True) + br_ref[...]
    else:
        wrh, wrl = _split_bf16(wr_ref[...])
        logits = _dot(thi, wrh) + _dot(tlo, wrh) + _dot(thi, wrl) + br_ref[...]

    lane_i = lax.broadcasted_iota(I32, logits.shape, 1)
    lane = lane_i.astype(F32)
    lane_group = (lane_i >> (EXPERTS_PER_GROUP.bit_length() - 1)).astype(F32)
    isg = (lane_i >= GROUP_LANE0) & (lane_i < GROUP_LANE0 + N_EXPERT_GROUPS)
    gmax = jnp.max(jnp.where(isg, logits, BIG_NEG), axis=1, keepdims=True)
    eg = jnp.where(isg, jnp.exp(jnp.where(isg, logits, gmax) - gmax), 0.0)
    gp = eg / jnp.sum(eg, axis=1, keepdims=True)
    pg = jnp.max(jnp.where(isg, gp, -1.0), axis=1, keepdims=True)
    gi = jnp.min(jnp.where(isg & (gp == pg), lane, NO_IDX), axis=1, keepdims=True) - GROUP_LANE0

    insel = (lane_i < N_EXPERTS) & (lane_group == gi)
    el = jnp.where(insel, logits, BIG_NEG)
    m1 = jnp.max(el, axis=1, keepdims=True)
    i1 = jnp.min(jnp.where(insel & (el == m1), lane, NO_IDX), axis=1, keepdims=True)
    rest = insel & (lane != i1)
    el2 = jnp.where(rest, logits, BIG_NEG)
    m2 = jnp.max(el2, axis=1, keepdims=True)
    i2 = jnp.min(jnp.where(rest & (el2 == m2), lane, NO_IDX), axis=1, keepdims=True)
    e2 = jnp.exp(m2 - m1)
    den = 1.0 + e2
    cmb_ref[...] = jnp.where(lane == i1, (1.0 / den) * pg, jnp.where(lane == i2, (e2 / den) * pg, 0.0))


def _mid_kernel(x_ref, at_ref, sga_ref, sb_ref, mk_ref, mv_ref, wpa_ref, wo_ref, gx_ref, wxq_ref, wxo_ref,
                gf_ref, wr_ref, br_ref, x2_ref, xn_ref, cmb_ref, x1_s, xq_s, xo_s):
    _merge_kernel(x_ref, at_ref, sga_ref, sb_ref, wpa_ref, wo_ref, gx_ref, wxq_ref, x1_s, xq_s.at[0],
                  precise=False)
    _xattn_kernel(xq_s, mk_ref, mv_ref, xo_s)
    _post_kernel(x1_s, xo_s.at[0], wxo_ref, gf_ref, wr_ref, br_ref, x2_ref, xn_ref, cmb_ref, precise=False)


def _mid(x, attn, sga, sb, mk3, mv3, w, batch, seq, tm):
    tiles = seq // tm
    row = lambda n: pl.BlockSpec((tm, n), lambda b, s: (b * tiles + s, 0))
    full = lambda a: pl.BlockSpec(a.shape, lambda b, s: (0,) * a.ndim)
    mspec = pl.BlockSpec((1, N_MEM, D_XATTN), lambda b, s: (b, 0, 0))
    t = batch * seq
    weights = [w[k] for k in ('wpa', 'wo', 'gx', 'wxq', 'wxo', 'gf', 'wr', 'br')]
    return pl.pallas_call(
        _mid_kernel, grid=(batch, tiles),
        in_specs=[row(D_MODEL), row(D_ATTN), row(D_MODEL), row(D_MODEL), mspec, mspec]
        + [full(a) for a in weights],
        out_specs=[row(D_MODEL), row(D_MODEL), row(LANES)],
        out_shape=[jax.ShapeDtypeStruct((t, D_MODEL), F32), jax.ShapeDtypeStruct((t, D_MODEL), BF16),
                   jax.ShapeDtypeStruct((t, LANES), F32)],
        scratch_shapes=[pltpu.VMEM((tm, D_MODEL), F32), pltpu.VMEM((1, tm, D_XATTN), BF16),
                        pltpu.VMEM((1, tm, D_XATTN), BF16)],
        compiler_params=_cparams("parallel", "parallel"), name="mid",
    )(x, attn, sga, sb, mk3, mv3, *weights)


def _post_sample(x1, xo, wxo, gf, wr, br):
    t = x1.shape[0]
    full = lambda a: pl.BlockSpec(a.shape, lambda i: (0,) * a.ndim)
    out = lambda n: pl.BlockSpec((t, n), lambda i: (0, 0))
    args = (x1, xo, wxo, gf, wr, br)
    return pl.pallas_call(
        functools.partial(_post_kernel, precise=True), grid=(1,),
        in_specs=[full(a) for a in args],
        out_specs=[out(D_MODEL), out(D_MODEL), out(LANES)],
        out_shape=[jax.ShapeDtypeStruct((t, D_MODEL), F32), jax.ShapeDtypeStruct((t, D_MODEL), BF16),
                   jax.ShapeDtypeStruct((t, LANES), F32)],
        compiler_params=_cparams("arbitrary"), name="post_sample",
    )(*args)


EXPERTS_PER_STEP = 2
PAGES_PER_STEP = 32
PAGES_PER_BLOCK = MOBA_BLOCK // PAGE_SIZE
BLOCKS_PER_STEP = PAGES_PER_STEP // PAGES_PER_BLOCK


def _block_score_step(s, pages, qb_ref, sc_ref):
    lane = lax.broadcasted_iota(I32, (N_HEADS_A, LANES), 1)
    sc = sc_ref[0]
    qb = qb_ref[0]
    for r in range(BLOCKS_PER_STEP):
        acc = pages[PAGES_PER_BLOCK * r][0].reshape(D_ATTN, PAGE_SIZE)
        for t in range(1, PAGES_PER_BLOCK):
            acc = acc + pages[PAGES_PER_BLOCK * r + t][0].reshape(D_ATTN, PAGE_SIZE)
        prod = (acc * qb).reshape(N_HEADS_A, HEAD_DIM_A // 8, 8, PAGE_SIZE)
        per_head = jnp.sum(jnp.sum(prod, axis=1), axis=1)
        sc = jnp.where(lane == s * BLOCKS_PER_STEP + r, jnp.sum(per_head, axis=1, keepdims=True), sc)
    sc_ref[0] = sc


def _moe_stream_kernel(pt_ref, x2_ref, xn_ref, cmb_ref, w13_ref, w2_ref, gfin_ref, qb_ref, *refs,
                       steps_per_seq):
    del pt_ref
    pages, (y_ref, sc_ref, acc_ref) = refs[:PAGES_PER_STEP], refs[PAGES_PER_STEP:]
    s = lax.rem(pl.program_id(0) * pl.num_programs(1) + pl.program_id(1), steps_per_seq)

    @pl.when(s == 0)
    def _():
        sc_ref[0] = jnp.zeros((N_HEADS_A, LANES), F32)

    _moe_kernel(x2_ref, xn_ref, cmb_ref, w13_ref, w2_ref, gfin_ref, y_ref, acc_ref,
                side_work=functools.partial(_block_score_step, s, pages, qb_ref, sc_ref))


def _moe_kernel(x2_ref, xn_ref, cmb_ref, w13_ref, w2_ref, gfin_ref, y_ref, acc_ref, side_work=None):
    step = pl.program_id(1)

    @pl.when(step == 0)
    def _():
        acc_ref[...] = jnp.zeros_like(acc_ref)

    if side_work is not None:
        side_work()
    xn = xn_ref[...]
    cmb = cmb_ref[...]
    lane = lax.broadcasted_iota(I32, cmb.shape, 1)
    hds = []
    for k in range(EXPERTS_PER_STEP):
        h = _dot(xn, w13_ref[k])
        cw = jnp.sum(jnp.where(lane == step * EXPERTS_PER_STEP + k, cmb, 0.0), axis=1, keepdims=True)
        hds.append((jax.nn.silu(h[:, :D_FF_EXPERT]) * h[:, D_FF_EXPERT:] * cw).astype(BF16))
    w2 = w2_ref[...].reshape(EXPERTS_PER_STEP * D_FF_EXPERT, D_MODEL)
    acc_ref[...] += _dot(jnp.concatenate(hds, axis=1), w2)

    @pl.when(step == pl.num_programs(1) - 1)
    def _():
        y_ref[...] = _rms(x2_ref[...] + acc_ref[...], gfin_ref[...])


def _moe(x2, xn, cmb, w13, w2, gfin, tm):
    t = x2.shape[0]
    row = lambda n: pl.BlockSpec((tm, n), lambda i, e: (i, 0))
    return pl.pallas_call(
        _moe_kernel, grid=(t // tm, N_EXPERTS // EXPERTS_PER_STEP),
        in_specs=[row(D_MODEL), row(D_MODEL), row(LANES),
                  pl.BlockSpec((EXPERTS_PER_STEP, D_MODEL, 2 * D_FF_EXPERT), lambda i, e: (e, 0, 0)),
                  pl.BlockSpec((EXPERTS_PER_STEP, D_FF_EXPERT, D_MODEL), lambda i, e: (e, 0, 0)),
                  pl.BlockSpec(gfin.shape, lambda i, e: (0, 0))],
        out_specs=row(D_MODEL),
        out_shape=jax.ShapeDtypeStruct((t, D_MODEL), F32),
        scratch_shapes=[pltpu.VMEM((tm, D_MODEL), F32)],
        compiler_params=_cparams("parallel", "arbitrary"), name="moe",
    )(x2, xn, cmb, w13, w2, gfin)


def _moe_with_key_sums(x2, xn, cmb, w13, w2, gfin, tm, cache4, page_table, qb):
    t = x2.shape[0]
    nb, npages = page_table.shape
    steps_per_seq = npages // PAGES_PER_STEP
    n_tiles = t // tm
    n_esteps = N_EXPERTS // EXPERTS_PER_STEP
    assert n_tiles * n_esteps == nb * steps_per_seq and npages // PAGES_PER_BLOCK <= LANES

    def seq_of(i, e):
        return (i * n_esteps + e) // steps_per_seq

    def pspec(r):
        def imap(i, e, pt):
            g = i * n_esteps + e
            return (pt[g // steps_per_seq, lax.rem(g, steps_per_seq) * PAGES_PER_STEP + r], 0, 0, 0)
        return pl.BlockSpec((1, N_HEADS_A, HEAD_DIM_A, PAGE_SIZE), imap)

    row = lambda n: pl.BlockSpec((tm, n), lambda i, e, pt: (i, 0))
    return pl.pallas_call(
        functools.partial(_moe_stream_kernel, steps_per_seq=steps_per_seq),
        grid_spec=pltpu.PrefetchScalarGridSpec(
            num_scalar_prefetch=1, grid=(n_tiles, n_esteps),
            in_specs=[row(D_MODEL), row(D_MODEL), row(LANES),
                      pl.BlockSpec((EXPERTS_PER_STEP, D_MODEL, 2 * D_FF_EXPERT), lambda i, e, pt: (e, 0, 0)),
                      pl.BlockSpec((EXPERTS_PER_STEP, D_FF_EXPERT, D_MODEL), lambda i, e, pt: (e, 0, 0)),
                      pl.BlockSpec(gfin.shape, lambda i, e, pt: (0, 0)),
                      pl.BlockSpec((1, D_ATTN, LANES), lambda i, e, pt: (seq_of(i, e), 0, 0))]
            + [pspec(r) for r in range(PAGES_PER_STEP)],
            out_specs=[row(D_MODEL),
                       pl.BlockSpec((1, N_HEADS_A, LANES), lambda i, e, pt: (seq_of(i, e), 0, 0))],
            scratch_shapes=[pltpu.VMEM((tm, D_MODEL), F32)]),
        out_shape=[jax.ShapeDtypeStruct((t, D_MODEL), F32), jax.ShapeDtypeStruct((nb, N_HEADS_A, LANES), F32)],
        compiler_params=_cparams("arbitrary", "arbitrary"), name="moe_keysums",
    )(page_table, x2, xn, cmb, w13, w2, gfin, qb, *([cache4] * PAGES_PER_STEP))


def _ssel_kernel(sc_ref, o_ref, *, n_past):
    sc = sc_ref[...] * (1.0 / MOBA_BLOCK)
    bl_i = lax.broadcasted_iota(I32, sc.shape, 1)
    bl = bl_i.astype(F32)
    valid = bl_i < n_past
    taken = jnp.zeros(sc.shape, jnp.bool_)
    out = jnp.zeros(sc.shape, F32)
    for r in range(MOBA_TOPK):
        sm = jnp.where(valid & jnp.logical_not(taken), sc, BIG_NEG)
        mx = jnp.max(sm, axis=1, keepdims=True)
        idx = jnp.min(jnp.where((sm == mx) & (sm > BIG_NEG), bl, NO_IDX), axis=1, keepdims=True)
        taken = taken | (bl == idx)
        out = jnp.where(bl_i == r, idx, out)
    o_ref[...] = out.astype(I32)


def _sample_select(scores, n_past):
    return pl.pallas_call(
        functools.partial(_ssel_kernel, n_past=n_past),
        out_shape=jax.ShapeDtypeStruct(scores.shape, I32), name="sample_select",
    )(scores)


PAGES_PER_HEAD = MOBA_TOPK * PAGES_PER_BLOCK
N_SEL_PAGES = N_HEADS_A * PAGES_PER_HEAD


def _sattn_kernel(sel_ref, pt_ref, q_ref, kn_ref, vn_ref, *refs):
    del sel_ref, pt_ref
    kp, vp, o_ref = refs[:N_SEL_PAGES], refs[N_SEL_PAGES:2 * N_SEL_PAGES], refs[2 * N_SEL_PAGES]
    for h in range(N_HEADS_A):
        qc = q_ref[0][:, h:h + 1] * (HEAD_DIM_A ** -0.5)
        kts = [kp[h * PAGES_PER_HEAD + r][0, 0] for r in range(PAGES_PER_HEAD)]
        vts = [vp[h * PAGES_PER_HEAD + r][0, 0] for r in range(PAGES_PER_HEAD)]
        ss = [jnp.sum(kt * qc, axis=0, keepdims=True) for kt in kts]
        s_self = jnp.sum(qc * kn_ref[0][:, h:h + 1], axis=0, keepdims=True)
        mx = s_self
        for s in ss:
            mx = jnp.maximum(mx, jnp.max(s, axis=1, keepdims=True))
        p_self = jnp.exp(s_self - mx)
        den = p_self
        acc = jnp.zeros((HEAD_DIM_A, PAGE_SIZE), F32)
        for s, vt in zip(ss, vts):
            p = jnp.exp(s - mx)
            den = den + jnp.sum(p, axis=1, keepdims=True)
            acc = acc + vt * p
        out = p_self * vn_ref[0][:, h:h + 1] + jnp.sum(acc, axis=1, keepdims=True)
        o_ref[0, :, h:h + 1] = out / den


def _sample_attn(sel_flat, pt_flat, q3, kn3, vn3, ck4, cv4, n_pages):
    nb = q3.shape[0]

    def pspec(slot):
        h, rem = divmod(slot, PAGES_PER_HEAD)
        r, half = divmod(rem, PAGES_PER_BLOCK)

        def imap(b, sel, pt):
            blk = sel[(b * N_HEADS_A + h) * MOBA_TOPK + r]
            return (pt[b * n_pages + blk * PAGES_PER_BLOCK + half], h, 0, 0)

        return pl.BlockSpec((1, 1, HEAD_DIM_A, PAGE_SIZE), imap)

    tok = pl.BlockSpec((1, HEAD_DIM_A, N_HEADS_A), lambda b, sel, pt: (b, 0, 0))
    return pl.pallas_call(
        _sattn_kernel,
        grid_spec=pltpu.PrefetchScalarGridSpec(
            num_scalar_prefetch=2, grid=(nb,),
            in_specs=[tok, tok, tok] + [pspec(s) for s in range(N_SEL_PAGES)] * 2,
            out_specs=tok),
        out_shape=jax.ShapeDtypeStruct((nb, HEAD_DIM_A, N_HEADS_A), F32),
        compiler_params=_cparams("arbitrary"), name="sample_attn",
    )(sel_flat, pt_flat, q3, kn3, vn3, *([ck4] * N_SEL_PAGES), *([cv4] * N_SEL_PAGES))


def _tail_prompt(x, attn, sga, sb, mk3, mv3, w, batch, seq, key_stream):
    x2, xn, cmb = _mid(x, attn, sga, sb, mk3, mv3, w, batch, seq, MID_ROWS)
    return _moe_with_key_sums(x2, xn, cmb, w['w13'], w['w2'], w['gfin'], MOE_ROWS, *key_stream)


def _tail_sample(x, attn, sga, sb, mk3, mv3, w):
    t = x.shape[0]
    x1, xq = _merge_sample(x, attn, sga, sb, w['wpa'], w['wo'], w['gx'], w['wxq'])
    xo = _xattn_one(xq.reshape(t, 1, D_XATTN), mk3, mv3).reshape(t, D_XATTN)
    x2, xn, cmb = _post_sample(x1, xo, w['wxo'], w['gf'], w['wr'], w['br'])
    return _moe(x2, xn, cmb, w['w13'], w['w2'], w['gfin'], t)


def kernel(x_prompt, x_sample, mem_prompt, cache_k, cache_v, page_table, state_ssm_re, state_ssm_im,
           cache_mem_k, cache_mem_v, g_mix, w_in, ssm_lambda_re, ssm_lambda_im, ssm_log_dt,
           ssm_b_re, ssm_b_im, ssm_c_re, ssm_c_im, ssm_d, w_glu, w_pa, w_pb, w_o, g_x, g_mem,
           w_xq, w_xk, w_xv, w_xo, g_ffn, w_group, b_group, w_erouter, b_erouter, w1, w3, w2, g_final):
    depth = w_in.shape[0]
    assert depth == 1
    l = 0
    bp, sp, _ = x_prompt.shape
    bs, ss, _ = x_sample.shape
    assert ss == 1
    n_pages = page_table.shape[1]
    past_len = n_pages * PAGE_SIZE
    assert past_len % MOBA_BLOCK == 0 and sp % MOBA_BLOCK == 0

    row = lambda a: a.reshape(1, -1).astype(F32)
    bf = lambda a: a.astype(BF16)
    wr = jnp.zeros((D_MODEL, LANES), F32)
    wr = wr.at[:, :N_EXPERTS].set(w_erouter[l]).at[:, GROUP_LANE0:GROUP_LANE0 + N_EXPERT_GROUPS].set(w_group[l])
    br = jnp.zeros((1, LANES), F32)
    br = br.at[0, :N_EXPERTS].set(b_erouter[l]).at[0, GROUP_LANE0:GROUP_LANE0 + N_EXPERT_GROUPS].set(b_group[l])
    ws = dict(wpa=w_pa[l], wo=w_o[l], gx=row(g_x[l]), wxq=w_xq[l], wxo=w_xo[l], gf=row(g_ffn[l]),
              wr=wr, br=br, w13=bf(jnp.concatenate([w1[l], w3[l]], axis=-1)), w2=bf(w2[l]),
              gfin=row(g_final))
    wp = dict(ws, wpa=bf(w_pa[l]), wo=bf(w_o[l]), wxq=bf(w_xq[l]), wxo=bf(w_xo[l]))
    gmix = row(g_mix[l])

    ar, ai, bbt_re, bbt_im = _ssm_params(ssm_lambda_re[l], ssm_lambda_im[l], ssm_log_dt[l], ssm_b_re[l],
                                         ssm_b_im[l])
    bb, cc = _s5_weights(ar, ai, bbt_re, bbt_im, ssm_c_re[l], ssm_c_im[l])
    d_row = row(ssm_d[l])

    tp = bp * sp
    xp = x_prompt.reshape(tp, D_MODEL)
    q, kt, vt, kb, vb, u, sga, gb = _inproj(xp, gmix, bf(w_in[l]), INPROJ_ROWS, bp)
    attn = _moba_prompt(q, kb, vb, bp, sp)
    h0 = jnp.zeros((bp, 2 * N_STATE), F32)
    sb3, ht = _s5(u.reshape(bp, sp, D_SSM), gb.reshape(bp, sp, D_MODEL), h0, ar, ai, bf(bb), bf(cc), d_row,
                  bf(w_glu[l]), bf(w_pb[l]), bn=bp, lc=S5_STEPS, precise=False)
    sr_p, si_p = _lanes_to_state(ht)
    mk_p, mv_p = _memkv(mem_prompt.reshape(bp * N_MEM, D_MODEL), row(g_mem[l]), bf(w_xk[l]), bf(w_xv[l]), MEMKV_ROWS)
    ck4 = jnp.transpose(cache_k[l], (0, 2, 3, 1))
    cv4 = jnp.transpose(cache_v[l], (0, 2, 3, 1))
    xs = x_sample.reshape(bs, D_MODEL)
    proj_s = _inproj_precise(xs, gmix, w_in[l])
    q_s, k_s, v_s = (proj_s[:, D_ATTN * n:D_ATTN * (n + 1)] for n in range(3))
    u_s = proj_s[:, 3 * D_ATTN:GA_COL0]
    sga_s = proj_s[:, GA_COL0:GA_COL0 + D_MODEL]
    gb_s = proj_s[:, GA_COL0 + D_MODEL:]
    qb = jnp.broadcast_to(q_s[:, :, None], (bs, D_ATTN, LANES))
    y_p, scores = _tail_prompt(xp, attn, sga, sb3.reshape(tp, D_MODEL), mk_p.reshape(bp, N_MEM, D_XATTN),
                               mv_p.reshape(bp, N_MEM, D_XATTN), wp, bp, sp, (ck4, page_table, qb))

    sel = _sample_select(scores.reshape(bs * N_HEADS_A, LANES), past_len // MOBA_BLOCK)
    sel_flat = sel[:, :MOBA_TOPK].reshape(-1)
    hsplit = lambda a: a.reshape(bs, N_HEADS_A, HEAD_DIM_A).transpose(0, 2, 1)
    attn_s = _sample_attn(sel_flat, page_table.reshape(-1), hsplit(q_s), hsplit(k_s), hsplit(v_s), ck4, cv4,
                          n_pages).transpose(0, 2, 1)
    h0_s = _state_to_lanes(state_ssm_re[l].reshape(bs, N_STATE), state_ssm_im[l].reshape(bs, N_STATE))
    sb_s, ht_s = _s5(u_s.reshape(1, bs, D_SSM), gb_s.reshape(1, bs, D_MODEL), h0_s, ar, ai, bb, cc, d_row,
                     w_glu[l], w_pb[l], bn=bs, lc=1, precise=True)
    sr_s, si_s = _lanes_to_state(ht_s)
    y_s = _tail_sample(xs, attn_s.reshape(bs, D_ATTN), sga_s, sb_s.reshape(bs, D_MODEL),
                       cache_mem_k[l].reshape(bs, N_MEM, D_XATTN), cache_mem_v[l].reshape(bs, N_MEM, D_XATTN), ws)

    kv5 = lambda a, b, s: a.reshape(1, b, s, N_HEADS_A, HEAD_DIM_A)
    kvt5 = lambda a: a.reshape(bp, N_HEADS_A, HEAD_DIM_A, sp).transpose(0, 3, 1, 2)[None]
    st4 = lambda a: a[None]
    mem5 = lambda a: a.reshape(1, bp, N_MEM, N_HEADS_X, HEAD_DIM_X)
    return (y_p.reshape(bp, sp, D_MODEL), y_s.reshape(bs, 1, D_MODEL),
            kvt5(kt), kvt5(vt), st4(sr_p), st4(si_p), mem5(mk_p), mem5(mv_p),
            kv5(k_s, bs, 1), kv5(v_s, bs, 1), st4(sr_s), st4(si_s))
```

```python
import functools

import jax
import jax.numpy as jnp
from jax import lax
from jax.experimental import pallas as pl
from jax.experimental.pallas import tpu as pltpu

F32 = jnp.float32
BF16 = jnp.bfloat16
I32 = jnp.int32

D_MODEL = 1024
N_HEADS_A = 8
HEAD_DIM_A = 64
D_ATTN = N_HEADS_A * HEAD_DIM_A
MOBA_BLOCK = 256
BLOCK_SHIFT = MOBA_BLOCK.bit_length() - 1
MOBA_TOPK = 3
D_SSM = 512
SSM_GROUP = 16
N_SSM_GROUPS = D_SSM // SSM_GROUP
SSM_STATE = 64
N_STATE = N_SSM_GROUPS * SSM_STATE
N_MEM = 256
N_HEADS_X = 4
HEAD_DIM_X = 128
D_XATTN = N_HEADS_X * HEAD_DIM_X
N_EXPERT_GROUPS = 4
EXPERTS_PER_GROUP = 4
N_EXPERTS = N_EXPERT_GROUPS * EXPERTS_PER_GROUP
D_FF_EXPERT = 256
D_IN_PROJ = 3 * D_ATTN + D_SSM + 2 * D_MODEL
RMS_EPS = 1e-6
NEG_INF = -1e30
PAGE_SIZE = 128

LANES = 128
VMEM_LIMIT = 56 * 1024 * 1024
BIG_NEG = -3e38
NO_IDX = 1e9

INPROJ_ROWS = 512
S5_STEPS = 128
MEMKV_ROWS = 256
MID_ROWS = 512
MOE_ROWS = 1024


def _cparams(*sem):
    return pltpu.CompilerParams(dimension_semantics=sem, vmem_limit_bytes=VMEM_LIMIT)


def _rms(x, g):
    return x * lax.rsqrt(jnp.mean(x * x, axis=-1, keepdims=True) + RMS_EPS) * g


def _dot(a, b):
    return jnp.dot(a, b, preferred_element_type=F32)


def _dot_nt(a, b):
    return lax.dot_general(a, b, (((1,), (1,)), ((), ())), preferred_element_type=F32)


def _split_bf16(x):
    hi = x.astype(BF16)
    return hi, (x - hi.astype(F32)).astype(BF16)


def _split3_bf16(x):
    hi, rest = x.astype(BF16), None
    rest = x - hi.astype(F32)
    mid = rest.astype(BF16)
    return hi, mid, (rest - mid.astype(F32)).astype(BF16)


def _mm(a, w, precise, nt=False):
    dot = _dot_nt if nt else _dot
    if not precise:
        return dot(a.astype(BF16), w.astype(BF16))
    ah, am, al = _split3_bf16(a.astype(F32))
    wh, wm, wl = _split3_bf16(w)
    return ((dot(al, wh) + dot(ah, wl)) + (dot(am, wm) + dot(am, wh) + dot(ah, wm))) + dot(ah, wh)


def _inproj_kernel(x_ref, g_ref, w_ref, q_ref, k_ref, v_ref, kb_ref, vb_ref, u_ref, sga_ref, gb_ref):
    xn = _rms(x_ref[...], g_ref[...]).astype(BF16)

    def mm(c0, n):
        return _dot(xn, w_ref[:, c0:c0 + n])

    q_ref[...] = mm(0, D_ATTN)
    k = mm(D_ATTN, D_ATTN)
    k_ref[0] = k.T
    kb_ref[...] = k.astype(BF16)
    v = mm(2 * D_ATTN, D_ATTN)
    v_ref[0] = v.T
    vb_ref[...] = v.astype(BF16)
    u_ref[...] = mm(3 * D_ATTN, D_SSM)
    sga_ref[...] = jax.nn.sigmoid(mm(3 * D_ATTN + D_SSM, D_MODEL)).astype(BF16)
    gb_ref[...] = mm(3 * D_ATTN + D_SSM + D_MODEL, D_MODEL)


def _inproj(x, g, w_bf, tm, batch):
    t = x.shape[0]
    seq = t // batch
    tiles = seq // tm
    row = lambda n: pl.BlockSpec((tm, n), lambda i: (i, 0))
    col = pl.BlockSpec((1, D_ATTN, tm), lambda i: (i // tiles, 0, i % tiles))
    full = lambda a: pl.BlockSpec(a.shape, lambda i: (0,) * a.ndim)
    shp = lambda n, dt: jax.ShapeDtypeStruct((t, n), dt)
    tshp = jax.ShapeDtypeStruct((batch, D_ATTN, seq), F32)
    return pl.pallas_call(
        _inproj_kernel,
        grid=(t // tm,),
        in_specs=[row(D_MODEL), full(g), full(w_bf)],
        out_specs=[row(D_ATTN), col, col, row(D_ATTN), row(D_ATTN), row(D_SSM), row(D_MODEL), row(D_MODEL)],
        out_shape=[shp(D_ATTN, F32), tshp, tshp, shp(D_ATTN, BF16),
                   shp(D_ATTN, BF16), shp(D_SSM, F32), shp(D_MODEL, BF16), shp(D_MODEL, F32)],
        compiler_params=_cparams("parallel"),
        name="inproj",
    )(x, g, w_bf)


INPROJ_COLS = 512
GA_COL0 = 3 * D_ATTN + D_SSM


def _inproj_precise_kernel(x_ref, g_ref, w_ref, o_ref):
    c0 = pl.program_id(0) * INPROJ_COLS
    acc = _mm(_rms(x_ref[...], g_ref[...]), w_ref[...], True)
    is_ga = (c0 >= GA_COL0) & (c0 < GA_COL0 + D_MODEL)

    @pl.when(is_ga)
    def _():
        o_ref[...] = jax.nn.sigmoid(acc)

    @pl.when(jnp.logical_not(is_ga))
    def _():
        o_ref[...] = acc


def _inproj_precise(x, g, w):
    t = x.shape[0]
    full = lambda a: pl.BlockSpec(a.shape, lambda j: (0,) * a.ndim)
    return pl.pallas_call(
        _inproj_precise_kernel,
        grid=(D_IN_PROJ // INPROJ_COLS,),
        in_specs=[full(x), full(g), pl.BlockSpec((D_MODEL, INPROJ_COLS), lambda j: (0, j))],
        out_specs=pl.BlockSpec((t, INPROJ_COLS), lambda j: (0, j)),
        out_shape=jax.ShapeDtypeStruct((t, D_IN_PROJ), F32),
        compiler_params=_cparams("parallel"),
        name="inproj_sample",
    )(x, g, w)


def _ssm_param_kernel(lr_ref, li_ref, dt_ref, br_ref, bi_ref, ar_ref, ai_ref, bbr_ref, bbi_ref):
    lr = jnp.minimum(lr_ref[...], -1e-4)
    li = li_ref[...]
    dt = jnp.exp(dt_ref[...])
    mag = jnp.exp(lr * dt)
    ar = mag * jnp.cos(li * dt)
    ai = mag * jnp.sin(li * dt)
    den = lr * lr + li * li
    nr = ar - 1.0
    cr = (nr * lr + ai * li) / den
    ci = (ai * lr - nr * li) / den
    ar_ref[...] = ar
    ai_ref[...] = ai
    br = br_ref[...]
    bi = bi_ref[...]
    bbr_ref[...] = cr * br - ci * bi
    bbi_ref[...] = cr * bi + ci * br


def _ssm_params(lam_re, lam_im, log_dt, b_re, b_im):
    n = N_STATE
    lr = lam_re.reshape(1, n)
    li = lam_im.reshape(1, n)
    dt = jnp.broadcast_to(log_dt[:, None], (N_SSM_GROUPS, SSM_STATE)).reshape(1, n)
    brt = b_re.reshape(n, SSM_GROUP).T
    bit = b_im.reshape(n, SSM_GROUP).T
    row = jax.ShapeDtypeStruct((1, n), F32)
    mat = jax.ShapeDtypeStruct((SSM_GROUP, n), F32)
    return pl.pallas_call(_ssm_param_kernel, out_shape=[row, row, mat, mat], name="ssm_params")(
        lr, li, dt, brt, bit)


HALF_STATE = N_STATE // 2
SCAN_COLS = 512
SCAN_TILES = SCAN_COLS // 128


def _s5_kernel(u_ref, gb_ref, h0_ref, ar_ref, ai_ref, bb_ref, cc_ref, d_ref, wglu_ref, wpb_ref, *refs,
               bn, lc, precise):
    m = bn * lc
    if lc > 1:
        perm_ref, sb_ref, ht_ref, s_ref, yt_ref, carry_ref = refs
    else:
        sb_ref, ht_ref, s_ref, yt_ref, carry_ref = refs

    @pl.when(pl.program_id(0) == 0)
    def _():
        carry_ref[...] = h0_ref[...]

    u = u_ref[...].reshape(m, D_SSM)
    us = _dot(perm_ref[...], u.astype(BF16)).astype(BF16) if lc > 1 else u
    for j in range(16):
        sec, jj = divmod(j, 4)
        gbase = (sec // 2) * 16 + jj * 4
        lt = (gbase * SSM_GROUP) // LANES
        bu = _mm(us[:, LANES * lt:LANES * (lt + 1)], bb_ref[j], precise)
        s_ref[2 * j] = bu[:, :LANES]
        s_ref[2 * j + 1] = bu[:, LANES:]

    tiles_half = HALF_STATE // LANES
    for h in range(2):
        for c in range(HALF_STATE // SCAN_COLS):
            re_t = [2 * tiles_half * h + SCAN_TILES * c + n for n in range(SCAN_TILES)]
            im_t = [t + tiles_half for t in re_t]
            a_t = [tiles_half * h + SCAN_TILES * c + n for n in range(SCAN_TILES)]
            ars = [jnp.broadcast_to(ar_ref[:, LANES * t:LANES * (t + 1)], (bn, LANES)) for t in a_t]
            ais = [jnp.broadcast_to(ai_ref[:, LANES * t:LANES * (t + 1)], (bn, LANES)) for t in a_t]

            def body(t, carry, re_t=re_t, im_t=im_t, ars=ars, ais=ais):
                rows = pl.ds(pl.multiple_of(t * bn, bn), bn)
                out = []
                for n in range(SCAN_TILES):
                    xr, xi = carry[2 * n], carry[2 * n + 1]
                    nxr = ars[n] * xr - ais[n] * xi + s_ref[re_t[n], rows, :]
                    nxi = ars[n] * xi + ais[n] * xr + s_ref[im_t[n], rows, :]
                    s_ref[re_t[n], rows, :] = nxr
                    s_ref[im_t[n], rows, :] = nxi
                    out += [nxr, nxi]
                return tuple(out)

            x0 = []
            for n in range(SCAN_TILES):
                x0 += [carry_ref[:, LANES * re_t[n]:LANES * (re_t[n] + 1)],
                       carry_ref[:, LANES * im_t[n]:LANES * (im_t[n] + 1)]]
            xs = lax.fori_loop(0, lc, body, tuple(x0), unroll=min(lc, 8))
            for n in range(SCAN_TILES):
                carry_ref[:, LANES * re_t[n]:LANES * (re_t[n] + 1)] = xs[2 * n]
                carry_ref[:, LANES * im_t[n]:LANES * (im_t[n] + 1)] = xs[2 * n + 1]
    ht_ref[...] = carry_ref[...]

    ys = []
    for h in range(2):
        sdt = F32 if precise else BF16
        xh = jnp.concatenate([s_ref[2 * tiles_half * h + n].astype(sdt) for n in range(2 * tiles_half)], axis=1)
        ys.append(_mm(xh, cc_ref[h], precise))
    y = jnp.concatenate(ys, axis=1)
    if lc > 1:
        for n in range(D_SSM // LANES):
            yt_ref[n] = y[:, LANES * n:LANES * (n + 1)]
        y = jnp.concatenate(
            [jnp.concatenate([yt_ref[n, pl.ds(b, lc, stride=bn), :] for n in range(D_SSM // LANES)], axis=1)
             for b in range(bn)], axis=0)
    y = y + d_ref[...] * u
    z = jax.nn.gelu(y)
    s5 = z * jax.nn.sigmoid(_mm(z, wglu_ref[...], precise))
    pb = _mm(s5, wpb_ref[...], precise)
    gb = gb_ref[...].reshape(m, D_MODEL)
    sb_ref[...] = (jax.nn.sigmoid(gb) * pb).astype(sb_ref.dtype).reshape(sb_ref.shape)


def _s5(u3, gb3, h0, ar, ai, bb, cc, d, wglu, wpb, bn, lc, precise):
    nb, s, _ = u3.shape
    rows = bn * lc // nb
    nchunk = s // rows
    full = lambda a: pl.BlockSpec(a.shape, lambda c: (0,) * a.ndim)
    blk = lambda n: pl.BlockSpec((nb, rows, n), lambda c: (0, c, 0))
    perms = []
    if lc > 1:
        r = jnp.arange(bn * lc)
        perms = [(((r % bn) * lc + r // bn)[:, None] == r[None, :]).astype(BF16)]
    return pl.pallas_call(
        functools.partial(_s5_kernel, bn=bn, lc=lc, precise=precise),
        grid=(nchunk,),
        in_specs=[blk(D_SSM), blk(D_MODEL), full(h0), full(ar), full(ai), full(bb), full(cc), full(d),
                  full(wglu), full(wpb)] + [full(p) for p in perms],
        out_specs=[blk(D_MODEL), full(h0)],
        out_shape=[jax.ShapeDtypeStruct((nb, s, D_MODEL), F32 if precise else BF16),
                   jax.ShapeDtypeStruct(h0.shape, F32)],
        scratch_shapes=[pltpu.VMEM((2 * N_STATE // LANES, bn * lc, LANES), F32),
                        pltpu.VMEM((D_SSM // LANES, bn * lc, LANES), F32),
                        pltpu.VMEM((bn, 2 * N_STATE), F32)],
        compiler_params=_cparams("arbitrary"),
        name="s5",
    )(u3, gb3, h0, ar, ai, bb, cc, d, wglu, wpb, *perms)


def _s5_weights(bbt_re, bbt_im, c_re, c_im):
    g, p, h = N_SSM_GROUPS, SSM_STATE, SSM_GROUP
    same_group = (jnp.arange(g * h)[:, None] // h) == (jnp.arange(g * p)[None, :] // p)

    def bfull(bt):
        return jnp.where(same_group, jnp.tile(bt, (g, 1)), 0.0)

    bre, bim = bfull(bbt_re), bfull(bbt_im)
    tiles = []
    for j in range(16):
        sec, jj = divmod(j, 4)
        src = bre if sec % 2 == 0 else bim
        gbase = (sec // 2) * 16 + jj * 4
        lt = (gbase * h) // LANES
        tiles.append(src[LANES * lt:LANES * (lt + 1), gbase * p:(gbase + 4) * p])
    bb = jnp.stack(tiles)

    def cfull(c):
        ct = c.transpose(0, 2, 1).reshape(g * p, h)
        return jnp.where(same_group.T, jnp.tile(ct, (1, g)), 0.0)

    cre, cim = cfull(c_re), cfull(c_im)
    halves = []
    for hh in range(2):
        rs = slice(HALF_STATE * hh, HALF_STATE * (hh + 1))
        cs = slice(256 * hh, 256 * (hh + 1))
        halves.append(jnp.concatenate([cre[rs, cs], -cim[rs, cs]], axis=0))
    return bb, jnp.stack(halves)


def _state_to_lanes(re, im):
    b = re.shape[0]
    r = re.reshape(b, 2, HALF_STATE)
    i = im.reshape(b, 2, HALF_STATE)
    return jnp.concatenate([r[:, 0], i[:, 0], r[:, 1], i[:, 1]], axis=1)


def _lanes_to_state(h):
    b = h.shape[0]
    h4 = h.reshape(b, 4, HALF_STATE)
    re = jnp.concatenate([h4[:, 0], h4[:, 2]], axis=1).reshape(b, N_SSM_GROUPS, SSM_STATE)
    im = jnp.concatenate([h4[:, 1], h4[:, 3]], axis=1).reshape(b, N_SSM_GROUPS, SSM_STATE)
    return re, im


MOBA_CHUNK = 4


def _moba_kernel(q_ref, k_ref, v_ref, o_ref, ka0_ref, ka1_ref, selb_ref, *, seq):
    blk, hd = MOBA_BLOCK, HEAD_DIM_A
    nblk = seq // blk
    chunk_rows = MOBA_CHUNK * blk

    k = k_ref[...]
    rblk = lax.broadcasted_iota(I32, (seq, LANES), 0) >> BLOCK_SHIFT
    lane = lax.broadcasted_iota(I32, (seq, LANES), 1)
    kf = k.astype(F32)
    ka0_ref[...] = jnp.where(lane < hd, kf, jnp.where(lane - hd == rblk, 1.0, 0.0)).astype(BF16)
    ka1_ref[...] = jnp.where(lane >= hd, kf, jnp.where(lane == rblk, 1.0, 0.0)).astype(BF16)
    r = lax.broadcasted_iota(I32, (LANES, seq), 0)
    cblk = lax.broadcasted_iota(I32, (LANES, seq), 1) >> BLOCK_SHIFT
    ind = jnp.where((r == cblk) | (r - hd == cblk), 1.0, 0.0).astype(BF16)
    kmean = _dot(ind, k) * (1.0 / blk)
    rr = lax.broadcasted_iota(I32, (LANES, LANES), 0)
    ll = lax.broadcasted_iota(I32, (LANES, LANES), 1)
    keep = ((rr < hd) & (ll >= hd)) | ((rr >= hd) & (ll < hd))
    ahi, alo = _split_bf16(jnp.where(keep, kmean, 0.0))

    qhi, qlo = _split_bf16(q_ref[...])
    sc = _dot_nt(qhi, ahi) + _dot_nt(qlo, ahi) + _dot_nt(qhi, alo)
    sct = sc.T
    blk_i = lax.broadcasted_iota(I32, (nblk, seq), 0)
    blk_f = blk_i.astype(F32)
    own = lax.broadcasted_iota(I32, (nblk, seq), 1) >> BLOCK_SHIFT
    valid = blk_i < own
    biases = []
    for x in (sct[0:nblk], sct[hd:hd + nblk]):
        taken = jnp.zeros((nblk, seq), jnp.bool_)
        for _ in range(MOBA_TOPK):
            sm = jnp.where(valid & jnp.logical_not(taken), x, BIG_NEG)
            mx = jnp.max(sm, axis=0, keepdims=True)
            idx = jnp.min(jnp.where((sm == mx) & (sm > BIG_NEG), blk_f, NO_IDX), axis=0, keepdims=True)
            taken = taken | (blk_f == idx)
        biases.append(jnp.where(taken | (blk_i == own), 0.0, NEG_INF))
    pad = jnp.zeros((hd - nblk, seq), F32)
    selb_ref[...] = jnp.concatenate([biases[0], pad, biases[1], pad], axis=0).T.astype(BF16)

    lane_b = lax.broadcasted_iota(I32, (blk, LANES), 1)
    col_minus_row = (lax.broadcasted_iota(I32, (blk, chunk_rows), 1)
                     - lax.broadcasted_iota(I32, (blk, chunk_rows), 0))
    heads = ((ka0_ref, lane_b < hd), (ka1_ref, lane_b >= hd))
    chunk_shift = MOBA_CHUNK.bit_length() - 1

    def qblock(i, _):
        off = pl.multiple_of(i * blk, blk)
        qs = q_ref[pl.ds(off, blk), :] * (hd ** -0.5)
        sb = selb_ref[pl.ds(off, blk), :].astype(F32)
        qas = [jnp.where(mine, qs, sb).astype(BF16) for _, mine in heads]

        dc = i >> chunk_shift
        thr = (i - (dc << chunk_shift)) * blk

        def attend(n_chunks):
            def run():
                outs = []
                for (ka_ref, _), qa in zip(heads, qas):
                    ss = [_dot_nt(qa, ka_ref[c * chunk_rows:(c + 1) * chunk_rows, :]) for c in range(n_chunks)]
                    ss[-1] = jnp.where(col_minus_row <= thr, ss[-1], NEG_INF)
                    m = functools.reduce(jnp.maximum, [jnp.max(s, axis=1, keepdims=True) for s in ss])
                    ps = [jnp.exp(s - m) for s in ss]
                    l = functools.reduce(lambda a, b: a + b, [jnp.sum(p, axis=1, keepdims=True) for p in ps])
                    pv = _dot(jnp.concatenate([p.astype(BF16) for p in ps], axis=1),
                              v_ref[0:n_chunks * chunk_rows, :])
                    outs.append(pv / l)
                return jnp.where(lane_b < hd, outs[0], outs[1]).astype(BF16)
            return run

        o_ref[pl.ds(off, blk), :] = lax.switch(dc, [attend(n) for n in range(1, nblk // MOBA_CHUNK + 1)])
        return 0

    lax.fori_loop(0, nblk, qblock, 0)


def _moba_prompt(q, kb, vb, batch, seq):
    spec = pl.BlockSpec((seq, LANES), lambda b, hp: (b, hp))
    return pl.pallas_call(
        functools.partial(_moba_kernel, seq=seq),
        grid=(batch, D_ATTN // LANES),
        in_specs=[spec, spec, spec],
        out_specs=spec,
        out_shape=jax.ShapeDtypeStruct((batch * seq, D_ATTN), BF16),
        scratch_shapes=[pltpu.VMEM((seq, LANES), BF16)] * 3,
        compiler_params=_cparams("parallel", "parallel"),
        name="moba_prompt",
    )(q, kb, vb)


def _memkv_kernel(m_ref, g_ref, wk_ref, wv_ref, mk_ref, mv_ref):
    mn = _rms(m_ref[...], g_ref[...]).astype(BF16)
    mk_ref[...] = _dot(mn, wk_ref[...])
    mv_ref[...] = _dot(mn, wv_ref[...])


def _memkv(mem, g, wk, wv, tm):
    t = mem.shape[0]
    row = lambda n: pl.BlockSpec((tm, n), lambda i: (i, 0))
    full = lambda a: pl.BlockSpec(a.shape, lambda i: (0,) * a.ndim)
    shp = jax.ShapeDtypeStruct((t, D_XATTN), F32)
    return pl.pallas_call(
        _memkv_kernel, grid=(t // tm,),
        in_specs=[row(D_MODEL), full(g), full(wk), full(wv)],
        out_specs=[row(D_XATTN)] * 2, out_shape=[shp, shp],
        compiler_params=_cparams("parallel"), name="memkv",
    )(mem, g, wk, wv)


def _merge_kernel(x_ref, at_ref, sga_ref, sb_ref, wpa_ref, wo_ref, gx_ref, wxq_ref, x1_ref, xq_ref, *, precise):
    pa = _mm(at_ref[...], wpa_ref[...], precise)
    merged = sga_ref[...].astype(F32) * pa + sb_ref[...].astype(F32)
    x1 = x_ref[...] + _mm(merged, wo_ref[...], precise)
    x1_ref[...] = x1
    xq_ref[...] = _mm(_rms(x1, gx_ref[...]), wxq_ref[...], precise).astype(xq_ref.dtype)


def _merge_sample(x, attn, sga, sb, wpa, wo, gx, wxq):
    t = x.shape[0]
    full = lambda a: pl.BlockSpec(a.shape, lambda i: (0,) * a.ndim)
    args = (x, attn, sga, sb, wpa, wo, gx, wxq)
    return pl.pallas_call(
        functools.partial(_merge_kernel, precise=True), grid=(1,),
        in_specs=[full(a) for a in args],
        out_specs=[pl.BlockSpec((t, D_MODEL), lambda i: (0, 0)), pl.BlockSpec((t, D_XATTN), lambda i: (0, 0))],
        out_shape=[jax.ShapeDtypeStruct((t, D_MODEL), F32), jax.ShapeDtypeStruct((t, D_XATTN), F32)],
        compiler_params=_cparams("arbitrary"), name="merge_sample",
    )(*args)


def _xattn_kernel(q_ref, mk_ref, mv_ref, o_ref):
    q = q_ref[0]
    mk = mk_ref[0].astype(BF16)
    mv = mv_ref[0].astype(BF16)
    outs = []
    for h in range(N_HEADS_X):
        cs = slice(HEAD_DIM_X * h, HEAD_DIM_X * (h + 1))
        s = _dot_nt(q[:, cs], mk[:, cs]) * (HEAD_DIM_X ** -0.5)
        e = jnp.exp(s - jnp.max(s, axis=1, keepdims=True))
        p = e / jnp.sum(e, axis=1, keepdims=True)
        outs.append(_dot(p.astype(BF16), mv[:, cs]))
    o_ref[0] = jnp.concatenate(outs, axis=1).astype(o_ref.dtype)


def _xattn_one_kernel(q_ref, mk_ref, mv_ref, o_ref):
    q = q_ref[...]
    s = jnp.sum(mk_ref[0] * q, axis=-1, keepdims=True) * (HEAD_DIM_X ** -0.5)
    e = jnp.exp(s - jnp.max(s, axis=0, keepdims=True))
    p = e / jnp.sum(e, axis=0, keepdims=True)
    o_ref[0] = jnp.sum(p * mv_ref[0], axis=0)


def _xattn_one(q3, mk4, mv4):
    b = q3.shape[0]
    qspec = pl.BlockSpec((1, N_HEADS_X, HEAD_DIM_X), lambda bi: (bi, 0, 0))
    mspec = pl.BlockSpec((1, N_MEM, N_HEADS_X, HEAD_DIM_X), lambda bi: (bi, 0, 0, 0))
    return pl.pallas_call(
        _xattn_one_kernel, grid=(b,), in_specs=[qspec, mspec, mspec], out_specs=qspec,
        out_shape=jax.ShapeDtypeStruct(q3.shape, F32),
        compiler_params=_cparams("parallel"), name="xattn_one",
    )(q3, mk4, mv4)


GROUP_LANE0 = N_EXPERTS


def _post_kernel(x1_ref, xo_ref, wxo_ref, gf_ref, wr_ref, br_ref, x2_ref, xn_ref, cmb_ref, *, precise):
    x2 = x1_ref[...] + _mm(xo_ref[...], wxo_ref[...], precise)
    x2_ref[...] = x2
    t = _rms(x2, gf_ref[...])
    thi, tlo = _split_bf16(t)
    xn_ref[...] = thi
    if precise:
        logits = _mm(t, wr_ref[...], True) + br_ref[...]
    else:
        wrh, wrl = _split_bf16(wr_ref[...])
        logits = _dot(thi, wrh) + _dot(tlo, wrh) + _dot(thi, wrl) + br_ref[...]

    lane_i = lax.broadcasted_iota(I32, logits.shape, 1)
    lane = lane_i.astype(F32)
    lane_group = (lane_i >> (EXPERTS_PER_GROUP.bit_length() - 1)).astype(F32)
    isg = (lane_i >= GROUP_LANE0) & (lane_i < GROUP_LANE0 + N_EXPERT_GROUPS)
    gmax = jnp.max(jnp.where(isg, logits, BIG_NEG), axis=1, keepdims=True)
    eg = jnp.where(isg, jnp.exp(jnp.where(isg, logits, gmax) - gmax), 0.0)
    gp = eg / jnp.sum(eg, axis=1, keepdims=True)
    pg = jnp.max(jnp.where(isg, gp, -1.0), axis=1, keepdims=True)
    gi = jnp.min(jnp.where(isg & (gp == pg), lane, NO_IDX), axis=1, keepdims=True) - GROUP_LANE0

    insel = (lane_i < N_EXPERTS) & (lane_group == gi)
    el = jnp.where(insel, logits, BIG_NEG)
    m1 = jnp.max(el, axis=1, keepdims=True)
    i1 = jnp.min(jnp.where(insel & (el == m1), lane, NO_IDX), axis=1, keepdims=True)
    rest = insel & (lane != i1)
    el2 = jnp.where(rest, logits, BIG_NEG)
    m2 = jnp.max(el2, axis=1, keepdims=True)
    i2 = jnp.min(jnp.where(rest & (el2 == m2), lane, NO_IDX), axis=1, keepdims=True)
    e2 = jnp.exp(m2 - m1)
    den = 1.0 + e2
    cmb_ref[...] = jnp.where(lane == i1, (1.0 / den) * pg, jnp.where(lane == i2, (e2 / den) * pg, 0.0))


def _mid_kernel(x_ref, at_ref, sga_ref, sb_ref, mk_ref, mv_ref, wpa_ref, wo_ref, gx_ref, wxq_ref, wxo_ref,
                gf_ref, wr_ref, br_ref, x2_ref, xn_ref, cmb_ref, x1_s, xq_s, xo_s):
    _merge_kernel(x_ref, at_ref, sga_ref, sb_ref, wpa_ref, wo_ref, gx_ref, wxq_ref, x1_s, xq_s.at[0],
                  precise=False)
    _xattn_kernel(xq_s, mk_ref, mv_ref, xo_s)
    _post_kernel(x1_s, xo_s.at[0], wxo_ref, gf_ref, wr_ref, br_ref, x2_ref, xn_ref, cmb_ref, precise=False)


def _mid(x, attn, sga, sb, mk3, mv3, w, batch, seq, tm):
    tiles = seq // tm
    row = lambda n: pl.BlockSpec((tm, n), lambda b, s: (b * tiles + s, 0))
    full = lambda a: pl.BlockSpec(a.shape, lambda b, s: (0,) * a.ndim)
    mspec = pl.BlockSpec((1, N_MEM, D_XATTN), lambda b, s: (b, 0, 0))
    t = batch * seq
    weights = [w[k] for k in ('wpa', 'wo', 'gx', 'wxq', 'wxo', 'gf', 'wr', 'br')]
    return pl.pallas_call(
        _mid_kernel, grid=(batch, tiles),
        in_specs=[row(D_MODEL), row(D_ATTN), row(D_MODEL), row(D_MODEL), mspec, mspec]
        + [full(a) for a in weights],
        out_specs=[row(D_MODEL), row(D_MODEL), row(LANES)],
        out_shape=[jax.ShapeDtypeStruct((t, D_MODEL), F32), jax.ShapeDtypeStruct((t, D_MODEL), BF16),
                   jax.ShapeDtypeStruct((t, LANES), F32)],
        scratch_shapes=[pltpu.VMEM((tm, D_MODEL), F32), pltpu.VMEM((1, tm, D_XATTN), BF16),
                        pltpu.VMEM((1, tm, D_XATTN), BF16)],
        compiler_params=_cparams("parallel", "parallel"), name="mid",
    )(x, attn, sga, sb, mk3, mv3, *weights)


def _post_sample(x1, xo, wxo, gf, wr, br):
    t = x1.shape[0]
    full = lambda a: pl.BlockSpec(a.shape, lambda i: (0,) * a.ndim)
    out = lambda n: pl.BlockSpec((t, n), lambda i: (0, 0))
    args = (x1, xo, wxo, gf, wr, br)
    return pl.pallas_call(
        functools.partial(_post_kernel, precise=True), grid=(1,),
        in_specs=[full(a) for a in args],
        out_specs=[out(D_MODEL), out(D_MODEL), out(LANES)],
        out_shape=[jax.ShapeDtypeStruct((t, D_MODEL), F32), jax.ShapeDtypeStruct((t, D_MODEL), BF16),
                   jax.ShapeDtypeStruct((t, LANES), F32)],
        compiler_params=_cparams("arbitrary"), name="post_sample",
    )(*args)


EXPERTS_PER_STEP = 2
PAGES_PER_STEP = 32
PAGES_PER_BLOCK = MOBA_BLOCK // PAGE_SIZE
BLOCKS_PER_STEP = PAGES_PER_STEP // PAGES_PER_BLOCK


def _block_score_step(s, pages, qb_ref, sc_ref):
    lane = lax.broadcasted_iota(I32, (N_HEADS_A, LANES), 1)
    sc = sc_ref[0]
    qb = qb_ref[0]
    for r in range(BLOCKS_PER_STEP):
        acc = pages[PAGES_PER_BLOCK * r][0].reshape(D_ATTN, PAGE_SIZE)
        for t in range(1, PAGES_PER_BLOCK):
            acc = acc + pages[PAGES_PER_BLOCK * r + t][0].reshape(D_ATTN, PAGE_SIZE)
        prod = (acc * qb).reshape(N_HEADS_A, HEAD_DIM_A // 8, 8, PAGE_SIZE)
        per_head = jnp.sum(jnp.sum(prod, axis=1), axis=1)
        sc = jnp.where(lane == s * BLOCKS_PER_STEP + r, jnp.sum(per_head, axis=1, keepdims=True), sc)
    sc_ref[0] = sc


def _moe_stream_kernel(pt_ref, x2_ref, xn_ref, cmb_ref, w13_ref, w2_ref, gfin_ref, qb_ref, *refs,
                       steps_per_seq):
    del pt_ref
    pages, (y_ref, sc_ref, acc_ref) = refs[:PAGES_PER_STEP], refs[PAGES_PER_STEP:]
    s = lax.rem(pl.program_id(0) * pl.num_programs(1) + pl.program_id(1), steps_per_seq)

    @pl.when(s == 0)
    def _():
        sc_ref[0] = jnp.zeros((N_HEADS_A, LANES), F32)

    _moe_kernel(x2_ref, xn_ref, cmb_ref, w13_ref, w2_ref, gfin_ref, y_ref, acc_ref,
                side_work=functools.partial(_block_score_step, s, pages, qb_ref, sc_ref))


def _moe_kernel(x2_ref, xn_ref, cmb_ref, w13_ref, w2_ref, gfin_ref, y_ref, acc_ref, side_work=None):
    step = pl.program_id(1)

    @pl.when(step == 0)
    def _():
        acc_ref[...] = jnp.zeros_like(acc_ref)

    if side_work is not None:
        side_work()
    xn = xn_ref[...]
    cmb = cmb_ref[...]
    lane = lax.broadcasted_iota(I32, cmb.shape, 1)
    hds = []
    for k in range(EXPERTS_PER_STEP):
        h = _dot(xn, w13_ref[k])
        cw = jnp.sum(jnp.where(lane == step * EXPERTS_PER_STEP + k, cmb, 0.0), axis=1, keepdims=True)
        hds.append((jax.nn.silu(h[:, :D_FF_EXPERT]) * h[:, D_FF_EXPERT:] * cw).astype(BF16))
    w2 = w2_ref[...].reshape(EXPERTS_PER_STEP * D_FF_EXPERT, D_MODEL)
    acc_ref[...] += _dot(jnp.concatenate(hds, axis=1), w2)

    @pl.when(step == pl.num_programs(1) - 1)
    def _():
        y_ref[...] = _rms(x2_ref[...] + acc_ref[...], gfin_ref[...])


def _moe(x2, xn, cmb, w13, w2, gfin, tm):
    t = x2.shape[0]
    row = lambda n: pl.BlockSpec((tm, n), lambda i, e: (i, 0))
    return pl.pallas_call(
        _moe_kernel, grid=(t // tm, N_EXPERTS // EXPERTS_PER_STEP),
        in_specs=[row(D_MODEL), row(D_MODEL), row(LANES),
                  pl.BlockSpec((EXPERTS_PER_STEP, D_MODEL, 2 * D_FF_EXPERT), lambda i, e: (e, 0, 0)),
                  pl.BlockSpec((EXPERTS_PER_STEP, D_FF_EXPERT, D_MODEL), lambda i, e: (e, 0, 0)),
                  pl.BlockSpec(gfin.shape, lambda i, e: (0, 0))],
        out_specs=row(D_MODEL),
        out_shape=jax.ShapeDtypeStruct((t, D_MODEL), F32),
        scratch_shapes=[pltpu.VMEM((tm, D_MODEL), F32)],
        compiler_params=_cparams("parallel", "arbitrary"), name="moe",
    )(x2, xn, cmb, w13, w2, gfin)


def _moe_with_key_sums(x2, xn, cmb, w13, w2, gfin, tm, cache4, page_table, qb):
    t = x2.shape[0]
    nb, npages = page_table.shape
    steps_per_seq = npages // PAGES_PER_STEP
    n_tiles = t // tm
    n_esteps = N_EXPERTS // EXPERTS_PER_STEP
    assert n_tiles * n_esteps == nb * steps_per_seq and npages // PAGES_PER_BLOCK <= LANES

    def seq_of(i, e):
        return (i * n_esteps + e) // steps_per_seq

    def pspec(r):
        def imap(i, e, pt):
            g = i * n_esteps + e
            return (pt[g // steps_per_seq, lax.rem(g, steps_per_seq) * PAGES_PER_STEP + r], 0, 0, 0)
        return pl.BlockSpec((1, N_HEADS_A, HEAD_DIM_A, PAGE_SIZE), imap)

    row = lambda n: pl.BlockSpec((tm, n), lambda i, e, pt: (i, 0))
    return pl.pallas_call(
        functools.partial(_moe_stream_kernel, steps_per_seq=steps_per_seq),
        grid_spec=pltpu.PrefetchScalarGridSpec(
            num_scalar_prefetch=1, grid=(n_tiles, n_esteps),
            in_specs=[row(D_MODEL), row(D_MODEL), row(LANES),
                      pl.BlockSpec((EXPERTS_PER_STEP, D_MODEL, 2 * D_FF_EXPERT), lambda i, e, pt: (e, 0, 0)),
                      pl.BlockSpec((EXPERTS_PER_STEP, D_FF_EXPERT, D_MODEL), lambda i, e, pt: (e, 0, 0)),
                      pl.BlockSpec(gfin.shape, lambda i, e, pt: (0, 0)),
                      pl.BlockSpec((1, D_ATTN, LANES), lambda i, e, pt: (seq_of(i, e), 0, 0))]
            + [pspec(r) for r in range(PAGES_PER_STEP)],
            out_specs=[row(D_MODEL),
                       pl.BlockSpec((1, N_HEADS_A, LANES), lambda i, e, pt: (seq_of(i, e), 0, 0))],
            scratch_shapes=[pltpu.VMEM((tm, D_MODEL), F32)]),
        out_shape=[jax.ShapeDtypeStruct((t, D_MODEL), F32), jax.ShapeDtypeStruct((nb, N_HEADS_A, LANES), F32)],
        compiler_params=_cparams("arbitrary", "arbitrary"), name="moe_keysums",
    )(page_table, x2, xn, cmb, w13, w2, gfin, qb, *([cache4] * PAGES_PER_STEP))


def _ssel_kernel(sc_ref, o_ref, *, n_past):
    sc = sc_ref[...] * (1.0 / MOBA_BLOCK)
    bl_i = lax.broadcasted_iota(I32, sc.shape, 1)
    bl = bl_i.astype(F32)
    valid = bl_i < n_past
    taken = jnp.zeros(sc.shape, jnp.bool_)
    out = jnp.zeros(sc.shape, F32)
    for r in range(MOBA_TOPK):
        sm = jnp.where(valid & jnp.logical_not(taken), sc, BIG_NEG)
        mx = jnp.max(sm, axis=1, keepdims=True)
        idx = jnp.min(jnp.where((sm == mx) & (sm > BIG_NEG), bl, NO_IDX), axis=1, keepdims=True)
        taken = taken | (bl == idx)
        out = jnp.where(bl_i == r, idx, out)
    o_ref[...] = out.astype(I32)


def _sample_select(scores, n_past):
    return pl.pallas_call(
        functools.partial(_ssel_kernel, n_past=n_past),
        out_shape=jax.ShapeDtypeStruct(scores.shape, I32), name="sample_select",
    )(scores)


PAGES_PER_HEAD = MOBA_TOPK * PAGES_PER_BLOCK
N_SEL_PAGES = N_HEADS_A * PAGES_PER_HEAD


def _sattn_kernel(sel_ref, pt_ref, q_ref, kn_ref, vn_ref, *refs):
    del sel_ref, pt_ref
    kp, vp, o_ref = refs[:N_SEL_PAGES], refs[N_SEL_PAGES:2 * N_SEL_PAGES], refs[2 * N_SEL_PAGES]
    for h in range(N_HEADS_A):
        qc = q_ref[0][:, h:h + 1] * (HEAD_DIM_A ** -0.5)
        kts = [kp[h * PAGES_PER_HEAD + r][0, 0] for r in range(PAGES_PER_HEAD)]
        vts = [vp[h * PAGES_PER_HEAD + r][0, 0] for r in range(PAGES_PER_HEAD)]
        ss = [jnp.sum(kt * qc, axis=0, keepdims=True) for kt in kts]
        s_self = jnp.sum(qc * kn_ref[0][:, h:h + 1], axis=0, keepdims=True)
        mx = s_self
        for s in ss:
            mx = jnp.maximum(mx, jnp.max(s, axis=1, keepdims=True))
        p_self = jnp.exp(s_self - mx)
        den = p_self
        acc = jnp.zeros((HEAD_DIM_A, PAGE_SIZE), F32)
        for s, vt in zip(ss, vts):
            p = jnp.exp(s - mx)
            den = den + jnp.sum(p, axis=1, keepdims=True)
            acc = acc + vt * p
        out = p_self * vn_ref[0][:, h:h + 1] + jnp.sum(acc, axis=1, keepdims=True)
        o_ref[0, :, h:h + 1] = out / den


def _sample_attn(sel_flat, pt_flat, q3, kn3, vn3, ck4, cv4, n_pages):
    nb = q3.shape[0]

    def pspec(slot):
        h, rem = divmod(slot, PAGES_PER_HEAD)
        r, half = divmod(rem, PAGES_PER_BLOCK)

        def imap(b, sel, pt):
            blk = sel[(b * N_HEADS_A + h) * MOBA_TOPK + r]
            return (pt[b * n_pages + blk * PAGES_PER_BLOCK + half], h, 0, 0)

        return pl.BlockSpec((1, 1, HEAD_DIM_A, PAGE_SIZE), imap)

    tok = pl.BlockSpec((1, HEAD_DIM_A, N_HEADS_A), lambda b, sel, pt: (b, 0, 0))
    return pl.pallas_call(
        _sattn_kernel,
        grid_spec=pltpu.PrefetchScalarGridSpec(
            num_scalar_prefetch=2, grid=(nb,),
            in_specs=[tok, tok, tok] + [pspec(s) for s in range(N_SEL_PAGES)] * 2,
            out_specs=tok),
        out_shape=jax.ShapeDtypeStruct((nb, HEAD_DIM_A, N_HEADS_A), F32),
        compiler_params=_cparams("arbitrary"), name="sample_attn",
    )(sel_flat, pt_flat, q3, kn3, vn3, *([ck4] * N_SEL_PAGES), *([cv4] * N_SEL_PAGES))


def _tail_prompt(x, attn, sga, sb, mk3, mv3, w, batch, seq, key_stream):
    x2, xn, cmb = _mid(x, attn, sga, sb, mk3, mv3, w, batch, seq, MID_ROWS)
    return _moe_with_key_sums(x2, xn, cmb, w['w13'], w['w2'], w['gfin'], MOE_ROWS, *key_stream)


def _tail_sample(x, attn, sga, sb, mk4, mv4, w):
    t = x.shape[0]
    x1, xq = _merge_sample(x, attn, sga, sb, w['wpa'], w['wo'], w['gx'], w['wxq'])
    xo = _xattn_one(xq.reshape(t, N_HEADS_X, HEAD_DIM_X), mk4, mv4).reshape(t, D_XATTN)
    x2, xn, cmb = _post_sample(x1, xo, w['wxo'], w['gf'], w['wr'], w['br'])
    return _moe(x2, xn, cmb, w['w13'], w['w2'], w['gfin'], t)


def kernel(x_prompt, x_sample, mem_prompt, cache_k, cache_v, page_table, state_ssm_re, state_ssm_im,
           cache_mem_k, cache_mem_v, g_mix, w_in, ssm_lambda_re, ssm_lambda_im, ssm_log_dt,
           ssm_b_re, ssm_b_im, ssm_c_re, ssm_c_im, ssm_d, w_glu, w_pa, w_pb, w_o, g_x, g_mem,
           w_xq, w_xk, w_xv, w_xo, g_ffn, w_group, b_group, w_erouter, b_erouter, w1, w3, w2, g_final):
    depth = w_in.shape[0]
    assert depth == 1
    l = 0
    bp, sp, _ = x_prompt.shape
    bs, ss, _ = x_sample.shape
    assert ss == 1
    n_pages = page_table.shape[1]
    past_len = n_pages * PAGE_SIZE
    assert past_len % MOBA_BLOCK == 0 and sp % MOBA_BLOCK == 0

    row = lambda a: a.reshape(1, -1).astype(F32)
    bf = lambda a: a.astype(BF16)
    wr = jnp.zeros((D_MODEL, LANES), F32)
    wr = wr.at[:, :N_EXPERTS].set(w_erouter[l]).at[:, GROUP_LANE0:GROUP_LANE0 + N_EXPERT_GROUPS].set(w_group[l])
    br = jnp.zeros((1, LANES), F32)
    br = br.at[0, :N_EXPERTS].set(b_erouter[l]).at[0, GROUP_LANE0:GROUP_LANE0 + N_EXPERT_GROUPS].set(b_group[l])
    ws = dict(wpa=w_pa[l], wo=w_o[l], gx=row(g_x[l]), wxq=w_xq[l], wxo=w_xo[l], gf=row(g_ffn[l]),
              wr=wr, br=br, w13=bf(jnp.concatenate([w1[l], w3[l]], axis=-1)), w2=bf(w2[l]),
              gfin=row(g_final))
    wp = dict(ws, wpa=bf(w_pa[l]), wo=bf(w_o[l]), wxq=bf(w_xq[l]), wxo=bf(w_xo[l]))
    gmix = row(g_mix[l])

    ar, ai, bbt_re, bbt_im = _ssm_params(ssm_lambda_re[l], ssm_lambda_im[l], ssm_log_dt[l], ssm_b_re[l],
                                         ssm_b_im[l])
    bb, cc = _s5_weights(bbt_re, bbt_im, ssm_c_re[l], ssm_c_im[l])
    d_row = row(ssm_d[l])

    tp = bp * sp
    xp = x_prompt.reshape(tp, D_MODEL)
    q, kt, vt, kb, vb, u, sga, gb = _inproj(xp, gmix, bf(w_in[l]), INPROJ_ROWS, bp)
    attn = _moba_prompt(q, kb, vb, bp, sp)
    h0 = jnp.zeros((bp, 2 * N_STATE), F32)
    sb3, ht = _s5(u.reshape(bp, sp, D_SSM), gb.reshape(bp, sp, D_MODEL), h0, ar, ai, bf(bb), bf(cc), d_row,
                  bf(w_glu[l]), bf(w_pb[l]), bn=bp, lc=S5_STEPS, precise=False)
    sr_p, si_p = _lanes_to_state(ht)
    mk_p, mv_p = _memkv(mem_prompt.reshape(bp * N_MEM, D_MODEL), row(g_mem[l]), bf(w_xk[l]), bf(w_xv[l]), MEMKV_ROWS)
    ck4 = jnp.transpose(cache_k[l], (0, 2, 3, 1))
    cv4 = jnp.transpose(cache_v[l], (0, 2, 3, 1))
    xs = x_sample.reshape(bs, D_MODEL)
    proj_s = _inproj_precise(xs, gmix, w_in[l])
    q_s, k_s, v_s = (proj_s[:, D_ATTN * n:D_ATTN * (n + 1)] for n in range(3))
    u_s = proj_s[:, 3 * D_ATTN:GA_COL0]
    sga_s = proj_s[:, GA_COL0:GA_COL0 + D_MODEL]
    gb_s = proj_s[:, GA_COL0 + D_MODEL:]
    qb = jnp.broadcast_to(q_s[:, :, None], (bs, D_ATTN, LANES))
    y_p, scores = _tail_prompt(xp, attn, sga, sb3.reshape(tp, D_MODEL), mk_p.reshape(bp, N_MEM, D_XATTN),
                               mv_p.reshape(bp, N_MEM, D_XATTN), wp, bp, sp, (ck4, page_table, qb))

    sel = _sample_select(scores.reshape(bs * N_HEADS_A, LANES), past_len // MOBA_BLOCK)
    sel_flat = sel[:, :MOBA_TOPK].reshape(-1)
    hsplit = lambda a: a.reshape(bs, N_HEADS_A, HEAD_DIM_A).transpose(0, 2, 1)
    attn_s = _sample_attn(sel_flat, page_table.reshape(-1), hsplit(q_s), hsplit(k_s), hsplit(v_s), ck4, cv4,
                          n_pages).transpose(0, 2, 1)
    h0_s = _state_to_lanes(state_ssm_re[l].reshape(bs, N_STATE), state_ssm_im[l].reshape(bs, N_STATE))
    sb_s, ht_s = _s5(u_s.reshape(1, bs, D_SSM), gb_s.reshape(1, bs, D_MODEL), h0_s, ar, ai, bb, cc, d_row,
                     w_glu[l], w_pb[l], bn=bs, lc=1, precise=True)
    sr_s, si_s = _lanes_to_state(ht_s)
    y_s = _tail_sample(xs, attn_s.reshape(bs, D_ATTN), sga_s, sb_s.reshape(bs, D_MODEL),
                       cache_mem_k[l], cache_mem_v[l], ws)

    kv5 = lambda a, b, s: a.reshape(1, b, s, N_HEADS_A, HEAD_DIM_A)
    kvt5 = lambda a: a.reshape(bp, N_HEADS_A, HEAD_DIM_A, sp).transpose(0, 3, 1, 2)[None]
    st4 = lambda a: a[None]
    mem5 = lambda a: a.reshape(1, bp, N_MEM, N_HEADS_X, HEAD_DIM_X)
    return (y_p.reshape(bp, sp, D_MODEL), y_s.reshape(bs, 1, D_MODEL),
            kvt5(kt), kvt5(vt), st4(sr_p), st4(si_p), mem5(mk_p), mem5(mv_p),
            kv5(k_s, bs, 1), kv5(v_s, bs, 1), st4(sr_s), st4(si_s))
```

```python
import functools

import jax
import jax.numpy as jnp
from jax import lax
from jax.experimental import pallas as pl
from jax.experimental.pallas import tpu as pltpu

F32 = jnp.float32
BF16 = jnp.bfloat16
I32 = jnp.int32

D_MODEL = 1024
N_HEADS_A = 8
HEAD_DIM_A = 64
D_ATTN = N_HEADS_A * HEAD_DIM_A
MOBA_BLOCK = 256
BLOCK_SHIFT = MOBA_BLOCK.bit_length() - 1
MOBA_TOPK = 3
D_SSM = 512
SSM_GROUP = 16
N_SSM_GROUPS = D_SSM // SSM_GROUP
SSM_STATE = 64
N_STATE = N_SSM_GROUPS * SSM_STATE
N_MEM = 256
N_HEADS_X = 4
HEAD_DIM_X = 128
D_XATTN = N_HEADS_X * HEAD_DIM_X
N_EXPERT_GROUPS = 4
EXPERTS_PER_GROUP = 4
N_EXPERTS = N_EXPERT_GROUPS * EXPERTS_PER_GROUP
D_FF_EXPERT = 256
D_IN_PROJ = 3 * D_ATTN + D_SSM + 2 * D_MODEL
RMS_EPS = 1e-6
NEG_INF = -1e30
PAGE_SIZE = 128

LANES = 128
VMEM_LIMIT = 56 * 1024 * 1024
BIG_NEG = -3e38
NO_IDX = 1e9

INPROJ_ROWS = 512
S5_STEPS = 128
MEMKV_ROWS = 256
MID_ROWS = 512
MOE_ROWS = 1024


def _cparams(*sem):
    return pltpu.CompilerParams(dimension_semantics=sem, vmem_limit_bytes=VMEM_LIMIT)


def _rms(x, g):
    return x * lax.rsqrt(jnp.mean(x * x, axis=-1, keepdims=True) + RMS_EPS) * g


def _dot(a, b):
    return jnp.dot(a, b, preferred_element_type=F32)


def _dot_nt(a, b):
    return lax.dot_general(a, b, (((1,), (1,)), ((), ())), preferred_element_type=F32)


def _split_bf16(x):
    hi = x.astype(BF16)
    return hi, (x - hi.astype(F32)).astype(BF16)


def _split3_bf16(x):
    hi, rest = x.astype(BF16), None
    rest = x - hi.astype(F32)
    mid = rest.astype(BF16)
    return hi, mid, (rest - mid.astype(F32)).astype(BF16)


def _mm(a, w, precise, nt=False):
    dot = _dot_nt if nt else _dot
    if not precise:
        return dot(a.astype(BF16), w.astype(BF16))
    ah, am, al = _split3_bf16(a.astype(F32))
    wh, wm, wl = _split3_bf16(w)
    return ((dot(al, wh) + dot(ah, wl)) + (dot(am, wm) + dot(am, wh) + dot(ah, wm))) + dot(ah, wh)


def _inproj_kernel(x_ref, g_ref, w_ref, q_ref, k_ref, v_ref, kb_ref, vb_ref, u_ref, sga_ref, gb_ref):
    xn = _rms(x_ref[...], g_ref[...]).astype(BF16)

    def mm(c0, n):
        return _dot(xn, w_ref[:, c0:c0 + n])

    q_ref[...] = mm(0, D_ATTN)
    k = mm(D_ATTN, D_ATTN)
    k_ref[0] = k.T
    kb_ref[...] = k.astype(BF16)
    v = mm(2 * D_ATTN, D_ATTN)
    v_ref[0] = v.T
    vb_ref[...] = v.astype(BF16)
    u_ref[...] = mm(3 * D_ATTN, D_SSM)
    sga_ref[...] = jax.nn.sigmoid(mm(3 * D_ATTN + D_SSM, D_MODEL)).astype(BF16)
    gb_ref[...] = mm(3 * D_ATTN + D_SSM + D_MODEL, D_MODEL)


def _inproj(x, g, w_bf, tm, batch):
    t = x.shape[0]
    seq = t // batch
    tiles = seq // tm
    row = lambda n: pl.BlockSpec((tm, n), lambda i: (i, 0))
    col = pl.BlockSpec((1, D_ATTN, tm), lambda i: (i // tiles, 0, i % tiles))
    full = lambda a: pl.BlockSpec(a.shape, lambda i: (0,) * a.ndim)
    shp = lambda n, dt: jax.ShapeDtypeStruct((t, n), dt)
    tshp = jax.ShapeDtypeStruct((batch, D_ATTN, seq), F32)
    return pl.pallas_call(
        _inproj_kernel,
        grid=(t // tm,),
        in_specs=[row(D_MODEL), full(g), full(w_bf)],
        out_specs=[row(D_ATTN), col, col, row(D_ATTN), row(D_ATTN), row(D_SSM), row(D_MODEL), row(D_MODEL)],
        out_shape=[shp(D_ATTN, F32), tshp, tshp, shp(D_ATTN, BF16),
                   shp(D_ATTN, BF16), shp(D_SSM, F32), shp(D_MODEL, BF16), shp(D_MODEL, F32)],
        compiler_params=_cparams("parallel"),
        name="inproj",
    )(x, g, w_bf)


INPROJ_COLS = 512
GA_COL0 = 3 * D_ATTN + D_SSM


def _inproj_precise_kernel(x_ref, g_ref, w_ref, o_ref):
    c0 = pl.program_id(0) * INPROJ_COLS
    acc = _mm(_rms(x_ref[...], g_ref[...]), w_ref[...], True)
    is_ga = (c0 >= GA_COL0) & (c0 < GA_COL0 + D_MODEL)

    @pl.when(is_ga)
    def _():
        o_ref[...] = jax.nn.sigmoid(acc)

    @pl.when(jnp.logical_not(is_ga))
    def _():
        o_ref[...] = acc


def _inproj_precise(x, g, w):
    t = x.shape[0]
    full = lambda a: pl.BlockSpec(a.shape, lambda j: (0,) * a.ndim)
    return pl.pallas_call(
        _inproj_precise_kernel,
        grid=(D_IN_PROJ // INPROJ_COLS,),
        in_specs=[full(x), full(g), pl.BlockSpec((D_MODEL, INPROJ_COLS), lambda j: (0, j))],
        out_specs=pl.BlockSpec((t, INPROJ_COLS), lambda j: (0, j)),
        out_shape=jax.ShapeDtypeStruct((t, D_IN_PROJ), F32),
        compiler_params=_cparams("parallel"),
        name="inproj_sample",
    )(x, g, w)


def _ssm_param_kernel(lr_ref, li_ref, dt_ref, br_ref, bi_ref, ar_ref, ai_ref, bbr_ref, bbi_ref):
    lr = jnp.minimum(lr_ref[...], -1e-4)
    li = li_ref[...]
    dt = jnp.exp(dt_ref[...])
    mag = jnp.exp(lr * dt)
    ar = mag * jnp.cos(li * dt)
    ai = mag * jnp.sin(li * dt)
    den = lr * lr + li * li
    nr = ar - 1.0
    cr = (nr * lr + ai * li) / den
    ci = (ai * lr - nr * li) / den
    ar_ref[...] = ar
    ai_ref[...] = ai
    br = br_ref[...]
    bi = bi_ref[...]
    bbr_ref[...] = cr * br - ci * bi
    bbi_ref[...] = cr * bi + ci * br


def _ssm_params(lam_re, lam_im, log_dt, b_re, b_im):
    n = N_STATE
    lr = lam_re.reshape(1, n)
    li = lam_im.reshape(1, n)
    dt = jnp.broadcast_to(log_dt[:, None], (N_SSM_GROUPS, SSM_STATE)).reshape(1, n)
    brt = b_re.reshape(n, SSM_GROUP).T
    bit = b_im.reshape(n, SSM_GROUP).T
    row = jax.ShapeDtypeStruct((1, n), F32)
    mat = jax.ShapeDtypeStruct((SSM_GROUP, n), F32)
    return pl.pallas_call(_ssm_param_kernel, out_shape=[row, row, mat, mat], name="ssm_params")(
        lr, li, dt, brt, bit)


HALF_STATE = N_STATE // 2
SCAN_COLS = 512
SCAN_TILES = SCAN_COLS // 128


def _s5_kernel(u_ref, gb_ref, h0_ref, ar_ref, ai_ref, bb_ref, cc_ref, d_ref, wglu_ref, wpb_ref, *refs,
               bn, lc, precise):
    m = bn * lc
    if lc > 1:
        perm_ref, sb_ref, ht_ref, s_ref, yt_ref, carry_ref = refs
    else:
        sb_ref, ht_ref, s_ref, yt_ref, carry_ref = refs

    @pl.when(pl.program_id(0) == 0)
    def _():
        carry_ref[...] = h0_ref[...]

    u = u_ref[...].reshape(m, D_SSM)
    us = _dot(perm_ref[...], u.astype(BF16)).astype(BF16) if lc > 1 else u
    for j in range(16):
        sec, jj = divmod(j, 4)
        gbase = (sec // 2) * 16 + jj * 4
        lt = (gbase * SSM_GROUP) // LANES
        bu = _mm(us[:, LANES * lt:LANES * (lt + 1)], bb_ref[j], precise)
        s_ref[2 * j] = bu[:, :LANES]
        s_ref[2 * j + 1] = bu[:, LANES:]

    tiles_half = HALF_STATE // LANES
    for h in range(2):
        for c in range(HALF_STATE // SCAN_COLS):
            re_t = [2 * tiles_half * h + SCAN_TILES * c + n for n in range(SCAN_TILES)]
            im_t = [t + tiles_half for t in re_t]
            a_t = [tiles_half * h + SCAN_TILES * c + n for n in range(SCAN_TILES)]
            ars = [jnp.broadcast_to(ar_ref[:, LANES * t:LANES * (t + 1)], (bn, LANES)) for t in a_t]
            ais = [jnp.broadcast_to(ai_ref[:, LANES * t:LANES * (t + 1)], (bn, LANES)) for t in a_t]

            def body(t, carry, re_t=re_t, im_t=im_t, ars=ars, ais=ais):
                rows = pl.ds(pl.multiple_of(t * bn, bn), bn)
                out = []
                for n in range(SCAN_TILES):
                    xr, xi = carry[2 * n], carry[2 * n + 1]
                    nxr = ars[n] * xr - ais[n] * xi + s_ref[re_t[n], rows, :]
                    nxi = ars[n] * xi + ais[n] * xr + s_ref[im_t[n], rows, :]
                    s_ref[re_t[n], rows, :] = nxr
                    s_ref[im_t[n], rows, :] = nxi
                    out += [nxr, nxi]
                return tuple(out)

            x0 = []
            for n in range(SCAN_TILES):
                x0 += [carry_ref[:, LANES * re_t[n]:LANES * (re_t[n] + 1)],
                       carry_ref[:, LANES * im_t[n]:LANES * (im_t[n] + 1)]]
            xs = lax.fori_loop(0, lc, body, tuple(x0), unroll=min(lc, 8))
            for n in range(SCAN_TILES):
                carry_ref[:, LANES * re_t[n]:LANES * (re_t[n] + 1)] = xs[2 * n]
                carry_ref[:, LANES * im_t[n]:LANES * (im_t[n] + 1)] = xs[2 * n + 1]
    ht_ref[...] = carry_ref[...]

    ys = []
    for h in range(2):
        sdt = F32 if precise else BF16
        xh = jnp.concatenate([s_ref[2 * tiles_half * h + n].astype(sdt) for n in range(2 * tiles_half)], axis=1)
        ys.append(_mm(xh, cc_ref[h], precise))
    y = jnp.concatenate(ys, axis=1)
    if lc > 1:
        for n in range(D_SSM // LANES):
            yt_ref[n] = y[:, LANES * n:LANES * (n + 1)]
        y = jnp.concatenate(
            [jnp.concatenate([yt_ref[n, pl.ds(b, lc, stride=bn), :] for n in range(D_SSM // LANES)], axis=1)
             for b in range(bn)], axis=0)
    y = y + d_ref[...] * u
    z = jax.nn.gelu(y)
    s5 = z * jax.nn.sigmoid(_mm(z, wglu_ref[...], precise))
    pb = _mm(s5, wpb_ref[...], precise)
    gb = gb_ref[...].reshape(m, D_MODEL)
    sb_ref[...] = (jax.nn.sigmoid(gb) * pb).astype(sb_ref.dtype).reshape(sb_ref.shape)


def _s5(u3, gb3, h0, ar, ai, bb, cc, d, wglu, wpb, bn, lc, precise):
    nb, s, _ = u3.shape
    rows = bn * lc // nb
    nchunk = s // rows
    full = lambda a: pl.BlockSpec(a.shape, lambda c: (0,) * a.ndim)
    blk = lambda n: pl.BlockSpec((nb, rows, n), lambda c: (0, c, 0))
    perms = []
    if lc > 1:
        r = jnp.arange(bn * lc)
        perms = [(((r % bn) * lc + r // bn)[:, None] == r[None, :]).astype(BF16)]
    return pl.pallas_call(
        functools.partial(_s5_kernel, bn=bn, lc=lc, precise=precise),
        grid=(nchunk,),
        in_specs=[blk(D_SSM), blk(D_MODEL), full(h0), full(ar), full(ai), full(bb), full(cc), full(d),
                  full(wglu), full(wpb)] + [full(p) for p in perms],
        out_specs=[blk(D_MODEL), full(h0)],
        out_shape=[jax.ShapeDtypeStruct((nb, s, D_MODEL), F32 if precise else BF16),
                   jax.ShapeDtypeStruct(h0.shape, F32)],
        scratch_shapes=[pltpu.VMEM((2 * N_STATE // LANES, bn * lc, LANES), F32),
                        pltpu.VMEM((D_SSM // LANES, bn * lc, LANES), F32),
                        pltpu.VMEM((bn, 2 * N_STATE), F32)],
        compiler_params=_cparams("arbitrary"),
        name="s5",
    )(u3, gb3, h0, ar, ai, bb, cc, d, wglu, wpb, *perms)


def _s5_weights(bbt_re, bbt_im, c_re, c_im):
    g, p, h = N_SSM_GROUPS, SSM_STATE, SSM_GROUP
    same_group = (jnp.arange(g * h)[:, None] // h) == (jnp.arange(g * p)[None, :] // p)

    def bfull(bt):
        return jnp.where(same_group, jnp.tile(bt, (g, 1)), 0.0)

    bre, bim = bfull(bbt_re), bfull(bbt_im)
    tiles = []
    for j in range(16):
        sec, jj = divmod(j, 4)
        src = bre if sec % 2 == 0 else bim
        gbase = (sec // 2) * 16 + jj * 4
        lt = (gbase * h) // LANES
        tiles.append(src[LANES * lt:LANES * (lt + 1), gbase * p:(gbase + 4) * p])
    bb = jnp.stack(tiles)

    def cfull(c):
        ct = c.transpose(0, 2, 1).reshape(g * p, h)
        return jnp.where(same_group.T, jnp.tile(ct, (1, g)), 0.0)

    cre, cim = cfull(c_re), cfull(c_im)
    halves = []
    for hh in range(2):
        rs = slice(HALF_STATE * hh, HALF_STATE * (hh + 1))
        cs = slice(256 * hh, 256 * (hh + 1))
        halves.append(jnp.concatenate([cre[rs, cs], -cim[rs, cs]], axis=0))
    return bb, jnp.stack(halves)


def _state_to_lanes(re, im):
    b = re.shape[0]
    r = re.reshape(b, 2, HALF_STATE)
    i = im.reshape(b, 2, HALF_STATE)
    return jnp.concatenate([r[:, 0], i[:, 0], r[:, 1], i[:, 1]], axis=1)


def _lanes_to_state(h):
    b = h.shape[0]
    h4 = h.reshape(b, 4, HALF_STATE)
    re = jnp.concatenate([h4[:, 0], h4[:, 2]], axis=1).reshape(b, N_SSM_GROUPS, SSM_STATE)
    im = jnp.concatenate([h4[:, 1], h4[:, 3]], axis=1).reshape(b, N_SSM_GROUPS, SSM_STATE)
    return re, im


MOBA_CHUNK = 2


def _moba_kernel(q_ref, k_ref, v_ref, o_ref, ka0_ref, ka1_ref, selb_ref, *, seq):
    blk, hd = MOBA_BLOCK, HEAD_DIM_A
    nblk = seq // blk
    chunk_rows = MOBA_CHUNK * blk

    k = k_ref[...]
    rblk = lax.broadcasted_iota(I32, (seq, LANES), 0) >> BLOCK_SHIFT
    lane = lax.broadcasted_iota(I32, (seq, LANES), 1)
    kf = k.astype(F32)
    ka0_ref[...] = jnp.where(lane < hd, kf, jnp.where(lane - hd == rblk, 1.0, 0.0)).astype(BF16)
    ka1_ref[...] = jnp.where(lane >= hd, kf, jnp.where(lane == rblk, 1.0, 0.0)).astype(BF16)
    r = lax.broadcasted_iota(I32, (LANES, seq), 0)
    cblk = lax.broadcasted_iota(I32, (LANES, seq), 1) >> BLOCK_SHIFT
    ind = jnp.where((r == cblk) | (r - hd == cblk), 1.0, 0.0).astype(BF16)
    kmean = _dot(ind, k) * (1.0 / blk)
    rr = lax.broadcasted_iota(I32, (LANES, LANES), 0)
    ll = lax.broadcasted_iota(I32, (LANES, LANES), 1)
    keep = ((rr < hd) & (ll >= hd)) | ((rr >= hd) & (ll < hd))
    ahi, alo = _split_bf16(jnp.where(keep, kmean, 0.0))

    qhi, qlo = _split_bf16(q_ref[...])
    sc = _dot_nt(qhi, ahi) + _dot_nt(qlo, ahi) + _dot_nt(qhi, alo)
    sct = sc.T
    blk_i = lax.broadcasted_iota(I32, (nblk, seq), 0)
    blk_f = blk_i.astype(F32)
    own = lax.broadcasted_iota(I32, (nblk, seq), 1) >> BLOCK_SHIFT
    valid = blk_i < own
    biases = []
    for x in (sct[0:nblk], sct[hd:hd + nblk]):
        taken = jnp.zeros((nblk, seq), jnp.bool_)
        for _ in range(MOBA_TOPK):
            sm = jnp.where(valid & jnp.logical_not(taken), x, BIG_NEG)
            mx = jnp.max(sm, axis=0, keepdims=True)
            idx = jnp.min(jnp.where((sm == mx) & (sm > BIG_NEG), blk_f, NO_IDX), axis=0, keepdims=True)
            taken = taken | (blk_f == idx)
        biases.append(jnp.where(taken | (blk_i == own), 0.0, NEG_INF))
    pad = jnp.zeros((hd - nblk, seq), F32)
    selb_ref[...] = jnp.concatenate([biases[0], pad, biases[1], pad], axis=0).T.astype(BF16)

    lane_b = lax.broadcasted_iota(I32, (blk, LANES), 1)
    col_minus_row = (lax.broadcasted_iota(I32, (blk, chunk_rows), 1)
                     - lax.broadcasted_iota(I32, (blk, chunk_rows), 0))
    heads = ((ka0_ref, lane_b < hd), (ka1_ref, lane_b >= hd))
    chunk_shift = MOBA_CHUNK.bit_length() - 1

    def qblock(i, _):
        off = pl.multiple_of(i * blk, blk)
        qs = q_ref[pl.ds(off, blk), :] * (hd ** -0.5)
        sb = selb_ref[pl.ds(off, blk), :].astype(F32)
        qas = [jnp.where(mine, qs, sb).astype(BF16) for _, mine in heads]

        dc = i >> chunk_shift
        thr = (i - (dc << chunk_shift)) * blk

        def attend(n_chunks):
            def run():
                outs = []
                for (ka_ref, _), qa in zip(heads, qas):
                    ss = [_dot_nt(qa, ka_ref[c * chunk_rows:(c + 1) * chunk_rows, :]) for c in range(n_chunks)]
                    ss[-1] = jnp.where(col_minus_row <= thr, ss[-1], NEG_INF)
                    m = functools.reduce(jnp.maximum, [jnp.max(s, axis=1, keepdims=True) for s in ss])
                    ps = [jnp.exp(s - m) for s in ss]
                    l = functools.reduce(lambda a, b: a + b, [jnp.sum(p, axis=1, keepdims=True) for p in ps])
                    pv = _dot(jnp.concatenate([p.astype(BF16) for p in ps], axis=1),
                              v_ref[0:n_chunks * chunk_rows, :])
                    outs.append(pv / l)
                return jnp.where(lane_b < hd, outs[0], outs[1]).astype(BF16)
            return run

        o_ref[pl.ds(off, blk), :] = lax.switch(dc, [attend(n) for n in range(1, nblk // MOBA_CHUNK + 1)])
        return 0

    lax.fori_loop(0, nblk, qblock, 0)


def _moba_prompt(q, kb, vb, batch, seq):
    spec = pl.BlockSpec((seq, LANES), lambda b, hp: (b, hp))
    return pl.pallas_call(
        functools.partial(_moba_kernel, seq=seq),
        grid=(batch, D_ATTN // LANES),
        in_specs=[spec, spec, spec],
        out_specs=spec,
        out_shape=jax.ShapeDtypeStruct((batch * seq, D_ATTN), BF16),
        scratch_shapes=[pltpu.VMEM((seq, LANES), BF16)] * 3,
        compiler_params=_cparams("parallel", "parallel"),
        name="moba_prompt",
    )(q, kb, vb)


def _memkv_kernel(m_ref, g_ref, wk_ref, wv_ref, mk_ref, mv_ref):
    mn = _rms(m_ref[...], g_ref[...]).astype(BF16)
    mk_ref[...] = _dot(mn, wk_ref[...])
    mv_ref[...] = _dot(mn, wv_ref[...])


def _memkv(mem, g, wk, wv, tm):
    t = mem.shape[0]
    row = lambda n: pl.BlockSpec((tm, n), lambda i: (i, 0))
    full = lambda a: pl.BlockSpec(a.shape, lambda i: (0,) * a.ndim)
    shp = jax.ShapeDtypeStruct((t, D_XATTN), F32)
    return pl.pallas_call(
        _memkv_kernel, grid=(t // tm,),
        in_specs=[row(D_MODEL), full(g), full(wk), full(wv)],
        out_specs=[row(D_XATTN)] * 2, out_shape=[shp, shp],
        compiler_params=_cparams("parallel"), name="memkv",
    )(mem, g, wk, wv)


def _merge_kernel(x_ref, at_ref, sga_ref, sb_ref, wpa_ref, wo_ref, gx_ref, wxq_ref, x1_ref, xq_ref, *, precise):
    pa = _mm(at_ref[...], wpa_ref[...], precise)
    merged = sga_ref[...].astype(F32) * pa + sb_ref[...].astype(F32)
    x1 = x_ref[...] + _mm(merged, wo_ref[...], precise)
    x1_ref[...] = x1
    xq_ref[...] = _mm(_rms(x1, gx_ref[...]), wxq_ref[...], precise).astype(xq_ref.dtype)


def _merge_sample(x, attn, sga, sb, wpa, wo, gx, wxq):
    t = x.shape[0]
    full = lambda a: pl.BlockSpec(a.shape, lambda i: (0,) * a.ndim)
    args = (x, attn, sga, sb, wpa, wo, gx, wxq)
    return pl.pallas_call(
        functools.partial(_merge_kernel, precise=True), grid=(1,),
        in_specs=[full(a) for a in args],
        out_specs=[pl.BlockSpec((t, D_MODEL), lambda i: (0, 0)), pl.BlockSpec((t, D_XATTN), lambda i: (0, 0))],
        out_shape=[jax.ShapeDtypeStruct((t, D_MODEL), F32), jax.ShapeDtypeStruct((t, D_XATTN), F32)],
        compiler_params=_cparams("arbitrary"), name="merge_sample",
    )(*args)


def _xattn_kernel(q_ref, mk_ref, mv_ref, o_ref):
    q = q_ref[0]
    mk = mk_ref[0].astype(BF16)
    mv = mv_ref[0].astype(BF16)
    outs = []
    for h in range(N_HEADS_X):
        cs = slice(HEAD_DIM_X * h, HEAD_DIM_X * (h + 1))
        s = _dot_nt(q[:, cs], mk[:, cs]) * (HEAD_DIM_X ** -0.5)
        e = jnp.exp(s - jnp.max(s, axis=1, keepdims=True))
        p = e / jnp.sum(e, axis=1, keepdims=True)
        outs.append(_dot(p.astype(BF16), mv[:, cs]))
    o_ref[0] = jnp.concatenate(outs, axis=1).astype(o_ref.dtype)


def _xattn_one_kernel(q_ref, mk_ref, mv_ref, o_ref):
    q = q_ref[...]
    s = jnp.sum(mk_ref[0] * q, axis=-1, keepdims=True) * (HEAD_DIM_X ** -0.5)
    e = jnp.exp(s - jnp.max(s, axis=0, keepdims=True))
    p = e / jnp.sum(e, axis=0, keepdims=True)
    o_ref[0] = jnp.sum(p * mv_ref[0], axis=0)


def _xattn_one(q3, mk4, mv4):
    b = q3.shape[0]
    qspec = pl.BlockSpec((1, N_HEADS_X, HEAD_DIM_X), lambda bi: (bi, 0, 0))
    mspec = pl.BlockSpec((1, N_MEM, N_HEADS_X, HEAD_DIM_X), lambda bi: (bi, 0, 0, 0))
    return pl.pallas_call(
        _xattn_one_kernel, grid=(b,), in_specs=[qspec, mspec, mspec], out_specs=qspec,
        out_shape=jax.ShapeDtypeStruct(q3.shape, F32),
        compiler_params=_cparams("parallel"), name="xattn_one",
    )(q3, mk4, mv4)


GROUP_LANE0 = N_EXPERTS


def _post_kernel(x1_ref, xo_ref, wxo_ref, gf_ref, wr_ref, br_ref, x2_ref, xn_ref, cmb_ref, *, precise):
    x2 = x1_ref[...] + _mm(xo_ref[...], wxo_ref[...], precise)
    x2_ref[...] = x2
    t = _rms(x2, gf_ref[...])
    thi, tlo = _split_bf16(t)
    xn_ref[...] = thi
    if precise:
        logits = _mm(t, wr_ref[...], True) + br_ref[...]
    else:
        wrh, wrl = _split_bf16(wr_ref[...])
        logits = _dot(thi, wrh) + _dot(tlo, wrh) + _dot(thi, wrl) + br_ref[...]

    lane_i = lax.broadcasted_iota(I32, logits.shape, 1)
    lane = lane_i.astype(F32)
    lane_group = (lane_i >> (EXPERTS_PER_GROUP.bit_length() - 1)).astype(F32)
    isg = (lane_i >= GROUP_LANE0) & (lane_i < GROUP_LANE0 + N_EXPERT_GROUPS)
    gmax = jnp.max(jnp.where(isg, logits, BIG_NEG), axis=1, keepdims=True)
    eg = jnp.where(isg, jnp.exp(jnp.where(isg, logits, gmax) - gmax), 0.0)
    gp = eg / jnp.sum(eg, axis=1, keepdims=True)
    pg = jnp.max(jnp.where(isg, gp, -1.0), axis=1, keepdims=True)
    gi = jnp.min(jnp.where(isg & (gp == pg), lane, NO_IDX), axis=1, keepdims=True) - GROUP_LANE0

    insel = (lane_i < N_EXPERTS) & (lane_group == gi)
    el = jnp.where(insel, logits, BIG_NEG)
    m1 = jnp.max(el, axis=1, keepdims=True)
    i1 = jnp.min(jnp.where(insel & (el == m1), lane, NO_IDX), axis=1, keepdims=True)
    rest = insel & (lane != i1)
    el2 = jnp.where(rest, logits, BIG_NEG)
    m2 = jnp.max(el2, axis=1, keepdims=True)
    i2 = jnp.min(jnp.where(rest & (el2 == m2), lane, NO_IDX), axis=1, keepdims=True)
    e2 = jnp.exp(m2 - m1)
    den = 1.0 + e2
    cmb_ref[...] = jnp.where(lane == i1, (1.0 / den) * pg, jnp.where(lane == i2, (e2 / den) * pg, 0.0))


def _mid_kernel(x_ref, at_ref, sga_ref, sb_ref, mk_ref, mv_ref, wpa_ref, wo_ref, gx_ref, wxq_ref, wxo_ref,
                gf_ref, wr_ref, br_ref, x2_ref, xn_ref, cmb_ref, x1_s, xq_s, xo_s):
    _merge_kernel(x_ref, at_ref, sga_ref, sb_ref, wpa_ref, wo_ref, gx_ref, wxq_ref, x1_s, xq_s.at[0],
                  precise=False)
    _xattn_kernel(xq_s, mk_ref, mv_ref, xo_s)
    _post_kernel(x1_s, xo_s.at[0], wxo_ref, gf_ref, wr_ref, br_ref, x2_ref, xn_ref, cmb_ref, precise=False)


def _mid(x, attn, sga, sb, mk3, mv3, w, batch, seq, tm):
    tiles = seq // tm
    row = lambda n: pl.BlockSpec((tm, n), lambda b, s: (b * tiles + s, 0))
    full = lambda a: pl.BlockSpec(a.shape, lambda b, s: (0,) * a.ndim)
    mspec = pl.BlockSpec((1, N_MEM, D_XATTN), lambda b, s: (b, 0, 0))
    t = batch * seq
    weights = [w[k] for k in ('wpa', 'wo', 'gx', 'wxq', 'wxo', 'gf', 'wr', 'br')]
    return pl.pallas_call(
        _mid_kernel, grid=(batch, tiles),
        in_specs=[row(D_MODEL), row(D_ATTN), row(D_MODEL), row(D_MODEL), mspec, mspec]
        + [full(a) for a in weights],
        out_specs=[row(D_MODEL), row(D_MODEL), row(LANES)],
        out_shape=[jax.ShapeDtypeStruct((t, D_MODEL), F32), jax.ShapeDtypeStruct((t, D_MODEL), BF16),
                   jax.ShapeDtypeStruct((t, LANES), F32)],
        scratch_shapes=[pltpu.VMEM((tm, D_MODEL), F32), pltpu.VMEM((1, tm, D_XATTN), BF16),
                        pltpu.VMEM((1, tm, D_XATTN), BF16)],
        compiler_params=_cparams("parallel", "parallel"), name="mid",
    )(x, attn, sga, sb, mk3, mv3, *weights)


def _post_sample(x1, xo, wxo, gf, wr, br):
    t = x1.shape[0]
    full = lambda a: pl.BlockSpec(a.shape, lambda i: (0,) * a.ndim)
    out = lambda n: pl.BlockSpec((t, n), lambda i: (0, 0))
    args = (x1, xo, wxo, gf, wr, br)
    return pl.pallas_call(
        functools.partial(_post_kernel, precise=True), grid=(1,),
        in_specs=[full(a) for a in args],
        out_specs=[out(D_MODEL), out(D_MODEL), out(LANES)],
        out_shape=[jax.ShapeDtypeStruct((t, D_MODEL), F32), jax.ShapeDtypeStruct((t, D_MODEL), BF16),
                   jax.ShapeDtypeStruct((t, LANES), F32)],
        compiler_params=_cparams("arbitrary"), name="post_sample",
    )(*args)


EXPERTS_PER_STEP = 2
PAGES_PER_STEP = 32
PAGES_PER_BLOCK = MOBA_BLOCK // PAGE_SIZE
BLOCKS_PER_STEP = PAGES_PER_STEP // PAGES_PER_BLOCK


def _block_score_step(s, pages, qb_ref, sc_ref):
    lane = lax.broadcasted_iota(I32, (N_HEADS_A, LANES), 1)
    sc = sc_ref[0]
    qb = qb_ref[0]
    for r in range(BLOCKS_PER_STEP):
        acc = pages[PAGES_PER_BLOCK * r][0].reshape(D_ATTN, PAGE_SIZE)
        for t in range(1, PAGES_PER_BLOCK):
            acc = acc + pages[PAGES_PER_BLOCK * r + t][0].reshape(D_ATTN, PAGE_SIZE)
        prod = (acc * qb).reshape(N_HEADS_A, HEAD_DIM_A // 8, 8, PAGE_SIZE)
        per_head = jnp.sum(jnp.sum(prod, axis=1), axis=1)
        sc = jnp.where(lane == s * BLOCKS_PER_STEP + r, jnp.sum(per_head, axis=1, keepdims=True), sc)
    sc_ref[0] = sc


def _moe_stream_kernel(pt_ref, x2_ref, xn_ref, cmb_ref, w13_ref, w2_ref, gfin_ref, qb_ref, *refs,
                       steps_per_seq):
    del pt_ref
    pages, (y_ref, sc_ref, acc_ref) = refs[:PAGES_PER_STEP], refs[PAGES_PER_STEP:]
    s = lax.rem(pl.program_id(0) * pl.num_programs(1) + pl.program_id(1), steps_per_seq)

    @pl.when(s == 0)
    def _():
        sc_ref[0] = jnp.zeros((N_HEADS_A, LANES), F32)

    _moe_kernel(x2_ref, xn_ref, cmb_ref, w13_ref, w2_ref, gfin_ref, y_ref, acc_ref,
                side_work=functools.partial(_block_score_step, s, pages, qb_ref, sc_ref))


def _moe_kernel(x2_ref, xn_ref, cmb_ref, w13_ref, w2_ref, gfin_ref, y_ref, acc_ref, side_work=None):
    step = pl.program_id(1)

    @pl.when(step == 0)
    def _():
        acc_ref[...] = jnp.zeros_like(acc_ref)

    if side_work is not None:
        side_work()
    xn = xn_ref[...]
    cmb = cmb_ref[...]
    lane = lax.broadcasted_iota(I32, cmb.shape, 1)
    hds = []
    for k in range(EXPERTS_PER_STEP):
        h = _dot(xn, w13_ref[k])
        cw = jnp.sum(jnp.where(lane == step * EXPERTS_PER_STEP + k, cmb, 0.0), axis=1, keepdims=True)
        hds.append((jax.nn.silu(h[:, :D_FF_EXPERT]) * h[:, D_FF_EXPERT:] * cw).astype(BF16))
    w2 = w2_ref[...].reshape(EXPERTS_PER_STEP * D_FF_EXPERT, D_MODEL)
    acc_ref[...] += _dot(jnp.concatenate(hds, axis=1), w2)

    @pl.when(step == pl.num_programs(1) - 1)
    def _():
        y_ref[...] = _rms(x2_ref[...] + acc_ref[...], gfin_ref[...])


def _moe(x2, xn, cmb, w13, w2, gfin, tm):
    t = x2.shape[0]
    row = lambda n: pl.BlockSpec((tm, n), lambda i, e: (i, 0))
    return pl.pallas_call(
        _moe_kernel, grid=(t // tm, N_EXPERTS // EXPERTS_PER_STEP),
        in_specs=[row(D_MODEL), row(D_MODEL), row(LANES),
                  pl.BlockSpec((EXPERTS_PER_STEP, D_MODEL, 2 * D_FF_EXPERT), lambda i, e: (e, 0, 0)),
                  pl.BlockSpec((EXPERTS_PER_STEP, D_FF_EXPERT, D_MODEL), lambda i, e: (e, 0, 0)),
                  pl.BlockSpec(gfin.shape, lambda i, e: (0, 0))],
        out_specs=row(D_MODEL),
        out_shape=jax.ShapeDtypeStruct((t, D_MODEL), F32),
        scratch_shapes=[pltpu.VMEM((tm, D_MODEL), F32)],
        compiler_params=_cparams("parallel", "arbitrary"), name="moe",
    )(x2, xn, cmb, w13, w2, gfin)


def _moe_with_key_sums(x2, xn, cmb, w13, w2, gfin, tm, cache4, page_table, qb):
    t = x2.shape[0]
    nb, npages = page_table.shape
    steps_per_seq = npages // PAGES_PER_STEP
    n_tiles = t // tm
    n_esteps = N_EXPERTS // EXPERTS_PER_STEP
    assert n_tiles * n_esteps == nb * steps_per_seq and npages // PAGES_PER_BLOCK <= LANES

    def seq_of(i, e):
        return (i * n_esteps + e) // steps_per_seq

    def pspec(r):
        def imap(i, e, pt):
            g = i * n_esteps + e
            return (pt[g // steps_per_seq, lax.rem(g, steps_per_seq) * PAGES_PER_STEP + r], 0, 0, 0)
        return pl.BlockSpec((1, N_HEADS_A, HEAD_DIM_A, PAGE_SIZE), imap)

    row = lambda n: pl.BlockSpec((tm, n), lambda i, e, pt: (i, 0))
    return pl.pallas_call(
        functools.partial(_moe_stream_kernel, steps_per_seq=steps_per_seq),
        grid_spec=pltpu.PrefetchScalarGridSpec(
            num_scalar_prefetch=1, grid=(n_tiles, n_esteps),
            in_specs=[row(D_MODEL), row(D_MODEL), row(LANES),
                      pl.BlockSpec((EXPERTS_PER_STEP, D_MODEL, 2 * D_FF_EXPERT), lambda i, e, pt: (e, 0, 0)),
                      pl.BlockSpec((EXPERTS_PER_STEP, D_FF_EXPERT, D_MODEL), lambda i, e, pt: (e, 0, 0)),
                      pl.BlockSpec(gfin.shape, lambda i, e, pt: (0, 0)),
                      pl.BlockSpec((1, D_ATTN, LANES), lambda i, e, pt: (seq_of(i, e), 0, 0))]
            + [pspec(r) for r in range(PAGES_PER_STEP)],
            out_specs=[row(D_MODEL),
                       pl.BlockSpec((1, N_HEADS_A, LANES), lambda i, e, pt: (seq_of(i, e), 0, 0))],
            scratch_shapes=[pltpu.VMEM((tm, D_MODEL), F32)]),
        out_shape=[jax.ShapeDtypeStruct((t, D_MODEL), F32), jax.ShapeDtypeStruct((nb, N_HEADS_A, LANES), F32)],
        compiler_params=_cparams("arbitrary", "arbitrary"), name="moe_keysums",
    )(page_table, x2, xn, cmb, w13, w2, gfin, qb, *([cache4] * PAGES_PER_STEP))


def _ssel_kernel(sc_ref, o_ref, *, n_past):
    sc = sc_ref[...] * (1.0 / MOBA_BLOCK)
    bl_i = lax.broadcasted_iota(I32, sc.shape, 1)
    bl = bl_i.astype(F32)
    valid = bl_i < n_past
    taken = jnp.zeros(sc.shape, jnp.bool_)
    out = jnp.zeros(sc.shape, F32)
    for r in range(MOBA_TOPK):
        sm = jnp.where(valid & jnp.logical_not(taken), sc, BIG_NEG)
        mx = jnp.max(sm, axis=1, keepdims=True)
        idx = jnp.min(jnp.where((sm == mx) & (sm > BIG_NEG), bl, NO_IDX), axis=1, keepdims=True)
        taken = taken | (bl == idx)
        out = jnp.where(bl_i == r, idx, out)
    o_ref[...] = out.astype(I32)


def _sample_select(scores, n_past):
    return pl.pallas_call(
        functools.partial(_ssel_kernel, n_past=n_past),
        out_shape=jax.ShapeDtypeStruct(scores.shape, I32), name="sample_select",
    )(scores)


PAGES_PER_HEAD = MOBA_TOPK * PAGES_PER_BLOCK
N_SEL_PAGES = N_HEADS_A * PAGES_PER_HEAD


def _sattn_kernel(sel_ref, pt_ref, q_ref, kn_ref, vn_ref, *refs):
    del sel_ref, pt_ref
    kp, vp, o_ref = refs[:N_SEL_PAGES], refs[N_SEL_PAGES:2 * N_SEL_PAGES], refs[2 * N_SEL_PAGES]
    for h in range(N_HEADS_A):
        qc = q_ref[0][:, h:h + 1] * (HEAD_DIM_A ** -0.5)
        kts = [kp[h * PAGES_PER_HEAD + r][0, 0] for r in range(PAGES_PER_HEAD)]
        vts = [vp[h * PAGES_PER_HEAD + r][0, 0] for r in range(PAGES_PER_HEAD)]
        ss = [jnp.sum(kt * qc, axis=0, keepdims=True) for kt in kts]
        s_self = jnp.sum(qc * kn_ref[0][:, h:h + 1], axis=0, keepdims=True)
        mx = s_self
        for s in ss:
            mx = jnp.maximum(mx, jnp.max(s, axis=1, keepdims=True))
        p_self = jnp.exp(s_self - mx)
        den = p_self
        acc = jnp.zeros((HEAD_DIM_A, PAGE_SIZE), F32)
        for s, vt in zip(ss, vts):
            p = jnp.exp(s - mx)
            den = den + jnp.sum(p, axis=1, keepdims=True)
            acc = acc + vt * p
        out = p_self * vn_ref[0][:, h:h + 1] + jnp.sum(acc, axis=1, keepdims=True)
        o_ref[0, :, h:h + 1] = out / den


def _sample_attn(sel_flat, pt_flat, q3, kn3, vn3, ck4, cv4, n_pages):
    nb = q3.shape[0]

    def pspec(slot):
        h, rem = divmod(slot, PAGES_PER_HEAD)
        r, half = divmod(rem, PAGES_PER_BLOCK)

        def imap(b, sel, pt):
            blk = sel[(b * N_HEADS_A + h) * MOBA_TOPK + r]
            return (pt[b * n_pages + blk * PAGES_PER_BLOCK + half], h, 0, 0)

        return pl.BlockSpec((1, 1, HEAD_DIM_A, PAGE_SIZE), imap)

    tok = pl.BlockSpec((1, HEAD_DIM_A, N_HEADS_A), lambda b, sel, pt: (b, 0, 0))
    return pl.pallas_call(
        _sattn_kernel,
        grid_spec=pltpu.PrefetchScalarGridSpec(
            num_scalar_prefetch=2, grid=(nb,),
            in_specs=[tok, tok, tok] + [pspec(s) for s in range(N_SEL_PAGES)] * 2,
            out_specs=tok),
        out_shape=jax.ShapeDtypeStruct((nb, HEAD_DIM_A, N_HEADS_A), F32),
        compiler_params=_cparams("arbitrary"), name="sample_attn",
    )(sel_flat, pt_flat, q3, kn3, vn3, *([ck4] * N_SEL_PAGES), *([cv4] * N_SEL_PAGES))


def _tail_prompt(x, attn, sga, sb, mk3, mv3, w, batch, seq, key_stream):
    x2, xn, cmb = _mid(x, attn, sga, sb, mk3, mv3, w, batch, seq, MID_ROWS)
    return _moe_with_key_sums(x2, xn, cmb, w['w13'], w['w2'], w['gfin'], MOE_ROWS, *key_stream)


def _tail_sample(x, attn, sga, sb, mk4, mv4, w):
    t = x.shape[0]
    x1, xq = _merge_sample(x, attn, sga, sb, w['wpa'], w['wo'], w['gx'], w['wxq'])
    xo = _xattn_one(xq.reshape(t, N_HEADS_X, HEAD_DIM_X), mk4, mv4).reshape(t, D_XATTN)
    x2, xn, cmb = _post_sample(x1, xo, w['wxo'], w['gf'], w['wr'], w['br'])
    return _moe(x2, xn, cmb, w['w13'], w['w2'], w['gfin'], t)


def kernel(x_prompt, x_sample, mem_prompt, cache_k, cache_v, page_table, state_ssm_re, state_ssm_im,
           cache_mem_k, cache_mem_v, g_mix, w_in, ssm_lambda_re, ssm_lambda_im, ssm_log_dt,
           ssm_b_re, ssm_b_im, ssm_c_re, ssm_c_im, ssm_d, w_glu, w_pa, w_pb, w_o, g_x, g_mem,
           w_xq, w_xk, w_xv, w_xo, g_ffn, w_group, b_group, w_erouter, b_erouter, w1, w3, w2, g_final):
    depth = w_in.shape[0]
    assert depth == 1
    l = 0
    bp, sp, _ = x_prompt.shape
    bs, ss, _ = x_sample.shape
    assert ss == 1
    n_pages = page_table.shape[1]
    past_len = n_pages * PAGE_SIZE
    assert past_len % MOBA_BLOCK == 0 and sp % MOBA_BLOCK == 0

    row = lambda a: a.reshape(1, -1).astype(F32)
    bf = lambda a: a.astype(BF16)
    wr = jnp.zeros((D_MODEL, LANES), F32)
    wr = wr.at[:, :N_EXPERTS].set(w_erouter[l]).at[:, GROUP_LANE0:GROUP_LANE0 + N_EXPERT_GROUPS].set(w_group[l])
    br = jnp.zeros((1, LANES), F32)
    br = br.at[0, :N_EXPERTS].set(b_erouter[l]).at[0, GROUP_LANE0:GROUP_LANE0 + N_EXPERT_GROUPS].set(b_group[l])
    ws = dict(wpa=w_pa[l], wo=w_o[l], gx=row(g_x[l]), wxq=w_xq[l], wxo=w_xo[l], gf=row(g_ffn[l]),
              wr=wr, br=br, w13=bf(jnp.concatenate([w1[l], w3[l]], axis=-1)), w2=bf(w2[l]),
              gfin=row(g_final))
    wp = dict(ws, wpa=bf(w_pa[l]), wo=bf(w_o[l]), wxq=bf(w_xq[l]), wxo=bf(w_xo[l]))
    gmix = row(g_mix[l])

    ar, ai, bbt_re, bbt_im = _ssm_params(ssm_lambda_re[l], ssm_lambda_im[l], ssm_log_dt[l], ssm_b_re[l],
                                         ssm_b_im[l])
    bb, cc = _s5_weights(bbt_re, bbt_im, ssm_c_re[l], ssm_c_im[l])
    d_row = row(ssm_d[l])

    tp = bp * sp
    xp = x_prompt.reshape(tp, D_MODEL)
    q, kt, vt, kb, vb, u, sga, gb = _inproj(xp, gmix, bf(w_in[l]), INPROJ_ROWS, bp)
    attn = _moba_prompt(q, kb, vb, bp, sp)
    h0 = jnp.zeros((bp, 2 * N_STATE), F32)
    sb3, ht = _s5(u.reshape(bp, sp, D_SSM), gb.reshape(bp, sp, D_MODEL), h0, ar, ai, bf(bb), bf(cc), d_row,
                  bf(w_glu[l]), bf(w_pb[l]), bn=bp, lc=S5_STEPS, precise=False)
    sr_p, si_p = _lanes_to_state(ht)
    mk_p, mv_p = _memkv(mem_prompt.reshape(bp * N_MEM, D_MODEL), row(g_mem[l]), bf(w_xk[l]), bf(w_xv[l]), MEMKV_ROWS)
    ck4 = jnp.transpose(cache_k[l], (0, 2, 3, 1))
    cv4 = jnp.transpose(cache_v[l], (0, 2, 3, 1))
    xs = x_sample.reshape(bs, D_MODEL)
    proj_s = _inproj_precise(xs, gmix, w_in[l])
    q_s, k_s, v_s = (proj_s[:, D_ATTN * n:D_ATTN * (n + 1)] for n in range(3))
    u_s = proj_s[:, 3 * D_ATTN:GA_COL0]
    sga_s = proj_s[:, GA_COL0:GA_COL0 + D_MODEL]
    gb_s = proj_s[:, GA_COL0 + D_MODEL:]
    qb = jnp.broadcast_to(q_s[:, :, None], (bs, D_ATTN, LANES))
    y_p, scores = _tail_prompt(xp, attn, sga, sb3.reshape(tp, D_MODEL), mk_p.reshape(bp, N_MEM, D_XATTN),
                               mv_p.reshape(bp, N_MEM, D_XATTN), wp, bp, sp, (ck4, page_table, qb))

    sel = _sample_select(scores.reshape(bs * N_HEADS_A, LANES), past_len // MOBA_BLOCK)
    sel_flat = sel[:, :MOBA_TOPK].reshape(-1)
    hsplit = lambda a: a.reshape(bs, N_HEADS_A, HEAD_DIM_A).transpose(0, 2, 1)
    attn_s = _sample_attn(sel_flat, page_table.reshape(-1), hsplit(q_s), hsplit(k_s), hsplit(v_s), ck4, cv4,
                          n_pages).transpose(0, 2, 1)
    h0_s = _state_to_lanes(state_ssm_re[l].reshape(bs, N_STATE), state_ssm_im[l].reshape(bs, N_STATE))
    sb_s, ht_s = _s5(u_s.reshape(1, bs, D_SSM), gb_s.reshape(1, bs, D_MODEL), h0_s, ar, ai, bb, cc, d_row,
                     w_glu[l], w_pb[l], bn=bs, lc=1, precise=True)
    sr_s, si_s = _lanes_to_state(ht_s)
    y_s = _tail_sample(xs, attn_s.reshape(bs, D_ATTN), sga_s, sb_s.reshape(bs, D_MODEL),
                       cache_mem_k[l], cache_mem_v[l], ws)

    kv5 = lambda a, b, s: a.reshape(1, b, s, N_HEADS_A, HEAD_DIM_A)
    kvt5 = lambda a: a.reshape(bp, N_HEADS_A, HEAD_DIM_A, sp).transpose(0, 3, 1, 2)[None]
    st4 = lambda a: a[None]
    mem5 = lambda a: a.reshape(1, bp, N_MEM, N_HEADS_X, HEAD_DIM_X)
    return (y_p.reshape(bp, sp, D_MODEL), y_s.reshape(bs, 1, D_MODEL),
            kvt5(kt), kvt5(vt), st4(sr_p), st4(si_p), mem5(mk_p), mem5(mv_p),
            kv5(k_s, bs, 1), kv5(v_s, bs, 1), st4(sr_s), st4(si_s))
```

```python
import functools

import jax
import jax.numpy as jnp
from jax import lax
from jax.experimental import pallas as pl
from jax.experimental.pallas import tpu as pltpu

F32 = jnp.float32
BF16 = jnp.bfloat16
I32 = jnp.int32

D_MODEL = 1024
N_HEADS_A = 8
HEAD_DIM_A = 64
D_ATTN = N_HEADS_A * HEAD_DIM_A
MOBA_BLOCK = 256
BLOCK_SHIFT = MOBA_BLOCK.bit_length() - 1
MOBA_TOPK = 3
D_SSM = 512
SSM_GROUP = 16
N_SSM_GROUPS = D_SSM // SSM_GROUP
SSM_STATE = 64
N_STATE = N_SSM_GROUPS * SSM_STATE
N_MEM = 256
N_HEADS_X = 4
HEAD_DIM_X = 128
D_XATTN = N_HEADS_X * HEAD_DIM_X
N_EXPERT_GROUPS = 4
EXPERTS_PER_GROUP = 4
N_EXPERTS = N_EXPERT_GROUPS * EXPERTS_PER_GROUP
D_FF_EXPERT = 256
D_IN_PROJ = 3 * D_ATTN + D_SSM + 2 * D_MODEL
RMS_EPS = 1e-6
NEG_INF = -1e30
PAGE_SIZE = 128

LANES = 128
VMEM_LIMIT = 56 * 1024 * 1024
BIG_NEG = -3e38
NO_IDX = 1e9

INPROJ_ROWS = 512
S5_STEPS = 128
MEMKV_ROWS = 256
MID_ROWS = 512
MOE_ROWS = 1024


def _cparams(*sem):
    return pltpu.CompilerParams(dimension_semantics=sem, vmem_limit_bytes=VMEM_LIMIT)


def _rms(x, g):
    return x * lax.rsqrt(jnp.mean(x * x, axis=-1, keepdims=True) + RMS_EPS) * g


def _dot(a, b):
    return jnp.dot(a, b, preferred_element_type=F32)


def _dot_nt(a, b):
    return lax.dot_general(a, b, (((1,), (1,)), ((), ())), preferred_element_type=F32)


def _split_bf16(x):
    hi = x.astype(BF16)
    return hi, (x - hi.astype(F32)).astype(BF16)


def _split3_bf16(x):
    hi, rest = x.astype(BF16), None
    rest = x - hi.astype(F32)
    mid = rest.astype(BF16)
    return hi, mid, (rest - mid.astype(F32)).astype(BF16)


def _mm(a, w, precise, nt=False):
    dot = _dot_nt if nt else _dot
    if not precise:
        return dot(a.astype(BF16), w.astype(BF16))
    ah, am, al = _split3_bf16(a.astype(F32))
    wh, wm, wl = _split3_bf16(w)
    return ((dot(al, wh) + dot(ah, wl)) + (dot(am, wm) + dot(am, wh) + dot(ah, wm))) + dot(ah, wh)


def _inproj_kernel(x_ref, g_ref, w_ref, q_ref, k_ref, v_ref, kb_ref, vb_ref, u_ref, sga_ref, gb_ref):
    xn = _rms(x_ref[...], g_ref[...]).astype(BF16)

    def mm(c0, n):
        return _dot(xn, w_ref[:, c0:c0 + n])

    q_ref[...] = mm(0, D_ATTN)
    k = mm(D_ATTN, D_ATTN)
    k_ref[0] = k.T
    kb_ref[...] = k.astype(BF16)
    v = mm(2 * D_ATTN, D_ATTN)
    v_ref[0] = v.T
    vb_ref[...] = v.astype(BF16)
    u_ref[...] = mm(3 * D_ATTN, D_SSM)
    sga_ref[...] = jax.nn.sigmoid(mm(3 * D_ATTN + D_SSM, D_MODEL)).astype(BF16)
    gb_ref[...] = mm(3 * D_ATTN + D_SSM + D_MODEL, D_MODEL)


def _inproj(x, g, w_bf, tm, batch):
    t = x.shape[0]
    seq = t // batch
    tiles = seq // tm
    row = lambda n: pl.BlockSpec((tm, n), lambda i: (i, 0))
    col = pl.BlockSpec((1, D_ATTN, tm), lambda i: (i // tiles, 0, i % tiles))
    full = lambda a: pl.BlockSpec(a.shape, lambda i: (0,) * a.ndim)
    shp = lambda n, dt: jax.ShapeDtypeStruct((t, n), dt)
    tshp = jax.ShapeDtypeStruct((batch, D_ATTN, seq), F32)
    return pl.pallas_call(
        _inproj_kernel,
        grid=(t // tm,),
        in_specs=[row(D_MODEL), full(g), full(w_bf)],
        out_specs=[row(D_ATTN), col, col, row(D_ATTN), row(D_ATTN), row(D_SSM), row(D_MODEL), row(D_MODEL)],
        out_shape=[shp(D_ATTN, F32), tshp, tshp, shp(D_ATTN, BF16),
                   shp(D_ATTN, BF16), shp(D_SSM, F32), shp(D_MODEL, BF16), shp(D_MODEL, F32)],
        compiler_params=_cparams("parallel"),
        name="inproj",
    )(x, g, w_bf)


INPROJ_COLS = 512
GA_COL0 = 3 * D_ATTN + D_SSM


def _inproj_precise_kernel(x_ref, g_ref, w_ref, o_ref):
    c0 = pl.program_id(0) * INPROJ_COLS
    acc = _mm(_rms(x_ref[...], g_ref[...]), w_ref[...], True)
    is_ga = (c0 >= GA_COL0) & (c0 < GA_COL0 + D_MODEL)

    @pl.when(is_ga)
    def _():
        o_ref[...] = jax.nn.sigmoid(acc)

    @pl.when(jnp.logical_not(is_ga))
    def _():
        o_ref[...] = acc


def _inproj_precise(x, g, w):
    t = x.shape[0]
    full = lambda a: pl.BlockSpec(a.shape, lambda j: (0,) * a.ndim)
    return pl.pallas_call(
        _inproj_precise_kernel,
        grid=(D_IN_PROJ // INPROJ_COLS,),
        in_specs=[full(x), full(g), pl.BlockSpec((D_MODEL, INPROJ_COLS), lambda j: (0, j))],
        out_specs=pl.BlockSpec((t, INPROJ_COLS), lambda j: (0, j)),
        out_shape=jax.ShapeDtypeStruct((t, D_IN_PROJ), F32),
        compiler_params=_cparams("parallel"),
        name="inproj_sample",
    )(x, g, w)


def _ssm_param_kernel(lr_ref, li_ref, dt_ref, br_ref, bi_ref, ar_ref, ai_ref, bbr_ref, bbi_ref):
    lr = jnp.minimum(lr_ref[...], -1e-4)
    li = li_ref[...]
    dt = jnp.exp(dt_ref[...])
    mag = jnp.exp(lr * dt)
    ar = mag * jnp.cos(li * dt)
    ai = mag * jnp.sin(li * dt)
    den = lr * lr + li * li
    nr = ar - 1.0
    cr = (nr * lr + ai * li) / den
    ci = (ai * lr - nr * li) / den
    ar_ref[...] = ar
    ai_ref[...] = ai
    br = br_ref[...]
    bi = bi_ref[...]
    bbr_ref[...] = cr * br - ci * bi
    bbi_ref[...] = cr * bi + ci * br


def _ssm_params(lam_re, lam_im, log_dt, b_re, b_im):
    n = N_STATE
    lr = lam_re.reshape(1, n)
    li = lam_im.reshape(1, n)
    dt = jnp.broadcast_to(log_dt[:, None], (N_SSM_GROUPS, SSM_STATE)).reshape(1, n)
    brt = b_re.reshape(n, SSM_GROUP).T
    bit = b_im.reshape(n, SSM_GROUP).T
    row = jax.ShapeDtypeStruct((1, n), F32)
    mat = jax.ShapeDtypeStruct((SSM_GROUP, n), F32)
    return pl.pallas_call(_ssm_param_kernel, out_shape=[row, row, mat, mat], name="ssm_params")(
        lr, li, dt, brt, bit)


HALF_STATE = N_STATE // 2
SCAN_COLS = 512
SCAN_TILES = SCAN_COLS // 128


def _s5_kernel(u_ref, gb_ref, h0_ref, ar_ref, ai_ref, bb_ref, cc_ref, d_ref, wglu_ref, wpb_ref, *refs,
               bn, lc, precise):
    m = bn * lc
    if lc > 1:
        perm_ref, sb_ref, ht_ref, s_ref, yt_ref, carry_ref = refs
    else:
        sb_ref, ht_ref, s_ref, yt_ref, carry_ref = refs

    @pl.when(pl.program_id(0) == 0)
    def _():
        carry_ref[...] = h0_ref[...]

    u = u_ref[...].reshape(m, D_SSM)
    us = _dot(perm_ref[...], u.astype(BF16)).astype(BF16) if lc > 1 else u
    for j in range(16):
        sec, jj = divmod(j, 4)
        gbase = (sec // 2) * 16 + jj * 4
        lt = (gbase * SSM_GROUP) // LANES
        bu = _mm(us[:, LANES * lt:LANES * (lt + 1)], bb_ref[j], precise)
        s_ref[2 * j] = bu[:, :LANES]
        s_ref[2 * j + 1] = bu[:, LANES:]

    tiles_half = HALF_STATE // LANES
    for h in range(2):
        for c in range(HALF_STATE // SCAN_COLS):
            re_t = [2 * tiles_half * h + SCAN_TILES * c + n for n in range(SCAN_TILES)]
            im_t = [t + tiles_half for t in re_t]
            a_t = [tiles_half * h + SCAN_TILES * c + n for n in range(SCAN_TILES)]
            ars = [jnp.broadcast_to(ar_ref[:, LANES * t:LANES * (t + 1)], (bn, LANES)) for t in a_t]
            ais = [jnp.broadcast_to(ai_ref[:, LANES * t:LANES * (t + 1)], (bn, LANES)) for t in a_t]

            def body(t, carry, re_t=re_t, im_t=im_t, ars=ars, ais=ais):
                rows = pl.ds(pl.multiple_of(t * bn, bn), bn)
                out = []
                for n in range(SCAN_TILES):
                    xr, xi = carry[2 * n], carry[2 * n + 1]
                    nxr = ars[n] * xr - ais[n] * xi + s_ref[re_t[n], rows, :]
                    nxi = ars[n] * xi + ais[n] * xr + s_ref[im_t[n], rows, :]
                    s_ref[re_t[n], rows, :] = nxr
                    s_ref[im_t[n], rows, :] = nxi
                    out += [nxr, nxi]
                return tuple(out)

            x0 = []
            for n in range(SCAN_TILES):
                x0 += [carry_ref[:, LANES * re_t[n]:LANES * (re_t[n] + 1)],
                       carry_ref[:, LANES * im_t[n]:LANES * (im_t[n] + 1)]]
            xs = lax.fori_loop(0, lc, body, tuple(x0), unroll=min(lc, 8))
            for n in range(SCAN_TILES):
                carry_ref[:, LANES * re_t[n]:LANES * (re_t[n] + 1)] = xs[2 * n]
                carry_ref[:, LANES * im_t[n]:LANES * (im_t[n] + 1)] = xs[2 * n + 1]
    ht_ref[...] = carry_ref[...]

    ys = []
    for h in range(2):
        sdt = F32 if precise else BF16
        xh = jnp.concatenate([s_ref[2 * tiles_half * h + n].astype(sdt) for n in range(2 * tiles_half)], axis=1)
        ys.append(_mm(xh, cc_ref[h], precise))
    y = jnp.concatenate(ys, axis=1)
    if lc > 1:
        for n in range(D_SSM // LANES):
            yt_ref[n] = y[:, LANES * n:LANES * (n + 1)]
        y = jnp.concatenate(
            [jnp.concatenate([yt_ref[n, pl.ds(b, lc, stride=bn), :] for n in range(D_SSM // LANES)], axis=1)
             for b in range(bn)], axis=0)
    y = y + d_ref[...] * u
    z = jax.nn.gelu(y)
    s5 = z * jax.nn.sigmoid(_mm(z, wglu_ref[...], precise))
    pb = _mm(s5, wpb_ref[...], precise)
    gb = gb_ref[...].reshape(m, D_MODEL)
    sb_ref[...] = (jax.nn.sigmoid(gb) * pb).astype(sb_ref.dtype).reshape(sb_ref.shape)


def _s5(u3, gb3, h0, ar, ai, bb, cc, d, wglu, wpb, bn, lc, precise):
    nb, s, _ = u3.shape
    rows = bn * lc // nb
    nchunk = s // rows
    full = lambda a: pl.BlockSpec(a.shape, lambda c: (0,) * a.ndim)
    blk = lambda n: pl.BlockSpec((nb, rows, n), lambda c: (0, c, 0))
    perms = []
    if lc > 1:
        r = jnp.arange(bn * lc)
        perms = [(((r % bn) * lc + r // bn)[:, None] == r[None, :]).astype(BF16)]
    return pl.pallas_call(
        functools.partial(_s5_kernel, bn=bn, lc=lc, precise=precise),
        grid=(nchunk,),
        in_specs=[blk(D_SSM), blk(D_MODEL), full(h0), full(ar), full(ai), full(bb), full(cc), full(d),
                  full(wglu), full(wpb)] + [full(p) for p in perms],
        out_specs=[blk(D_MODEL), full(h0)],
        out_shape=[jax.ShapeDtypeStruct((nb, s, D_MODEL), F32 if precise else BF16),
                   jax.ShapeDtypeStruct(h0.shape, F32)],
        scratch_shapes=[pltpu.VMEM((2 * N_STATE // LANES, bn * lc, LANES), F32),
                        pltpu.VMEM((D_SSM // LANES, bn * lc, LANES), F32),
                        pltpu.VMEM((bn, 2 * N_STATE), F32)],
        compiler_params=_cparams("arbitrary"),
        name="s5",
    )(u3, gb3, h0, ar, ai, bb, cc, d, wglu, wpb, *perms)


def _s5_weights(bbt_re, bbt_im, c_re, c_im):
    g, p, h = N_SSM_GROUPS, SSM_STATE, SSM_GROUP
    same_group = (jnp.arange(g * h)[:, None] // h) == (jnp.arange(g * p)[None, :] // p)

    def bfull(bt):
        return jnp.where(same_group, jnp.tile(bt, (g, 1)), 0.0)

    bre, bim = bfull(bbt_re), bfull(bbt_im)
    tiles = []
    for j in range(16):
        sec, jj = divmod(j, 4)
        src = bre if sec % 2 == 0 else bim
        gbase = (sec // 2) * 16 + jj * 4
        lt = (gbase * h) // LANES
        tiles.append(src[LANES * lt:LANES * (lt + 1), gbase * p:(gbase + 4) * p])
    bb = jnp.stack(tiles)

    def cfull(c):
        ct = c.transpose(0, 2, 1).reshape(g * p, h)
        return jnp.where(same_group.T, jnp.tile(ct, (1, g)), 0.0)

    cre, cim = cfull(c_re), cfull(c_im)
    halves = []
    for hh in range(2):
        rs = slice(HALF_STATE * hh, HALF_STATE * (hh + 1))
        cs = slice(256 * hh, 256 * (hh + 1))
        halves.append(jnp.concatenate([cre[rs, cs], -cim[rs, cs]], axis=0))
    return bb, jnp.stack(halves)


def _state_to_lanes(re, im):
    b = re.shape[0]
    r = re.reshape(b, 2, HALF_STATE)
    i = im.reshape(b, 2, HALF_STATE)
    return jnp.concatenate([r[:, 0], i[:, 0], r[:, 1], i[:, 1]], axis=1)


def _lanes_to_state(h):
    b = h.shape[0]
    h4 = h.reshape(b, 4, HALF_STATE)
    re = jnp.concatenate([h4[:, 0], h4[:, 2]], axis=1).reshape(b, N_SSM_GROUPS, SSM_STATE)
    im = jnp.concatenate([h4[:, 1], h4[:, 3]], axis=1).reshape(b, N_SSM_GROUPS, SSM_STATE)
    return re, im


MOBA_CHUNK = 1


def _moba_kernel(q_ref, k_ref, v_ref, o_ref, ka0_ref, ka1_ref, selb_ref, *, seq):
    blk, hd = MOBA_BLOCK, HEAD_DIM_A
    nblk = seq // blk
    chunk_rows = MOBA_CHUNK * blk

    k = k_ref[...]
    rblk = lax.broadcasted_iota(I32, (seq, LANES), 0) >> BLOCK_SHIFT
    lane = lax.broadcasted_iota(I32, (seq, LANES), 1)
    kf = k.astype(F32)
    ka0_ref[...] = jnp.where(lane < hd, kf, jnp.where(lane - hd == rblk, 1.0, 0.0)).astype(BF16)
    ka1_ref[...] = jnp.where(lane >= hd, kf, jnp.where(lane == rblk, 1.0, 0.0)).astype(BF16)
    r = lax.broadcasted_iota(I32, (LANES, seq), 0)
    cblk = lax.broadcasted_iota(I32, (LANES, seq), 1) >> BLOCK_SHIFT
    ind = jnp.where((r == cblk) | (r - hd == cblk), 1.0, 0.0).astype(BF16)
    kmean = _dot(ind, k) * (1.0 / blk)
    rr = lax.broadcasted_iota(I32, (LANES, LANES), 0)
    ll = lax.broadcasted_iota(I32, (LANES, LANES), 1)
    keep = ((rr < hd) & (ll >= hd)) | ((rr >= hd) & (ll < hd))
    ahi, alo = _split_bf16(jnp.where(keep, kmean, 0.0))

    qhi, qlo = _split_bf16(q_ref[...])
    sc = _dot_nt(qhi, ahi) + _dot_nt(qlo, ahi) + _dot_nt(qhi, alo)
    sct = sc.T
    blk_i = lax.broadcasted_iota(I32, (nblk, seq), 0)
    blk_f = blk_i.astype(F32)
    own = lax.broadcasted_iota(I32, (nblk, seq), 1) >> BLOCK_SHIFT
    valid = blk_i < own
    biases = []
    for x in (sct[0:nblk], sct[hd:hd + nblk]):
        taken = jnp.zeros((nblk, seq), jnp.bool_)
        for _ in range(MOBA_TOPK):
            sm = jnp.where(valid & jnp.logical_not(taken), x, BIG_NEG)
            mx = jnp.max(sm, axis=0, keepdims=True)
            idx = jnp.min(jnp.where((sm == mx) & (sm > BIG_NEG), blk_f, NO_IDX), axis=0, keepdims=True)
            taken = taken | (blk_f == idx)
        biases.append(jnp.where(taken | (blk_i == own), 0.0, NEG_INF))
    pad = jnp.zeros((hd - nblk, seq), F32)
    selb_ref[...] = jnp.concatenate([biases[0], pad, biases[1], pad], axis=0).T.astype(BF16)

    lane_b = lax.broadcasted_iota(I32, (blk, LANES), 1)
    col_minus_row = (lax.broadcasted_iota(I32, (blk, chunk_rows), 1)
                     - lax.broadcasted_iota(I32, (blk, chunk_rows), 0))
    heads = ((ka0_ref, lane_b < hd), (ka1_ref, lane_b >= hd))
    chunk_shift = MOBA_CHUNK.bit_length() - 1

    def qblock(i, _):
        off = pl.multiple_of(i * blk, blk)
        qs = q_ref[pl.ds(off, blk), :] * (hd ** -0.5)
        sb = selb_ref[pl.ds(off, blk), :].astype(F32)
        qas = [jnp.where(mine, qs, sb).astype(BF16) for _, mine in heads]

        dc = i >> chunk_shift
        thr = (i - (dc << chunk_shift)) * blk

        def attend(n_chunks):
            def run():
                outs = []
                for (ka_ref, _), qa in zip(heads, qas):
                    ss = [_dot_nt(qa, ka_ref[c * chunk_rows:(c + 1) * chunk_rows, :]) for c in range(n_chunks)]
                    ss[-1] = jnp.where(col_minus_row <= thr, ss[-1], NEG_INF)
                    m = functools.reduce(jnp.maximum, [jnp.max(s, axis=1, keepdims=True) for s in ss])
                    ps = [jnp.exp(s - m) for s in ss]
                    l = functools.reduce(lambda a, b: a + b, [jnp.sum(p, axis=1, keepdims=True) for p in ps])
                    pv = _dot(jnp.concatenate([p.astype(BF16) for p in ps], axis=1),
                              v_ref[0:n_chunks * chunk_rows, :])
                    outs.append(pv / l)
                return jnp.where(lane_b < hd, outs[0], outs[1]).astype(BF16)
            return run

        o_ref[pl.ds(off, blk), :] = lax.switch(dc, [attend(n) for n in range(1, nblk // MOBA_CHUNK + 1)])
        return 0

    lax.fori_loop(0, nblk, qblock, 0)


def _moba_prompt(q, kb, vb, batch, seq):
    spec = pl.BlockSpec((seq, LANES), lambda b, hp: (b, hp))
    return pl.pallas_call(
        functools.partial(_moba_kernel, seq=seq),
        grid=(batch, D_ATTN // LANES),
        in_specs=[spec, spec, spec],
        out_specs=spec,
        out_shape=jax.ShapeDtypeStruct((batch * seq, D_ATTN), BF16),
        scratch_shapes=[pltpu.VMEM((seq, LANES), BF16)] * 3,
        compiler_params=_cparams("parallel", "parallel"),
        name="moba_prompt",
    )(q, kb, vb)


def _memkv_kernel(m_ref, g_ref, wk_ref, wv_ref, mk_ref, mv_ref):
    mn = _rms(m_ref[...], g_ref[...]).astype(BF16)
    mk_ref[...] = _dot(mn, wk_ref[...])
    mv_ref[...] = _dot(mn, wv_ref[...])


def _memkv(mem, g, wk, wv, tm):
    t = mem.shape[0]
    row = lambda n: pl.BlockSpec((tm, n), lambda i: (i, 0))
    full = lambda a: pl.BlockSpec(a.shape, lambda i: (0,) * a.ndim)
    shp = jax.ShapeDtypeStruct((t, D_XATTN), F32)
    return pl.pallas_call(
        _memkv_kernel, grid=(t // tm,),
        in_specs=[row(D_MODEL), full(g), full(wk), full(wv)],
        out_specs=[row(D_XATTN)] * 2, out_shape=[shp, shp],
        compiler_params=_cparams("parallel"), name="memkv",
    )(mem, g, wk, wv)


def _merge_kernel(x_ref, at_ref, sga_ref, sb_ref, wpa_ref, wo_ref, gx_ref, wxq_ref, x1_ref, xq_ref, *, precise):
    pa = _mm(at_ref[...], wpa_ref[...], precise)
    merged = sga_ref[...].astype(F32) * pa + sb_ref[...].astype(F32)
    x1 = x_ref[...] + _mm(merged, wo_ref[...], precise)
    x1_ref[...] = x1
    xq_ref[...] = _mm(_rms(x1, gx_ref[...]), wxq_ref[...], precise).astype(xq_ref.dtype)


def _merge_sample(x, attn, sga, sb, wpa, wo, gx, wxq):
    t = x.shape[0]
    full = lambda a: pl.BlockSpec(a.shape, lambda i: (0,) * a.ndim)
    args = (x, attn, sga, sb, wpa, wo, gx, wxq)
    return pl.pallas_call(
        functools.partial(_merge_kernel, precise=True), grid=(1,),
        in_specs=[full(a) for a in args],
        out_specs=[pl.BlockSpec((t, D_MODEL), lambda i: (0, 0)), pl.BlockSpec((t, D_XATTN), lambda i: (0, 0))],
        out_shape=[jax.ShapeDtypeStruct((t, D_MODEL), F32), jax.ShapeDtypeStruct((t, D_XATTN), F32)],
        compiler_params=_cparams("arbitrary"), name="merge_sample",
    )(*args)


def _xattn_kernel(q_ref, mk_ref, mv_ref, o_ref):
    q = q_ref[0]
    mk = mk_ref[0].astype(BF16)
    mv = mv_ref[0].astype(BF16)
    outs = []
    for h in range(N_HEADS_X):
        cs = slice(HEAD_DIM_X * h, HEAD_DIM_X * (h + 1))
        s = _dot_nt(q[:, cs], mk[:, cs]) * (HEAD_DIM_X ** -0.5)
        e = jnp.exp(s - jnp.max(s, axis=1, keepdims=True))
        p = e / jnp.sum(e, axis=1, keepdims=True)
        outs.append(_dot(p.astype(BF16), mv[:, cs]))
    o_ref[0] = jnp.concatenate(outs, axis=1).astype(o_ref.dtype)


def _xattn_one_kernel(q_ref, mk_ref, mv_ref, o_ref):
    q = q_ref[...]
    s = jnp.sum(mk_ref[0] * q, axis=-1, keepdims=True) * (HEAD_DIM_X ** -0.5)
    e = jnp.exp(s - jnp.max(s, axis=0, keepdims=True))
    p = e / jnp.sum(e, axis=0, keepdims=True)
    o_ref[0] = jnp.sum(p * mv_ref[0], axis=0)


def _xattn_one(q3, mk4, mv4):
    b = q3.shape[0]
    qspec = pl.BlockSpec((1, N_HEADS_X, HEAD_DIM_X), lambda bi: (bi, 0, 0))
    mspec = pl.BlockSpec((1, N_MEM, N_HEADS_X, HEAD_DIM_X), lambda bi: (bi, 0, 0, 0))
    return pl.pallas_call(
        _xattn_one_kernel, grid=(b,), in_specs=[qspec, mspec, mspec], out_specs=qspec,
        out_shape=jax.ShapeDtypeStruct(q3.shape, F32),
        compiler_params=_cparams("parallel"), name="xattn_one",
    )(q3, mk4, mv4)


GROUP_LANE0 = N_EXPERTS


def _post_kernel(x1_ref, xo_ref, wxo_ref, gf_ref, wr_ref, br_ref, x2_ref, xn_ref, cmb_ref, *, precise):
    x2 = x1_ref[...] + _mm(xo_ref[...], wxo_ref[...], precise)
    x2_ref[...] = x2
    t = _rms(x2, gf_ref[...])
    thi, tlo = _split_bf16(t)
    xn_ref[...] = thi
    if precise:
        logits = _mm(t, wr_ref[...], True) + br_ref[...]
    else:
        wrh, wrl = _split_bf16(wr_ref[...])
        logits = _dot(thi, wrh) + _dot(tlo, wrh) + _dot(thi, wrl) + br_ref[...]

    lane_i = lax.broadcasted_iota(I32, logits.shape, 1)
    lane = lane_i.astype(F32)
    lane_group = (lane_i >> (EXPERTS_PER_GROUP.bit_length() - 1)).astype(F32)
    isg = (lane_i >= GROUP_LANE0) & (lane_i < GROUP_LANE0 + N_EXPERT_GROUPS)
    gmax = jnp.max(jnp.where(isg, logits, BIG_NEG), axis=1, keepdims=True)
    eg = jnp.where(isg, jnp.exp(jnp.where(isg, logits, gmax) - gmax), 0.0)
    gp = eg / jnp.sum(eg, axis=1, keepdims=True)
    pg = jnp.max(jnp.where(isg, gp, -1.0), axis=1, keepdims=True)
    gi = jnp.min(jnp.where(isg & (gp == pg), lane, NO_IDX), axis=1, keepdims=True) - GROUP_LANE0

    insel = (lane_i < N_EXPERTS) & (lane_group == gi)
    el = jnp.where(insel, logits, BIG_NEG)
    m1 = jnp.max(el, axis=1, keepdims=True)
    i1 = jnp.min(jnp.where(insel & (el == m1), lane, NO_IDX), axis=1, keepdims=True)
    rest = insel & (lane != i1)
    el2 = jnp.where(rest, logits, BIG_NEG)
    m2 = jnp.max(el2, axis=1, keepdims=True)
    i2 = jnp.min(jnp.where(rest & (el2 == m2), lane, NO_IDX), axis=1, keepdims=True)
    e2 = jnp.exp(m2 - m1)
    den = 1.0 + e2
    cmb_ref[...] = jnp.where(lane == i1, (1.0 / den) * pg, jnp.where(lane == i2, (e2 / den) * pg, 0.0))


def _mid_kernel(x_ref, at_ref, sga_ref, sb_ref, mk_ref, mv_ref, wpa_ref, wo_ref, gx_ref, wxq_ref, wxo_ref,
                gf_ref, wr_ref, br_ref, x2_ref, xn_ref, cmb_ref, x1_s, xq_s, xo_s):
    _merge_kernel(x_ref, at_ref, sga_ref, sb_ref, wpa_ref, wo_ref, gx_ref, wxq_ref, x1_s, xq_s.at[0],
                  precise=False)
    _xattn_kernel(xq_s, mk_ref, mv_ref, xo_s)
    _post_kernel(x1_s, xo_s.at[0], wxo_ref, gf_ref, wr_ref, br_ref, x2_ref, xn_ref, cmb_ref, precise=False)


def _mid(x, attn, sga, sb, mk3, mv3, w, batch, seq, tm):
    tiles = seq // tm
    row = lambda n: pl.BlockSpec((tm, n), lambda b, s: (b * tiles + s, 0))
    full = lambda a: pl.BlockSpec(a.shape, lambda b, s: (0,) * a.ndim)
    mspec = pl.BlockSpec((1, N_MEM, D_XATTN), lambda b, s: (b, 0, 0))
    t = batch * seq
    weights = [w[k] for k in ('wpa', 'wo', 'gx', 'wxq', 'wxo', 'gf', 'wr', 'br')]
    return pl.pallas_call(
        _mid_kernel, grid=(batch, tiles),
        in_specs=[row(D_MODEL), row(D_ATTN), row(D_MODEL), row(D_MODEL), mspec, mspec]
        + [full(a) for a in weights],
        out_specs=[row(D_MODEL), row(D_MODEL), row(LANES)],
        out_shape=[jax.ShapeDtypeStruct((t, D_MODEL), F32), jax.ShapeDtypeStruct((t, D_MODEL), BF16),
                   jax.ShapeDtypeStruct((t, LANES), F32)],
        scratch_shapes=[pltpu.VMEM((tm, D_MODEL), F32), pltpu.VMEM((1, tm, D_XATTN), BF16),
                        pltpu.VMEM((1, tm, D_XATTN), BF16)],
        compiler_params=_cparams("parallel", "parallel"), name="mid",
    )(x, attn, sga, sb, mk3, mv3, *weights)


def _post_sample(x1, xo, wxo, gf, wr, br):
    t = x1.shape[0]
    full = lambda a: pl.BlockSpec(a.shape, lambda i: (0,) * a.ndim)
    out = lambda n: pl.BlockSpec((t, n), lambda i: (0, 0))
    args = (x1, xo, wxo, gf, wr, br)
    return pl.pallas_call(
        functools.partial(_post_kernel, precise=True), grid=(1,),
        in_specs=[full(a) for a in args],
        out_specs=[out(D_MODEL), out(D_MODEL), out(LANES)],
        out_shape=[jax.ShapeDtypeStruct((t, D_MODEL), F32), jax.ShapeDtypeStruct((t, D_MODEL), BF16),
                   jax.ShapeDtypeStruct((t, LANES), F32)],
        compiler_params=_cparams("arbitrary"), name="post_sample",
    )(*args)


EXPERTS_PER_STEP = 2
PAGES_PER_STEP = 32
PAGES_PER_BLOCK = MOBA_BLOCK // PAGE_SIZE
BLOCKS_PER_STEP = PAGES_PER_STEP // PAGES_PER_BLOCK


def _block_score_step(s, pages, qb_ref, sc_ref):
    lane = lax.broadcasted_iota(I32, (N_HEADS_A, LANES), 1)
    sc = sc_ref[0]
    qb = qb_ref[0]
    for r in range(BLOCKS_PER_STEP):
        acc = pages[PAGES_PER_BLOCK * r][0].reshape(D_ATTN, PAGE_SIZE)
        for t in range(1, PAGES_PER_BLOCK):
            acc = acc + pages[PAGES_PER_BLOCK * r + t][0].reshape(D_ATTN, PAGE_SIZE)
        prod = (acc * qb).reshape(N_HEADS_A, HEAD_DIM_A // 8, 8, PAGE_SIZE)
        per_head = jnp.sum(jnp.sum(prod, axis=1), axis=1)
        sc = jnp.where(lane == s * BLOCKS_PER_STEP + r, jnp.sum(per_head, axis=1, keepdims=True), sc)
    sc_ref[0] = sc


def _moe_stream_kernel(pt_ref, x2_ref, xn_ref, cmb_ref, w13_ref, w2_ref, gfin_ref, qb_ref, *refs,
                       steps_per_seq):
    del pt_ref
    pages, (y_ref, sc_ref, acc_ref) = refs[:PAGES_PER_STEP], refs[PAGES_PER_STEP:]
    s = lax.rem(pl.program_id(0) * pl.num_programs(1) + pl.program_id(1), steps_per_seq)

    @pl.when(s == 0)
    def _():
        sc_ref[0] = jnp.zeros((N_HEADS_A, LANES), F32)

    _moe_kernel(x2_ref, xn_ref, cmb_ref, w13_ref, w2_ref, gfin_ref, y_ref, acc_ref,
                side_work=functools.partial(_block_score_step, s, pages, qb_ref, sc_ref))


def _moe_kernel(x2_ref, xn_ref, cmb_ref, w13_ref, w2_ref, gfin_ref, y_ref, acc_ref, side_work=None):
    step = pl.program_id(1)

    @pl.when(step == 0)
    def _():
        acc_ref[...] = jnp.zeros_like(acc_ref)

    if side_work is not None:
        side_work()
    xn = xn_ref[...]
    cmb = cmb_ref[...]
    lane = lax.broadcasted_iota(I32, cmb.shape, 1)
    hds = []
    for k in range(EXPERTS_PER_STEP):
        h = _dot(xn, w13_ref[k])
        cw = jnp.sum(jnp.where(lane == step * EXPERTS_PER_STEP + k, cmb, 0.0), axis=1, keepdims=True)
        hds.append((jax.nn.silu(h[:, :D_FF_EXPERT]) * h[:, D_FF_EXPERT:] * cw).astype(BF16))
    w2 = w2_ref[...].reshape(EXPERTS_PER_STEP * D_FF_EXPERT, D_MODEL)
    acc_ref[...] += _dot(jnp.concatenate(hds, axis=1), w2)

    @pl.when(step == pl.num_programs(1) - 1)
    def _():
        y_ref[...] = _rms(x2_ref[...] + acc_ref[...], gfin_ref[...])


def _moe(x2, xn, cmb, w13, w2, gfin, tm):
    t = x2.shape[0]
    row = lambda n: pl.BlockSpec((tm, n), lambda i, e: (i, 0))
    return pl.pallas_call(
        _moe_kernel, grid=(t // tm, N_EXPERTS // EXPERTS_PER_STEP),
        in_specs=[row(D_MODEL), row(D_MODEL), row(LANES),
                  pl.BlockSpec((EXPERTS_PER_STEP, D_MODEL, 2 * D_FF_EXPERT), lambda i, e: (e, 0, 0)),
                  pl.BlockSpec((EXPERTS_PER_STEP, D_FF_EXPERT, D_MODEL), lambda i, e: (e, 0, 0)),
                  pl.BlockSpec(gfin.shape, lambda i, e: (0, 0))],
        out_specs=row(D_MODEL),
        out_shape=jax.ShapeDtypeStruct((t, D_MODEL), F32),
        scratch_shapes=[pltpu.VMEM((tm, D_MODEL), F32)],
        compiler_params=_cparams("parallel", "arbitrary"), name="moe",
    )(x2, xn, cmb, w13, w2, gfin)


def _moe_with_key_sums(x2, xn, cmb, w13, w2, gfin, tm, cache4, page_table, qb):
    t = x2.shape[0]
    nb, npages = page_table.shape
    steps_per_seq = npages // PAGES_PER_STEP
    n_tiles = t // tm
    n_esteps = N_EXPERTS // EXPERTS_PER_STEP
    assert n_tiles * n_esteps == nb * steps_per_seq and npages // PAGES_PER_BLOCK <= LANES

    def seq_of(i, e):
        return (i * n_esteps + e) // steps_per_seq

    def pspec(r):
        def imap(i, e, pt):
            g = i * n_esteps + e
            return (pt[g // steps_per_seq, lax.rem(g, steps_per_seq) * PAGES_PER_STEP + r], 0, 0, 0)
        return pl.BlockSpec((1, N_HEADS_A, HEAD_DIM_A, PAGE_SIZE), imap)

    row = lambda n: pl.BlockSpec((tm, n), lambda i, e, pt: (i, 0))
    return pl.pallas_call(
        functools.partial(_moe_stream_kernel, steps_per_seq=steps_per_seq),
        grid_spec=pltpu.PrefetchScalarGridSpec(
            num_scalar_prefetch=1, grid=(n_tiles, n_esteps),
            in_specs=[row(D_MODEL), row(D_MODEL), row(LANES),
                      pl.BlockSpec((EXPERTS_PER_STEP, D_MODEL, 2 * D_FF_EXPERT), lambda i, e, pt: (e, 0, 0)),
                      pl.BlockSpec((EXPERTS_PER_STEP, D_FF_EXPERT, D_MODEL), lambda i, e, pt: (e, 0, 0)),
                      pl.BlockSpec(gfin.shape, lambda i, e, pt: (0, 0)),
                      pl.BlockSpec((1, D_ATTN, LANES), lambda i, e, pt: (seq_of(i, e), 0, 0))]
            + [pspec(r) for r in range(PAGES_PER_STEP)],
            out_specs=[row(D_MODEL),
                       pl.BlockSpec((1, N_HEADS_A, LANES), lambda i, e, pt: (seq_of(i, e), 0, 0))],
            scratch_shapes=[pltpu.VMEM((tm, D_MODEL), F32)]),
        out_shape=[jax.ShapeDtypeStruct((t, D_MODEL), F32), jax.ShapeDtypeStruct((nb, N_HEADS_A, LANES), F32)],
        compiler_params=_cparams("arbitrary", "arbitrary"), name="moe_keysums",
    )(page_table, x2, xn, cmb, w13, w2, gfin, qb, *([cache4] * PAGES_PER_STEP))


def _ssel_kernel(sc_ref, o_ref, *, n_past):
    sc = sc_ref[...] * (1.0 / MOBA_BLOCK)
    bl_i = lax.broadcasted_iota(I32, sc.shape, 1)
    bl = bl_i.astype(F32)
    valid = bl_i < n_past
    taken = jnp.zeros(sc.shape, jnp.bool_)
    out = jnp.zeros(sc.shape, F32)
    for r in range(MOBA_TOPK):
        sm = jnp.where(valid & jnp.logical_not(taken), sc, BIG_NEG)
        mx = jnp.max(sm, axis=1, keepdims=True)
        idx = jnp.min(jnp.where((sm == mx) & (sm > BIG_NEG), bl, NO_IDX), axis=1, keepdims=True)
        taken = taken | (bl == idx)
        out = jnp.where(bl_i == r, idx, out)
    o_ref[...] = out.astype(I32)


def _sample_select(scores, n_past):
    return pl.pallas_call(
        functools.partial(_ssel_kernel, n_past=n_past),
        out_shape=jax.ShapeDtypeStruct(scores.shape, I32), name="sample_select",
    )(scores)


PAGES_PER_HEAD = MOBA_TOPK * PAGES_PER_BLOCK
N_SEL_PAGES = N_HEADS_A * PAGES_PER_HEAD


def _sattn_kernel(sel_ref, pt_ref, q_ref, kn_ref, vn_ref, *refs):
    del sel_ref, pt_ref
    kp, vp, o_ref = refs[:N_SEL_PAGES], refs[N_SEL_PAGES:2 * N_SEL_PAGES], refs[2 * N_SEL_PAGES]
    for h in range(N_HEADS_A):
        qc = q_ref[0][:, h:h + 1] * (HEAD_DIM_A ** -0.5)
        kts = [kp[h * PAGES_PER_HEAD + r][0, 0] for r in range(PAGES_PER_HEAD)]
        vts = [vp[h * PAGES_PER_HEAD + r][0, 0] for r in range(PAGES_PER_HEAD)]
        ss = [jnp.sum(kt * qc, axis=0, keepdims=True) for kt in kts]
        s_self = jnp.sum(qc * kn_ref[0][:, h:h + 1], axis=0, keepdims=True)
        mx = s_self
        for s in ss:
            mx = jnp.maximum(mx, jnp.max(s, axis=1, keepdims=True))
        p_self = jnp.exp(s_self - mx)
        den = p_self
        acc = jnp.zeros((HEAD_DIM_A, PAGE_SIZE), F32)
        for s, vt in zip(ss, vts):
            p = jnp.exp(s - mx)
            den = den + jnp.sum(p, axis=1, keepdims=True)
            acc = acc + vt * p
        out = p_self * vn_ref[0][:, h:h + 1] + jnp.sum(acc, axis=1, keepdims=True)
        o_ref[0, :, h:h + 1] = out / den


def _sample_attn(sel_flat, pt_flat, q3, kn3, vn3, ck4, cv4, n_pages):
    nb = q3.shape[0]

    def pspec(slot):
        h, rem = divmod(slot, PAGES_PER_HEAD)
        r, half = divmod(rem, PAGES_PER_BLOCK)

        def imap(b, sel, pt):
            blk = sel[(b * N_HEADS_A + h) * MOBA_TOPK + r]
            return (pt[b * n_pages + blk * PAGES_PER_BLOCK + half], h, 0, 0)

        return pl.BlockSpec((1, 1, HEAD_DIM_A, PAGE_SIZE), imap)

    tok = pl.BlockSpec((1, HEAD_DIM_A, N_HEADS_A), lambda b, sel, pt: (b, 0, 0))
    return pl.pallas_call(
        _sattn_kernel,
        grid_spec=pltpu.PrefetchScalarGridSpec(
            num_scalar_prefetch=2, grid=(nb,),
            in_specs=[tok, tok, tok] + [pspec(s) for s in range(N_SEL_PAGES)] * 2,
            out_specs=tok),
        out_shape=jax.ShapeDtypeStruct((nb, HEAD_DIM_A, N_HEADS_A), F32),
        compiler_params=_cparams("arbitrary"), name="sample_attn",
    )(sel_flat, pt_flat, q3, kn3, vn3, *([ck4] * N_SEL_PAGES), *([cv4] * N_SEL_PAGES))


def _tail_prompt(x, attn, sga, sb, mk3, mv3, w, batch, seq, key_stream):
    x2, xn, cmb = _mid(x, attn, sga, sb, mk3, mv3, w, batch, seq, MID_ROWS)
    return _moe_with_key_sums(x2, xn, cmb, w['w13'], w['w2'], w['gfin'], MOE_ROWS, *key_stream)


def _tail_sample(x, attn, sga, sb, mk4, mv4, w):
    t = x.shape[0]
    x1, xq = _merge_sample(x, attn, sga, sb, w['wpa'], w['wo'], w['gx'], w['wxq'])
    xo = _xattn_one(xq.reshape(t, N_HEADS_X, HEAD_DIM_X), mk4, mv4).reshape(t, D_XATTN)
    x2, xn, cmb = _post_sample(x1, xo, w['wxo'], w['gf'], w['wr'], w['br'])
    return _moe(x2, xn, cmb, w['w13'], w['w2'], w['gfin'], t)


def kernel(x_prompt, x_sample, mem_prompt, cache_k, cache_v, page_table, state_ssm_re, state_ssm_im,
           cache_mem_k, cache_mem_v, g_mix, w_in, ssm_lambda_re, ssm_lambda_im, ssm_log_dt,
           ssm_b_re, ssm_b_im, ssm_c_re, ssm_c_im, ssm_d, w_glu, w_pa, w_pb, w_o, g_x, g_mem,
           w_xq, w_xk, w_xv, w_xo, g_ffn, w_group, b_group, w_erouter, b_erouter, w1, w3, w2, g_final):
    depth = w_in.shape[0]
    assert depth == 1
    l = 0
    bp, sp, _ = x_prompt.shape
    bs, ss, _ = x_sample.shape
    assert ss == 1
    n_pages = page_table.shape[1]
    past_len = n_pages * PAGE_SIZE
    assert past_len % MOBA_BLOCK == 0 and sp % MOBA_BLOCK == 0

    row = lambda a: a.reshape(1, -1).astype(F32)
    bf = lambda a: a.astype(BF16)
    wr = jnp.zeros((D_MODEL, LANES), F32)
    wr = wr.at[:, :N_EXPERTS].set(w_erouter[l]).at[:, GROUP_LANE0:GROUP_LANE0 + N_EXPERT_GROUPS].set(w_group[l])
    br = jnp.zeros((1, LANES), F32)
    br = br.at[0, :N_EXPERTS].set(b_erouter[l]).at[0, GROUP_LANE0:GROUP_LANE0 + N_EXPERT_GROUPS].set(b_group[l])
    ws = dict(wpa=w_pa[l], wo=w_o[l], gx=row(g_x[l]), wxq=w_xq[l], wxo=w_xo[l], gf=row(g_ffn[l]),
              wr=wr, br=br, w13=bf(jnp.concatenate([w1[l], w3[l]], axis=-1)), w2=bf(w2[l]),
              gfin=row(g_final))
    wp = dict(ws, wpa=bf(w_pa[l]), wo=bf(w_o[l]), wxq=bf(w_xq[l]), wxo=bf(w_xo[l]))
    gmix = row(g_mix[l])

    ar, ai, bbt_re, bbt_im = _ssm_params(ssm_lambda_re[l], ssm_lambda_im[l], ssm_log_dt[l], ssm_b_re[l],
                                         ssm_b_im[l])
    bb, cc = _s5_weights(bbt_re, bbt_im, ssm_c_re[l], ssm_c_im[l])
    d_row = row(ssm_d[l])

    tp = bp * sp
    xp = x_prompt.reshape(tp, D_MODEL)
    q, kt, vt, kb, vb, u, sga, gb = _inproj(xp, gmix, bf(w_in[l]), INPROJ_ROWS, bp)
    attn = _moba_prompt(q, kb, vb, bp, sp)
    h0 = jnp.zeros((bp, 2 * N_STATE), F32)
    sb3, ht = _s5(u.reshape(bp, sp, D_SSM), gb.reshape(bp, sp, D_MODEL), h0, ar, ai, bf(bb), bf(cc), d_row,
                  bf(w_glu[l]), bf(w_pb[l]), bn=bp, lc=S5_STEPS, precise=False)
    sr_p, si_p = _lanes_to_state(ht)
    mk_p, mv_p = _memkv(mem_prompt.reshape(bp * N_MEM, D_MODEL), row(g_mem[l]), bf(w_xk[l]), bf(w_xv[l]), MEMKV_ROWS)
    ck4 = jnp.transpose(cache_k[l], (0, 2, 3, 1))
    cv4 = jnp.transpose(cache_v[l], (0, 2, 3, 1))
    xs = x_sample.reshape(bs, D_MODEL)
    proj_s = _inproj_precise(xs, gmix, w_in[l])
    q_s, k_s, v_s = (proj_s[:, D_ATTN * n:D_ATTN * (n + 1)] for n in range(3))
    u_s = proj_s[:, 3 * D_ATTN:GA_COL0]
    sga_s = proj_s[:, GA_COL0:GA_COL0 + D_MODEL]
    gb_s = proj_s[:, GA_COL0 + D_MODEL:]
    qb = jnp.broadcast_to(q_s[:, :, None], (bs, D_ATTN, LANES))
    y_p, scores = _tail_prompt(xp, attn, sga, sb3.reshape(tp, D_MODEL), mk_p.reshape(bp, N_MEM, D_XATTN),
                               mv_p.reshape(bp, N_MEM, D_XATTN), wp, bp, sp, (ck4, page_table, qb))

    sel = _sample_select(scores.reshape(bs * N_HEADS_A, LANES), past_len // MOBA_BLOCK)
    sel_flat = sel[:, :MOBA_TOPK].reshape(-1)
    hsplit = lambda a: a.reshape(bs, N_HEADS_A, HEAD_DIM_A).transpose(0, 2, 1)
    attn_s = _sample_attn(sel_flat, page_table.reshape(-1), hsplit(q_s), hsplit(k_s), hsplit(v_s), ck4, cv4,
                          n_pages).transpose(0, 2, 1)
    h0_s = _state_to_lanes(state_ssm_re[l].reshape(bs, N_STATE), state_ssm_im[l].reshape(bs, N_STATE))
    sb_s, ht_s = _s5(u_s.reshape(1, bs, D_SSM), gb_s.reshape(1, bs, D_MODEL), h0_s, ar, ai, bb, cc, d_row,
                     w_glu[l], w_pb[l], bn=bs, lc=1, precise=True)
    sr_s, si_s = _lanes_to_state(ht_s)
    y_s = _tail_sample(xs, attn_s.reshape(bs, D_ATTN), sga_s, sb_s.reshape(bs, D_MODEL),
                       cache_mem_k[l], cache_mem_v[l], ws)

    kv5 = lambda a, b, s: a.reshape(1, b, s, N_HEADS_A, HEAD_DIM_A)
    kvt5 = lambda a: a.reshape(bp, N_HEADS_A, HEAD_DIM_A, sp).transpose(0, 3, 1, 2)[None]
    st4 = lambda a: a[None]
    mem5 = lambda a: a.reshape(1, bp, N_MEM, N_HEADS_X, HEAD_DIM_X)
    return (y_p.reshape(bp, sp, D_MODEL), y_s.reshape(bs, 1, D_MODEL),
            kvt5(kt), kvt5(vt), st4(sr_p), st4(si_p), mem5(mk_p), mem5(mv_p),
            kv5(k_s, bs, 1), kv5(v_s, bs, 1), st4(sr_s), st4(si_s))
```
